```python
import math
import jax, jax.numpy as jnp
from jax import lax
import numpy as np

D_MODEL = 1024
BATCH = 2
SEQ = 8192
DEPTH = 2

N_MIXERS = 2
HEAD_DIM = 64
N_MIX_HEADS = 12
N_MEM_HEADS = 4
MEM_LEN = 256
D_MIX = N_MIX_HEADS * HEAD_DIM
D_MEM = N_MEM_HEADS * HEAD_DIM
D_CAT = D_MIX + D_MEM
D_FF = -(-8 * D_MODEL // (3 * 256)) * 256
N_REL_BUCKETS = 32
REL_MAX_EXACT = 16
REL_MAX_DIST = 2048
Q_BLOCK = 128
NSA_KV_HEADS = 2
NSA_GROUP = N_MIX_HEADS // NSA_KV_HEADS
CMP_LEN = 32
CMP_STRIDE = 16
CMP_HIDDEN = 256
SEL_BLOCK = 64
N_SEL = 16
WINDOW = 512
N_BRANCH = 3
FORCE_BONUS = 1e4
NSA_IN = D_MIX + 6 * NSA_KV_HEADS * HEAD_DIM + N_MIX_HEADS * N_BRANCH + D_MEM
Q_LORA = 256
KV_LORA = 128
IDX_HEADS = 8
IDX_DIM = 64
DSA_TOPK = 256
DSA_IN = Q_LORA + KV_LORA + IDX_DIM + IDX_HEADS + D_MEM
N_NSA_LAYERS = (DEPTH + 1) // 2
N_DSA_LAYERS = DEPTH // 2
NEG = -1e30
EPS = 1e-6

kernel_name = 'hybrid_nsa_dsa_memory_trunk'


def rmsnorm(x, g):
    xf = x.astype(jnp.float32)
    y = xf * lax.rsqrt(jnp.mean(xf * xf, axis=-1, keepdims=True) + EPS)
    return (y * g.astype(jnp.float32)).astype(x.dtype)


def rel_bucket(dist):
    n = jnp.maximum(dist, 0)
    nf = jnp.maximum(n, REL_MAX_EXACT).astype(jnp.float32)
    large = REL_MAX_EXACT + (jnp.log(nf / REL_MAX_EXACT) / math.log(REL_MAX_DIST / REL_MAX_EXACT)
                             * (N_REL_BUCKETS - REL_MAX_EXACT)).astype(jnp.int32)
    large = jnp.minimum(large, N_REL_BUCKETS - 1)
    return jnp.where(n < REL_MAX_EXACT, n, large)


def masked_softmax(s, mask):
    s = jnp.where(mask, s, NEG)
    p = jax.nn.softmax(s, axis=-1)
    return jnp.where(mask, p, 0.0)


def split_cols(a, sizes):
    return jnp.split(a, [int(v) for v in np.cumsum(sizes)[:-1]], axis=-1)


def compress(raw, pos, w1, b1, w2, b2):
    B, T = raw.shape[0], raw.shape[1]
    n_cmp = (T - CMP_LEN) // CMP_STRIDE + 1
    idx = jnp.arange(n_cmp)[:, None] * CMP_STRIDE + jnp.arange(CMP_LEN)[None, :]
    blk = raw[:, idx] + pos[None, None, :, None, :]
    blk = jnp.transpose(blk, (0, 1, 3, 2, 4)).reshape(B, n_cmp, NSA_KV_HEADS, CMP_LEN * HEAD_DIM)
    h = jax.nn.gelu(blk @ w1 + b1)
    return h @ w2 + b2


def nsa_mixer(xn, w_in, gate_b, pos_k, pos_v, k_w1, k_b1, k_w2, k_b2, v_w1, v_b1, v_w2, v_b2, rel_bias):
    B, T, _ = xn.shape
    G, R, dh, H = NSA_KV_HEADS, NSA_GROUP, HEAD_DIM, N_MIX_HEADS
    kvw = G * dh
    q, kc_raw, vc_raw, ks, vs, kw, vw, gl, q_mem = split_cols(
        xn @ w_in, [D_MIX, kvw, kvw, kvw, kvw, kvw, kvw, H * N_BRANCH, D_MEM])
    q = q.reshape(B, T, G, R, dh) * dh ** -0.5
    kc = compress(kc_raw.reshape(B, T, G, dh), pos_k, k_w1, k_b1, k_w2, k_b2)
    vc = compress(vc_raw.reshape(B, T, G, dh), pos_v, v_w1, v_b1, v_w2, v_b2)
    gates = jax.nn.sigmoid(gl + gate_b).reshape(B, T, G, R, N_BRANCH)
    n_sel_blk = T // SEL_BLOCK
    kb = jnp.transpose(ks.reshape(B, n_sel_blk, SEL_BLOCK, G, dh), (0, 3, 1, 2, 4))
    vb = jnp.transpose(vs.reshape(B, n_sel_blk, SEL_BLOCK, G, dh), (0, 3, 1, 2, 4))
    pad = ((0, 0), (WINDOW, 0), (0, 0), (0, 0))
    kwp = jnp.pad(kw.reshape(B, T, G, dh), pad)
    vwp = jnp.pad(vw.reshape(B, T, G, dh), pad)

    n_cmp = kc.shape[1]
    cmp_start = jnp.arange(n_cmp) * CMP_STRIDE
    cmp_end = cmp_start + CMP_LEN - 1
    sel_start = jnp.arange(n_sel_blk) * SEL_BLOCK
    overlap = ((cmp_start[:, None] <= sel_start[None, :] + SEL_BLOCK - 1)
               & (cmp_end[:, None] >= sel_start[None, :])).astype(jnp.float32)
    n_top = min(N_SEL, n_sel_blk)
    nk = n_top * SEL_BLOCK
    bias_gr = jnp.transpose(rel_bias.reshape(N_REL_BUCKETS, G, R), (1, 0, 2))
    bi = jnp.arange(B)[:, None, None, None]
    gi = jnp.arange(G)[None, :, None, None]
    in_blk = jnp.arange(SEL_BLOCK)
    blk_ids = jnp.arange(n_sel_blk)

    def block(qb_idx):
        qs = qb_idx * Q_BLOCK
        t = qs + jnp.arange(Q_BLOCK)
        qb = lax.dynamic_slice_in_dim(q, qs, Q_BLOCK, axis=1)
        s = jnp.einsum('bqgrd,bngd->bgrqn', qb, kc).astype(jnp.float32)
        cb = rel_bias[rel_bucket(t[:, None] - cmp_end[None, :])]
        s = s + jnp.transpose(cb, (2, 0, 1)).reshape(G, R, Q_BLOCK, n_cmp)
        p_cmp = masked_softmax(s, cmp_end[None, :] <= t[:, None])
        o_cmp = jnp.einsum('bgrqn,bngd->bqgrd', p_cmp.astype(vc.dtype), vc)
        p_slc = jnp.einsum('bgqn,nj->bgqj', p_cmp.sum(axis=2), overlap)
        cur = t // SEL_BLOCK
        forced = ((blk_ids[None, :] == 0) | (blk_ids[None, :] == cur[:, None])
                  | (blk_ids[None, :] == cur[:, None] - 1))
        admissible = sel_start[None, :] <= t[:, None]
        score = jnp.where(admissible, p_slc + FORCE_BONUS * forced, NEG)
        top_val, top_idx = lax.top_k(score, n_top)
        k_sel = kb[bi, gi, top_idx].reshape(B, G, Q_BLOCK, nk, dh)
        v_sel = vb[bi, gi, top_idx].reshape(B, G, Q_BLOCK, nk, dh)
        pos5 = top_idx[..., None] * SEL_BLOCK + in_blk
        mask5 = (top_val > 0.5 * NEG)[..., None] & (pos5 <= t[None, None, :, None, None])
        pos = pos5.reshape(B, G, Q_BLOCK, nk)
        smask = mask5.reshape(B, G, Q_BLOCK, nk)
        sb = bias_gr[gi, rel_bucket(t[None, None, :, None] - pos)]
        s = jnp.einsum('bqgrd,bgqkd->bgrqk', qb, k_sel).astype(jnp.float32) + jnp.moveaxis(sb, -1, 2)
        p = masked_softmax(s, smask[:, :, None])
        o_slc = jnp.einsum('bgrqk,bgqkd->bqgrd', p.astype(v_sel.dtype), v_sel)
        kwb = lax.dynamic_slice_in_dim(kwp, qs, Q_BLOCK + WINDOW, axis=1)
        vwb = lax.dynamic_slice_in_dim(vwp, qs, Q_BLOCK + WINDOW, axis=1)
        spos = qs - WINDOW + jnp.arange(Q_BLOCK + WINDOW)
        d = t[:, None] - spos[None, :]
        wmask = (spos[None, :] >= 0) & (d >= 0) & (d < WINDOW)
        wb = jnp.transpose(rel_bias[rel_bucket(d)], (2, 0, 1)).reshape(G, R, Q_BLOCK, Q_BLOCK + WINDOW)
        s = jnp.einsum('bqgrd,bkgd->bgrqk', qb, kwb).astype(jnp.float32) + wb
        p = masked_softmax(s, wmask)
        o_win = jnp.einsum('bgrqk,bkgd->bqgrd', p.astype(vwb.dtype), vwb)
        gb = lax.dynamic_slice_in_dim(gates, qs, Q_BLOCK, axis=1)
        return gb[..., 0:1] * o_cmp + gb[..., 1:2] * o_slc + gb[..., 2:3] * o_win

    out = lax.map(block, jnp.arange(T // Q_BLOCK))
    out = jnp.moveaxis(out, 0, 1).reshape(B, T, D_MIX)
    return out, q_mem


def dsa_mixer(xn, w_in, q_norm, kv_norm, w_q_up, w_uk, w_uv, w_q_idx, kidx_norm, rel_bias):
    B, T, _ = xn.shape
    H, dh = N_MIX_HEADS, HEAD_DIM
    c_q, c_kv, k_idx, w_idx, q_mem = split_cols(xn @ w_in, [Q_LORA, KV_LORA, IDX_DIM, IDX_HEADS, D_MEM])
    c_q = rmsnorm(c_q, q_norm)
    c_kv = rmsnorm(c_kv, kv_norm)
    q = (c_q @ w_q_up).reshape(B, T, H, dh) * dh ** -0.5
    q_abs = jnp.einsum('bthd,rhd->bthr', q, w_uk)
    q_idx = (c_q @ w_q_idx).reshape(B, T, IDX_HEADS, IDX_DIM)
    k_idx = rmsnorm(k_idx, kidx_norm)
    w_idx = w_idx * (IDX_HEADS ** -0.5 * IDX_DIM ** -0.5)
    k_top = min(DSA_TOPK, T // 4)
    s_all = jnp.arange(T)
    bi = jnp.arange(B)[:, None, None]

    def block(qb_idx):
        qs = qb_idx * Q_BLOCK
        t = qs + jnp.arange(Q_BLOCK)
        qi = lax.dynamic_slice_in_dim(q_idx, qs, Q_BLOCK, axis=1)
        wi = lax.dynamic_slice_in_dim(w_idx, qs, Q_BLOCK, axis=1)
        qa = lax.dynamic_slice_in_dim(q_abs, qs, Q_BLOCK, axis=1)
        logits = jax.nn.relu(jnp.einsum('bqhd,bsd->bqhs', qi, k_idx).astype(jnp.float32))
        score = jnp.einsum('bqhs,bqh->bqs', logits, wi.astype(jnp.float32))
        score = jnp.where(s_all[None, :] <= t[:, None], score, NEG)
        _, idx = lax.top_k(score, k_top)
        valid = idx <= t[None, :, None]
        c_sel = c_kv[bi, idx]
        s = jnp.einsum('bqhr,bqkr->bhqk', qa, c_sel).astype(jnp.float32)
        bias = rel_bias[rel_bucket(t[None, :, None] - idx)]
        s = s + jnp.transpose(bias, (0, 3, 1, 2))
        p = masked_softmax(s, valid[:, None])
        return jnp.einsum('bhqk,bqkr->bqhr', p.astype(c_sel.dtype), c_sel)

    o_lat = lax.map(block, jnp.arange(T // Q_BLOCK))
    o_lat = jnp.moveaxis(o_lat, 0, 1).reshape(B, T, H, KV_LORA)
    out = jnp.einsum('bthr,rhd->bthd', o_lat, w_uv).reshape(B, T, D_MIX)
    return out, q_mem


def memory_attention(q_mem, mem_n, w_mem_kv):
    B, M, _ = mem_n.shape
    kv = (mem_n @ w_mem_kv).reshape(B, M, 2, N_MEM_HEADS, HEAD_DIM)
    k, v = kv[:, :, 0], kv[:, :, 1]
    s = jnp.einsum('bthd,bmhd->bhtm', q_mem, k).astype(jnp.float32)
    p = jax.nn.softmax(s, axis=-1).astype(v.dtype)
    o = jnp.einsum('bhtm,bmhd->bthd', p, v)
    return o.reshape(o.shape[0], o.shape[1], D_MEM)


def setup_inputs(seed: int = 0) -> dict:
    key = jax.random.key(seed)
    keys = iter(jax.random.split(key, 48))
    D = D_MODEL

    def nrm(shape, scale):
        return jax.random.normal(next(keys), shape, jnp.float32) * scale

    def gain(shape):
        return 1.0 + nrm(shape, 0.01)

    cin = CMP_LEN * HEAD_DIM
    return {
        'x': nrm((BATCH, SEQ, D), 1.0),
        'mem': nrm((BATCH, MEM_LEN, D), 1.0),
        'rel_bias': nrm((N_REL_BUCKETS, N_MIX_HEADS), 0.1),
        'norm_mix': gain((DEPTH, D)),
        'norm_ffn': gain((DEPTH, D)),
        'norm_mem': gain((DEPTH, D)),
        'w_mem_kv': nrm((DEPTH, D, 2 * D_MEM), D ** -0.5),
        'w_out': nrm((DEPTH, D_CAT, D), D_CAT ** -0.5),
        'ffn_gate': nrm((DEPTH, D, D_FF), D ** -0.5),
        'ffn_up': nrm((DEPTH, D, D_FF), D ** -0.5),
        'ffn_down': nrm((DEPTH, D_FF, D), D_FF ** -0.5),
        'nsa_w_in': nrm((N_NSA_LAYERS, D, NSA_IN), D ** -0.5),
        'nsa_gate_b': nrm((N_NSA_LAYERS, N_MIX_HEADS * N_BRANCH), 0.01),
        'nsa_cmp_pos_k': nrm((N_NSA_LAYERS, CMP_LEN, HEAD_DIM), 0.1),
        'nsa_cmp_pos_v': nrm((N_NSA_LAYERS, CMP_LEN, HEAD_DIM), 0.1),
        'nsa_cmp_k_w1': nrm((N_NSA_LAYERS, cin, CMP_HIDDEN), cin ** -0.5),
        'nsa_cmp_k_b1': nrm((N_NSA_LAYERS, CMP_HIDDEN), 0.01),
        'nsa_cmp_k_w2': nrm((N_NSA_LAYERS, CMP_HIDDEN, HEAD_DIM), CMP_HIDDEN ** -0.5),
        'nsa_cmp_k_b2': nrm((N_NSA_LAYERS, HEAD_DIM), 0.01),
        'nsa_cmp_v_w1': nrm((N_NSA_LAYERS, cin, CMP_HIDDEN), cin ** -0.5),
        'nsa_cmp_v_b1': nrm((N_NSA_LAYERS, CMP_HIDDEN), 0.01),
        'nsa_cmp_v_w2': nrm((N_NSA_LAYERS, CMP_HIDDEN, HEAD_DIM), CMP_HIDDEN ** -0.5),
        'nsa_cmp_v_b2': nrm((N_NSA_LAYERS, HEAD_DIM), 0.01),
        'dsa_w_in': nrm((N_DSA_LAYERS, D, DSA_IN), D ** -0.5),
        'dsa_q_norm': gain((N_DSA_LAYERS, Q_LORA)),
        'dsa_kv_norm': gain((N_DSA_LAYERS, KV_LORA)),
        'dsa_w_q_up': nrm((N_DSA_LAYERS, Q_LORA, D_MIX), Q_LORA ** -0.5),
        'dsa_w_uk': nrm((N_DSA_LAYERS, KV_LORA, N_MIX_HEADS, HEAD_DIM), KV_LORA ** -0.5),
        'dsa_w_uv': nrm((N_DSA_LAYERS, KV_LORA, N_MIX_HEADS, HEAD_DIM), KV_LORA ** -0.5),
        'dsa_w_q_idx': nrm((N_DSA_LAYERS, Q_LORA, IDX_HEADS * IDX_DIM), Q_LORA ** -0.5),
        'dsa_kidx_norm': gain((N_DSA_LAYERS, IDX_DIM)),
        'norm_final': gain((D,)),
    }


def reference(x, mem, rel_bias, norm_mix, norm_ffn, norm_mem, w_mem_kv, w_out, ffn_gate, ffn_up, ffn_down,
              nsa_w_in, nsa_gate_b, nsa_cmp_pos_k, nsa_cmp_pos_v,
              nsa_cmp_k_w1, nsa_cmp_k_b1, nsa_cmp_k_w2, nsa_cmp_k_b2,
              nsa_cmp_v_w1, nsa_cmp_v_b1, nsa_cmp_v_w2, nsa_cmp_v_b2,
              dsa_w_in, dsa_q_norm, dsa_kv_norm, dsa_w_q_up, dsa_w_uk, dsa_w_uv, dsa_w_q_idx, dsa_kidx_norm,
              norm_final):
    B, T, _ = x.shape
    h = x
    for i in range(DEPTH):
        xn = rmsnorm(h, norm_mix[i])
        j = i // N_MIXERS
        if i % N_MIXERS == 0:
            mix, q_mem = nsa_mixer(xn, nsa_w_in[j], nsa_gate_b[j], nsa_cmp_pos_k[j], nsa_cmp_pos_v[j],
                                   nsa_cmp_k_w1[j], nsa_cmp_k_b1[j], nsa_cmp_k_w2[j], nsa_cmp_k_b2[j],
                                   nsa_cmp_v_w1[j], nsa_cmp_v_b1[j], nsa_cmp_v_w2[j], nsa_cmp_v_b2[j],
                                   rel_bias)
        else:
            mix, q_mem = dsa_mixer(xn, dsa_w_in[j], dsa_q_norm[j], dsa_kv_norm[j], dsa_w_q_up[j],
                                   dsa_w_uk[j], dsa_w_uv[j], dsa_w_q_idx[j], dsa_kidx_norm[j], rel_bias)
        q_mem = q_mem.reshape(B, T, N_MEM_HEADS, HEAD_DIM) * HEAD_DIM ** -0.5
        mem_o = memory_attention(q_mem, rmsnorm(mem, norm_mem[i]), w_mem_kv[i])
        h = h + jnp.concatenate([mix, mem_o], axis=-1) @ w_out[i]
        hn = rmsnorm(h, norm_ffn[i])
        h = h + (jax.nn.silu(hn @ ffn_gate[i]) * (hn @ ffn_up[i])) @ ffn_down[i]
    return rmsnorm(h, norm_final)
```

```python
import functools
import math

import numpy as np
import jax
import jax.numpy as jnp
from jax import lax
from jax.experimental import pallas as pl
from jax.experimental.pallas import tpu as pltpu

F32 = jnp.float32
BF16 = jnp.bfloat16
I32 = jnp.int32

NEG = -1e30
EPS = 1e-6
LANE = 128
HEAD_DIM = 64
N_MIX_HEADS = 12
N_MEM_HEADS = 4
N_REL_BUCKETS = 32
REL_MAX_EXACT = 16
REL_MAX_DIST = 2048
NSA_KV_HEADS = 2
NSA_GROUP = N_MIX_HEADS // NSA_KV_HEADS
CMP_LEN = 32
CMP_STRIDE = 16
SEL_BLOCK = 64
N_SEL = 16
WINDOW = 512
FORCE_BONUS = 1e4
Q_LORA = 256
KV_LORA = 128
IDX_HEADS = 8
IDX_DIM = 64
DSA_TOPK = 256
TQ = 128
TK = 128
N_BIAS_TILES = REL_MAX_DIST // TK + 2
CMP_PAD = 512
VMEM_LIMIT = 56 * 1024 * 1024
INT_MIN = -2 ** 31


def _dot(a, b):
    return jnp.dot(a, b, preferred_element_type=F32)


def _dot_nt(a, b):
    return lax.dot_general(a, b, (((1,), (1,)), ((), ())), preferred_element_type=F32)


def _rms(x, gain, n=None):
    n = x.shape[-1] if n is None else n
    ms = jnp.sum(x * x, axis=-1, keepdims=True) * (1.0 / n)
    return x * lax.rsqrt(ms + EPS) * gain


def _rel_bucket_np(dist):
    n = np.maximum(dist, 0)
    nf = np.maximum(n, REL_MAX_EXACT).astype(np.float32)
    large = REL_MAX_EXACT + (np.log(nf / np.float32(REL_MAX_EXACT))
                             / np.float32(math.log(REL_MAX_DIST / REL_MAX_EXACT))
                             * np.float32(N_REL_BUCKETS - REL_MAX_EXACT)).astype(np.int32)
    large = np.minimum(large, N_REL_BUCKETS - 1)
    return np.where(n < REL_MAX_EXACT, n, large).astype(np.int32)


def _key_to_float(v):
    bits = jnp.where(v >= 0, v, v ^ jnp.int32(0x7FFFFFFF))
    return pltpu.bitcast(bits, F32)


def _topk_mask_small(score, k, lane_idx):
    rows = score.shape[0]

    def vbody(it, v):
        cand = v + jnp.left_shift(jnp.int32(1), 31 - it)
        cnt = jnp.sum((score >= _key_to_float(cand)).astype(F32), axis=1, keepdims=True)
        return jnp.where(cnt >= k, cand, v)

    v = lax.fori_loop(0, 32, vbody, jnp.full((rows, 1), INT_MIN, I32))
    thr = _key_to_float(v)
    gt = score > thr
    eq = score == thr
    need = k - jnp.sum(gt.astype(F32), axis=1, keepdims=True)

    def cbody(it, c):
        cand = c + jnp.left_shift(jnp.int32(1), 6 - it)
        cnt = jnp.sum((eq & (lane_idx < cand)).astype(F32), axis=1, keepdims=True)
        return jnp.where(cnt < need, cand, c)

    c = lax.fori_loop(0, 7, cbody, jnp.zeros((rows, 1), I32))
    return gt | (eq & (lane_idx <= c))


def _rms_proj_kernel(x_ref, g_ref, w_ref, *o_refs, splits, scales):
    y = _rms(x_ref[...], g_ref[...]).astype(BF16)
    off = 0
    for o_ref, n, sc in zip(o_refs, splits, scales):
        acc = _dot(y, w_ref[:, off:off + n])
        if sc != 1.0:
            acc = acc * sc
        o_ref[...] = acc.astype(o_ref.dtype)
        off += n


def _rms_proj(x2d, gain, w_bf16, splits, dtypes, scales, tm):
    m, d = x2d.shape
    n = w_bf16.shape[1]
    assert sum(splits) == n and m % tm == 0
    return pl.pallas_call(
        functools.partial(_rms_proj_kernel, splits=tuple(splits), scales=tuple(scales)),
        grid=(m // tm,),
        in_specs=[pl.BlockSpec((tm, d), lambda i: (i, 0)),
                  pl.BlockSpec((1, d), lambda i: (0, 0)),
                  pl.BlockSpec((d, n), lambda i: (0, 0))],
        out_specs=[pl.BlockSpec((tm, s), lambda i: (i, 0)) for s in splits],
        out_shape=[jax.ShapeDtypeStruct((m, s), dt) for s, dt in zip(splits, dtypes)],
        compiler_params=pltpu.CompilerParams(dimension_semantics=("arbitrary",),
                                             vmem_limit_bytes=VMEM_LIMIT),
        name="rms_proj",
    )(x2d, gain.reshape(1, d), w_bf16)


def _compress_kernel(xk_ref, xv_ref, pos_ref, w1_ref, b1_ref, w2_ref, b2_ref, o_ref):
    nc = xk_ref.shape[2]
    half = xk_ref.shape[3]
    out = None
    for j, x_ref in enumerate((xk_ref, xv_ref)):
        x = x_ref[0, 0]
        top = _dot((x + pos_ref[j, 0:1, :]).astype(BF16), w1_ref[j, :half, :])
        bot = _dot((x + pos_ref[j, 1:2, :]).astype(BF16), w1_ref[j, half:, :])
        pre = top + pltpu.roll(bot, nc - 1, axis=0) + b1_ref[j]
        hid = jax.nn.gelu(pre)
        res = _dot(hid.astype(BF16), w2_ref[j]) + b2_ref[j]
        out = res if out is None else out + res
    o_ref[0, 0] = out.astype(o_ref.dtype)


def _compress(x2, pos, w1, b1, w2, b2):
    bsz, _, nc, width = x2.shape
    g = NSA_KV_HEADS
    return pl.pallas_call(
        _compress_kernel,
        grid=(bsz, g),
        in_specs=[pl.BlockSpec((1, 1, nc, width), lambda b, gg: (b, gg, 0, 0)),
                  pl.BlockSpec((1, 1, nc, width), lambda b, gg: (b, gg + NSA_KV_HEADS, 0, 0)),
                  pl.BlockSpec(pos.shape, lambda b, gg: (0, 0, 0)),
                  pl.BlockSpec(w1.shape, lambda b, gg: (0, 0, 0)),
                  pl.BlockSpec(b1.shape, lambda b, gg: (0, 0, 0)),
                  pl.BlockSpec(w2.shape, lambda b, gg: (0, 0, 0)),
                  pl.BlockSpec(b2.shape, lambda b, gg: (0, 0, 0))],
        out_specs=pl.BlockSpec((1, 1, nc, LANE), lambda b, gg: (b, gg, 0, 0)),
        out_shape=jax.ShapeDtypeStruct((bsz, g, nc, LANE), BF16),
        compiler_params=pltpu.CompilerParams(dimension_semantics=("arbitrary", "arbitrary"),
                                             vmem_limit_bytes=VMEM_LIMIT),
        name="nsa_compress",
    )(x2, x2, pos, w1, b1, w2, b2)


def _softmax_step(s, mask, kv, m_i, l_i, acc):
    n_h, tq, tk = s.shape
    mask = mask[None]
    s = jnp.where(mask, s, NEG)
    m_new = jnp.maximum(m_i, jnp.max(s, axis=-1, keepdims=True))
    alpha = jnp.exp(m_i - m_new)
    p = jnp.where(mask, jnp.exp(s - m_new), 0.0)
    l_new = alpha * l_i + jnp.sum(p, axis=-1, keepdims=True)
    pv = _dot(p.reshape(n_h * tq, tk).astype(BF16), kv).reshape(n_h, tq, kv.shape[1])
    return m_new, l_new, alpha * acc + pv


def _softmax_init(n_h):
    return (jnp.full((n_h, TQ, 1), NEG, F32), jnp.zeros((n_h, TQ, 1), F32),
            jnp.zeros((n_h, TQ, LANE), F32))


def _nsa_kernel(q_ref, kvc_ref, kvs_ref, kvw_ref, gl_ref, gb_ref, tab_ref, tabc_ref, ovl_ref, o_ref,
                *, n_sel):
    r_heads = NSA_GROUP
    i = pl.program_id(2)
    qs = i * TQ
    rows = r_heads * TQ
    q = q_ref[0]
    q_all = jnp.concatenate([q[:, r * LANE:(r + 1) * LANE] for r in range(r_heads)], axis=0)
    t_col = qs + lax.broadcasted_iota(I32, (TQ, 1), 0)
    lane = lax.broadcasted_iota(I32, (TQ, LANE), 1)

    def logits(kv, bias):
        return _dot_nt(q_all, kv).reshape(r_heads, TQ, kv.shape[0]) + bias

    kvc = kvc_ref[0, 0]
    bias_c = pltpu.roll(tabc_ref[0], lax.rem(8 * i + 128, CMP_PAD), axis=2)
    n_idx = lax.broadcasted_iota(I32, (TQ, CMP_PAD), 1)
    cmask = ((CMP_STRIDE * n_idx + (CMP_LEN - 1)) <= t_col)[None]
    s = jnp.where(cmask, logits(kvc, bias_c), NEG)
    m = jnp.max(s, axis=-1, keepdims=True)
    e = jnp.where(cmask, jnp.exp(s - m), 0.0)
    den = jnp.sum(e, axis=-1, keepdims=True)
    p = e / jnp.where(den > 0.0, den, 1.0)
    o_cmp = _dot(p.reshape(rows, CMP_PAD).astype(BF16), kvc).reshape(r_heads, TQ, LANE)
    psum = jnp.sum(p, axis=0)

    ovl = ovl_ref[...]
    hi = psum.astype(BF16)
    rem1 = psum - hi.astype(F32)
    mid = rem1.astype(BF16)
    lo = (rem1 - mid.astype(F32)).astype(BF16)
    p_slc = _dot(hi, ovl) + _dot(mid, ovl) + _dot(lo, ovl)
    cur = jnp.right_shift(t_col, 6)
    forced = (lane == 0) | (lane == cur) | (lane == cur - 1)
    admissible = (lane * SEL_BLOCK) <= t_col
    score = jnp.where(admissible, p_slc + jnp.where(forced, FORCE_BONUS, 0.0), NEG)
    score = jnp.where(lane < n_sel, score, -jnp.inf)
    sel = _topk_mask_small(score, min(N_SEL, n_sel), lane) & (score > 0.5 * NEG)
    sel_bf = sel.astype(BF16)

    blk_row = lax.broadcasted_iota(I32, (LANE, TK), 0)
    key_col = lax.broadcasted_iota(I32, (LANE, TK), 1)
    kcol = lax.broadcasted_iota(I32, (TQ, TK), 1)

    def sel_body(kt, carry):
        k0 = pl.multiple_of(kt * TK, TK)
        kv = kvs_ref[0, pl.ds(k0, TK), :]
        s = logits(kv, tab_ref[0, jnp.minimum(i - kt, N_BIAS_TILES - 1)])
        expand = (jnp.right_shift(k0 + key_col, 6) == blk_row).astype(BF16)
        mask = (_dot(sel_bf, expand) > 0.5) & ((k0 + kcol) <= t_col)
        return _softmax_step(s, mask, kv, *carry)

    _, l_s, acc_s = lax.fori_loop(0, i + 1, sel_body, _softmax_init(r_heads))
    o_slc = acc_s / l_s

    carry = _softmax_init(r_heads)
    for j in range(WINDOW // TK + 1):
        kt = i - j
        k0 = pl.multiple_of(jnp.maximum(kt, 0) * TK, TK)
        kv = kvw_ref[0, pl.ds(k0, TK), :]
        dist = t_col - (k0 + kcol)
        mask = (dist >= 0) & (dist < jnp.where(kt >= 0, WINDOW, 0))
        carry = _softmax_step(logits(kv, tab_ref[0, j]), mask, kv, *carry)
    o_win = carry[2] / carry[1]

    gates = jax.nn.sigmoid(gl_ref[0] + gb_ref[...])
    for r in range(r_heads):
        out = (gates[:, 3 * r:3 * r + 1] * o_cmp[r] + gates[:, 3 * r + 1:3 * r + 2] * o_slc[r]
               + gates[:, 3 * r + 2:3 * r + 3] * o_win[r])
        o_ref[0, :, r * LANE:(r + 1) * LANE] = out.astype(o_ref.dtype)


def _nsa_attention(q_wide, kvc, kvs, kvw, gl, gate_b, tab, tabc, ovl, n_sel):
    bsz, t, _ = q_wide.shape
    g, r = NSA_KV_HEADS, NSA_GROUP
    return pl.pallas_call(
        functools.partial(_nsa_kernel, n_sel=n_sel),
        grid=(bsz, g, t // TQ),
        in_specs=[pl.BlockSpec((1, TQ, r * LANE), lambda b, gg, i: (b, i, gg)),
                  pl.BlockSpec((1, 1, CMP_PAD, LANE), lambda b, gg, i: (b, gg, 0, 0)),
                  pl.BlockSpec((1, t, LANE), lambda b, gg, i: (b, 0, gg)),
                  pl.BlockSpec((1, t, LANE), lambda b, gg, i: (b, 0, gg)),
                  pl.BlockSpec((1, TQ, LANE), lambda b, gg, i: (b, i, gg)),
                  pl.BlockSpec((1, LANE), lambda b, gg, i: (0, gg)),
                  pl.BlockSpec((1, N_BIAS_TILES, r, TQ, TK), lambda b, gg, i: (gg, 0, 0, 0, 0)),
                  pl.BlockSpec((1, r, TQ, CMP_PAD), lambda b, gg, i: (gg, 0, 0, 0)),
                  pl.BlockSpec((CMP_PAD, LANE), lambda b, gg, i: (0, 0))],
        out_specs=pl.BlockSpec((1, TQ, r * LANE), lambda b, gg, i: (b, i, gg)),
        out_shape=jax.ShapeDtypeStruct((bsz, t, g * r * LANE), BF16),
        compiler_params=pltpu.CompilerParams(
            dimension_semantics=("arbitrary", "arbitrary", "arbitrary"),
            vmem_limit_bytes=VMEM_LIMIT),
        name="nsa_attention",
    )(q_wide, kvc, kvs, kvw, gl, gate_b, tab, tabc, ovl)


def _dsa_proj_kernel(x_ref, g_ref, w_ref, qn_ref, kvn_ref, kin_ref, wqu_ref, wuk_ref, wqi_ref,
                     qa_ref, qi_ref, ckv_ref, kidx_ref, widx_ref, qmem_ref):
    y = _rms(x_ref[...], g_ref[...]).astype(BF16)
    c_q = _rms(_dot(y, w_ref[:, 0:Q_LORA]), qn_ref[...]).astype(BF16)
    c_kv = _rms(_dot(y, w_ref[:, Q_LORA:Q_LORA + KV_LORA]), kvn_ref[...])
    ckv_ref[...] = c_kv.astype(ckv_ref.dtype)
    off = Q_LORA + KV_LORA
    k_idx = _rms(_dot(y, w_ref[:, off:off + LANE]), kin_ref[...], n=IDX_DIM)
    kidx_ref[...] = k_idx.astype(kidx_ref.dtype)
    off += LANE
    widx_ref[...] = _dot(y, w_ref[:, off:off + LANE]) * (IDX_HEADS ** -0.5 * IDX_DIM ** -0.5)
    off += LANE
    qmem_ref[...] = (_dot(y, w_ref[:, off:off + N_MEM_HEADS * LANE]) * HEAD_DIM ** -0.5
                     ).astype(qmem_ref.dtype)
    qi_ref[...] = _dot(c_q, wqi_ref[...]).astype(qi_ref.dtype)
    for h in range(N_MIX_HEADS):
        q_h = (_dot(c_q, wqu_ref[:, h * LANE:(h + 1) * LANE]) * HEAD_DIM ** -0.5).astype(BF16)
        qa_ref[:, h * LANE:(h + 1) * LANE] = _dot(q_h, wuk_ref[h]).astype(qa_ref.dtype)


def _dsa_proj(x2d, gain, w, qn, kvn, kin, wqu, wuk, wqi, tm):
    m, d = x2d.shape
    full = lambda a: pl.BlockSpec(a.shape, lambda i: (0,) * a.ndim)
    widths = (N_MIX_HEADS * LANE, IDX_HEADS * LANE, LANE, LANE, LANE, N_MEM_HEADS * LANE)
    dtypes = (BF16, BF16, BF16, BF16, F32, BF16)
    return pl.pallas_call(
        _dsa_proj_kernel,
        grid=(m // tm,),
        in_specs=[pl.BlockSpec((tm, d), lambda i: (i, 0)), full(gain), full(w), full(qn), full(kvn),
                  full(kin), full(wqu), full(wuk), full(wqi)],
        out_specs=[pl.BlockSpec((tm, n), lambda i: (i, 0)) for n in widths],
        out_shape=[jax.ShapeDtypeStruct((m, n), dt) for n, dt in zip(widths, dtypes)],
        compiler_params=pltpu.CompilerParams(dimension_semantics=("arbitrary",),
                                             vmem_limit_bytes=VMEM_LIMIT),
        name="dsa_proj",
    )(x2d, gain, w, qn, kvn, kin, wqu, wuk, wqi)


def _dsa_kernel(qi_ref, wi_ref, qa_ref, kidx_ref, ckv_ref, tab_ref, o_ref, sc_ref, *, topk):
    i = pl.program_id(1)
    qs = i * TQ
    n_tiles = i + 1
    t_col = qs + lax.broadcasted_iota(I32, (TQ, 1), 0)
    kcol = lax.broadcasted_iota(I32, (TQ, TK), 1)
    k_f = float(topk)

    qi = qi_ref[0]
    qi_all = jnp.concatenate([qi[:, h * LANE:(h + 1) * LANE] for h in range(IDX_HEADS)], axis=0)
    wi = wi_ref[0]
    w_cols = [jnp.broadcast_to(wi[:, h:h + 1], (TQ, TK)) for h in range(IDX_HEADS)]

    def score_body(kt, carry):
        k0 = pl.multiple_of(kt * TK, TK)
        logits = _dot_nt(qi_all, kidx_ref[0, pl.ds(k0, TK), :])
        sc = jnp.maximum(logits[0:TQ], 0.0) * w_cols[0]
        for h in range(1, IDX_HEADS):
            sc = sc + jnp.maximum(logits[h * TQ:(h + 1) * TQ], 0.0) * w_cols[h]
        sc_ref[kt] = jnp.where((k0 + kcol) <= t_col, sc, NEG)
        return carry

    lax.fori_loop(0, n_tiles, score_body, 0)

    def count(pred):
        def body(kt, acc):
            return acc + pred(sc_ref[kt], kt).astype(F32)
        acc = lax.fori_loop(0, n_tiles, body, jnp.zeros((TQ, TK), F32))
        return jnp.sum(acc, axis=1, keepdims=True)

    short = t_col < topk

    def v_cond(c):
        return (c[0] < 32) & (c[3] > 0)

    def v_body(c):
        it, v, cnt_v, _ = c
        cand = v + jnp.left_shift(jnp.int32(1), 31 - it)
        cand_f = _key_to_float(cand)
        cnt = count(lambda sc, kt: sc >= cand_f)
        take = cnt >= k_f
        v = jnp.where(take, cand, v)
        cnt_v = jnp.where(take, cnt, cnt_v)
        open_rows = jnp.sum(((cnt_v != k_f) & ~short).astype(I32))
        return it + 1, v, cnt_v, open_rows

    _, v, cnt_v, open_rows = lax.while_loop(
        v_cond, v_body,
        (jnp.int32(0), jnp.full((TQ, 1), INT_MIN, I32), jnp.full((TQ, 1), 1e9, F32), jnp.int32(1)))
    thr = _key_to_float(v)

    def tie_cut(_):
        need = k_f - count(lambda sc, kt: sc > thr)

        def c_body(it, c):
            cand = c + jnp.left_shift(jnp.int32(1), 13 - it)
            cnt = count(lambda sc, kt: (sc == thr) & ((kt * TK + kcol) < cand))
            return jnp.where(cnt < need, cand, c)

        return lax.fori_loop(0, 14, c_body, jnp.zeros((TQ, 1), I32))

    cut = lax.cond(open_rows > 0, tie_cut, lambda _: jnp.full((TQ, 1), 2 ** 30, I32), 0)

    qa = qa_ref[0]
    qa_all = jnp.concatenate([qa[:, h * LANE:(h + 1) * LANE] for h in range(N_MIX_HEADS)], axis=0)

    def att_body(kt, carry):
        k0 = pl.multiple_of(kt * TK, TK)
        kv = ckv_ref[0, pl.ds(k0, TK), :]
        bias = tab_ref[jnp.minimum(i - kt, N_BIAS_TILES - 1)]
        s = _dot_nt(qa_all, kv).reshape(N_MIX_HEADS, TQ, TK) + bias
        sc = sc_ref[kt]
        kpos = k0 + kcol
        chosen = short | (sc > thr) | ((sc == thr) & (kpos <= cut))
        return _softmax_step(s, chosen & (kpos <= t_col), kv, *carry)

    _, l_i, acc = lax.fori_loop(0, n_tiles, att_body, _softmax_init(N_MIX_HEADS))
    out = acc / l_i
    for h in range(N_MIX_HEADS):
        o_ref[0, :, h * LANE:(h + 1) * LANE] = out[h].astype(o_ref.dtype)


def _dsa_attention(qi, wi, qa, kidx, ckv, tab, topk):
    bsz, t, _ = qa.shape
    return pl.pallas_call(
        functools.partial(_dsa_kernel, topk=topk),
        grid=(bsz, t // TQ),
        in_specs=[pl.BlockSpec((1, TQ, IDX_HEADS * LANE), lambda b, i: (b, i, 0)),
                  pl.BlockSpec((1, TQ, LANE), lambda b, i: (b, i, 0)),
                  pl.BlockSpec((1, TQ, N_MIX_HEADS * LANE), lambda b, i: (b, i, 0)),
                  pl.BlockSpec((1, t, LANE), lambda b, i: (b, 0, 0)),
                  pl.BlockSpec((1, t, LANE), lambda b, i: (b, 0, 0)),
                  pl.BlockSpec(tab.shape, lambda b, i: (0, 0, 0, 0),
                               pipeline_mode=pl.Buffered(1))],
        out_specs=pl.BlockSpec((1, TQ, N_MIX_HEADS * LANE), lambda b, i: (b, i, 0)),
        out_shape=jax.ShapeDtypeStruct((bsz, t, N_MIX_HEADS * LANE), BF16),
        scratch_shapes=[pltpu.VMEM((t // TK, TQ, TK), F32)],
        compiler_params=pltpu.CompilerParams(dimension_semantics=("arbitrary", "arbitrary"),
                                             vmem_limit_bytes=VMEM_LIMIT),
        name="dsa_attention",
    )(qi, wi, qa, kidx, ckv, tab)


def _post_kernel(h_ref, mix_ref, qmem_ref, kvm_ref, *rest, has_uv):
    if has_uv:
        wuv_ref, wmix_ref, wmem_ref, o_ref = rest
    else:
        wmix_ref, wmem_ref, o_ref = rest
    mix = mix_ref[0]
    if has_uv:
        mix = _dot(mix, wuv_ref[...]).astype(BF16)
    upd = _dot(mix, wmix_ref[...])
    qm = qmem_ref[0]
    for hm in range(N_MEM_HEADS):
        sl = slice(hm * LANE, (hm + 1) * LANE)
        kv = kvm_ref[0, :, sl]
        s = _dot_nt(qm[:, sl], kv)
        e = jnp.exp(s - jnp.max(s, axis=-1, keepdims=True))
        p = e / jnp.sum(e, axis=-1, keepdims=True)
        o_h = _dot(p.astype(BF16), kv).astype(BF16)
        upd = upd + _dot(o_h, wmem_ref[sl, :])
    o_ref[0] = h_ref[0] + upd


def _post(h, mix, qmem, kvm, w_uv, w_mix, w_mem, tm):
    bsz, t, d = h.shape
    has_uv = w_uv is not None
    full = lambda a: pl.BlockSpec(a.shape, lambda b, i: (0,) * a.ndim)
    weights = ([w_uv] if has_uv else []) + [w_mix, w_mem]
    return pl.pallas_call(
        functools.partial(_post_kernel, has_uv=has_uv),
        grid=(bsz, t // tm),
        in_specs=[pl.BlockSpec((1, tm, d), lambda b, i: (b, i, 0)),
                  pl.BlockSpec((1, tm, mix.shape[2]), lambda b, i: (b, i, 0)),
                  pl.BlockSpec((1, tm, qmem.shape[2]), lambda b, i: (b, i, 0)),
                  pl.BlockSpec((1,) + kvm.shape[1:], lambda b, i: (b, 0, 0))]
                 + [full(w) for w in weights],
        out_specs=pl.BlockSpec((1, tm, d), lambda b, i: (b, i, 0)),
        out_shape=jax.ShapeDtypeStruct((bsz, t, d), F32),
        compiler_params=pltpu.CompilerParams(dimension_semantics=("arbitrary", "arbitrary"),
                                             vmem_limit_bytes=VMEM_LIMIT),
        name="mem_attn_out_proj",
    )(h, mix, qmem, kvm, *weights)


def _ffn_kernel(h_ref, g_ref, wg_ref, wu_ref, wd_ref, gf_ref, o_ref, *, final_norm):
    h = h_ref[...]
    hn = _rms(h, g_ref[...]).astype(BF16)
    act = (jax.nn.silu(_dot(hn, wg_ref[...])) * _dot(hn, wu_ref[...])).astype(BF16)
    out = h + _dot(act, wd_ref[...])
    if final_norm:
        out = _rms(out, gf_ref[...])
    o_ref[...] = out


def _ffn(h2d, gain, wg, wu, wd, gain_final, final_norm, tm):
    m, d = h2d.shape
    const = lambda a: pl.BlockSpec(a.shape, lambda i: (0,) * a.ndim, pipeline_mode=pl.Buffered(1))
    return pl.pallas_call(
        functools.partial(_ffn_kernel, final_norm=final_norm),
        grid=(m // tm,),
        in_specs=[pl.BlockSpec((tm, d), lambda i: (i, 0)), const(gain), const(wg), const(wu),
                  const(wd), const(gain_final)],
        out_specs=pl.BlockSpec((tm, d), lambda i: (i, 0)),
        out_shape=jax.ShapeDtypeStruct((m, d), F32),
        compiler_params=pltpu.CompilerParams(dimension_semantics=("arbitrary",),
                                             vmem_limit_bytes=VMEM_LIMIT),
        name="swiglu_ffn",
    )(h2d, gain, wg, wu, wd, gain_final)


def _pad_heads(w, n_heads):
    d_in = w.shape[0]
    w = w.reshape(d_in, n_heads, HEAD_DIM)
    return jnp.pad(w, ((0, 0), (0, 0), (0, LANE - HEAD_DIM))).reshape(d_in, n_heads * LANE)


def _pad_cols(w, n):
    return jnp.pad(w, ((0, 0), (0, n - w.shape[1])))


def _value_rows(w_rows, n_heads):
    d_out = w_rows.shape[1]
    w = w_rows.reshape(n_heads, HEAD_DIM, d_out)
    return jnp.pad(w, ((0, 0), (LANE - HEAD_DIM, 0), (0, 0))).reshape(n_heads * LANE, d_out)


def _interleave_kv(k, v, n_heads):
    d_in = k.shape[0]
    kv = jnp.concatenate([k.reshape(d_in, n_heads, HEAD_DIM), v.reshape(d_in, n_heads, HEAD_DIM)], axis=2)
    return kv.reshape(d_in, n_heads * LANE)


def _bias_tiles(rel_bias):
    m = np.arange(N_BIAS_TILES)[:, None, None]
    dist = TK * m + np.arange(TQ)[None, :, None] - np.arange(TK)[None, None, :]
    tab = rel_bias[_rel_bucket_np(dist)]
    return jnp.transpose(tab, (0, 3, 1, 2))


def _bias_cmp_table(rel_bias):
    rel = np.arange(CMP_PAD) - 384
    dist = np.arange(TQ)[:, None] - CMP_STRIDE * rel[None, :] - (CMP_LEN - 1)
    near = (rel >= -(REL_MAX_DIST // CMP_STRIDE + 2)) & (rel <= TQ // CMP_STRIDE - 1)
    bucket = np.where(near[None, :], _rel_bucket_np(dist), N_REL_BUCKETS - 1)
    return jnp.transpose(rel_bias[bucket], (2, 0, 1))


def _overlap_matrix(t):
    n_cmp = (t - CMP_LEN) // CMP_STRIDE + 1
    n_sel = t // SEL_BLOCK
    cs = np.arange(CMP_PAD) * CMP_STRIDE
    ss = np.arange(LANE) * SEL_BLOCK
    ov = (cs[:, None] <= ss[None, :] + SEL_BLOCK - 1) & (cs[:, None] + CMP_LEN - 1 >= ss[None, :])
    ov &= (np.arange(CMP_PAD) < n_cmp)[:, None] & (np.arange(LANE) < n_sel)[None, :]
    return jnp.asarray(ov, BF16)


def kernel(x, mem, rel_bias, norm_mix, norm_ffn, norm_mem, w_mem_kv, w_out, ffn_gate, ffn_up, ffn_down,
           nsa_w_in, nsa_gate_b, nsa_cmp_pos_k, nsa_cmp_pos_v,
           nsa_cmp_k_w1, nsa_cmp_k_b1, nsa_cmp_k_w2, nsa_cmp_k_b2,
           nsa_cmp_v_w1, nsa_cmp_v_b1, nsa_cmp_v_w2, nsa_cmp_v_b2,
           dsa_w_in, dsa_q_norm, dsa_kv_norm, dsa_w_q_up, dsa_w_uk, dsa_w_uv, dsa_w_q_idx, dsa_kidx_norm,
           norm_final):
    bsz, t, d = x.shape
    m_len = mem.shape[1]
    depth = norm_mix.shape[0]
    g, r, hh = NSA_KV_HEADS, NSA_GROUP, N_MIX_HEADS
    d_mix = hh * HEAD_DIM
    kvw_ = g * HEAD_DIM
    assert t % 1024 == 0 and t // SEL_BLOCK <= LANE and t // CMP_STRIDE <= CMP_PAD
    tm = 512
    h = x.astype(F32)

    tab = _bias_tiles(rel_bias)
    tab_nsa = jnp.transpose(tab.reshape(N_BIAS_TILES, g, r, TQ, TK), (1, 0, 2, 3, 4))
    tabc = _bias_cmp_table(rel_bias).reshape(g, r, TQ, CMP_PAD)
    ovl = _overlap_matrix(t)

    for layer in range(depth):
        j = layer // 2
        wkv = w_mem_kv[layer]
        wkv = _interleave_kv(wkv[:, :N_MEM_HEADS * HEAD_DIM], wkv[:, N_MEM_HEADS * HEAD_DIM:], N_MEM_HEADS)
        (kvm,) = _rms_proj(mem.reshape(bsz * m_len, d), norm_mem[layer], wkv.astype(BF16),
                           [N_MEM_HEADS * LANE], [BF16], [1.0], tm=m_len)
        kvm = kvm.reshape(bsz, m_len, N_MEM_HEADS * LANE)
        w_o = w_out[layer]
        w_mem_o = _value_rows(w_o[d_mix:], N_MEM_HEADS).astype(BF16)

        if layer % 2 == 0:
            w = nsa_w_in[j]
            c = np.cumsum([0, d_mix, kvw_, kvw_, kvw_, kvw_, kvw_, kvw_, hh * 3, N_MEM_HEADS * HEAD_DIM])
            wq, wkc, wvc, wks, wvs, wkw, wvw, wgl, wqm = [w[:, c[k]:c[k + 1]] for k in range(9)]
            wgl = jnp.concatenate([_pad_cols(wgl[:, gg * r * 3:(gg + 1) * r * 3], LANE) for gg in range(g)], 1)
            w_all = jnp.concatenate([_pad_heads(wq, hh), wkc, wvc, _interleave_kv(wks, wvs, g),
                                     _interleave_kv(wkw, wvw, g), wgl, _pad_heads(wqm, N_MEM_HEADS)], axis=1)
            splits = [hh * LANE, 2 * kvw_, g * LANE, g * LANE, g * LANE, N_MEM_HEADS * LANE]
            q_wide, cmp_raw, kvs, kvw, gl, qmem = _rms_proj(
                h.reshape(bsz * t, d), norm_mix[layer], w_all.astype(BF16), splits,
                [BF16, F32, BF16, BF16, F32, BF16],
                [HEAD_DIM ** -0.5, 1.0, 1.0, 1.0, 1.0, HEAD_DIM ** -0.5], tm)
            nc = t // CMP_STRIDE
            x2 = jnp.transpose(cmp_raw.reshape(bsz, t, 2 * g, HEAD_DIM), (0, 2, 1, 3))
            x2 = x2.reshape(bsz, 2 * g, nc, CMP_STRIDE * HEAD_DIM)
            pos = jnp.stack([nsa_cmp_pos_k[j], nsa_cmp_pos_v[j]]).reshape(2, 2, CMP_STRIDE * HEAD_DIM)
            w1 = jnp.stack([nsa_cmp_k_w1[j], nsa_cmp_v_w1[j]]).astype(BF16)
            b1 = jnp.stack([nsa_cmp_k_b1[j], nsa_cmp_v_b1[j]])[:, None, :]
            w2 = jnp.stack([jnp.pad(nsa_cmp_k_w2[j], ((0, 0), (0, HEAD_DIM))),
                            jnp.pad(nsa_cmp_v_w2[j], ((0, 0), (HEAD_DIM, 0)))]).astype(BF16)
            b2 = jnp.stack([jnp.pad(nsa_cmp_k_b2[j], (0, HEAD_DIM)),
                            jnp.pad(nsa_cmp_v_b2[j], (HEAD_DIM, 0))])[:, None, :]
            kvc = _compress(x2, pos, w1, b1, w2, b2)
            kvc = jnp.pad(kvc, ((0, 0), (0, 0), (0, CMP_PAD - nc), (0, 0)))
            gate_b = jnp.concatenate([_pad_cols(nsa_gate_b[j][None, gg * r * 3:(gg + 1) * r * 3], LANE)
                                      for gg in range(g)], axis=1)
            mix = _nsa_attention(q_wide.reshape(bsz, t, hh * LANE), kvc, kvs.reshape(bsz, t, g * LANE),
                                 kvw.reshape(bsz, t, g * LANE), gl.reshape(bsz, t, g * LANE), gate_b,
                                 tab_nsa, tabc, ovl, t // SEL_BLOCK)
            w_uv = None
        else:
            w = dsa_w_in[j]
            c = np.cumsum([0, Q_LORA, KV_LORA, IDX_DIM, IDX_HEADS, N_MEM_HEADS * HEAD_DIM])
            wcq, wckv, wki, wwi, wqm = [w[:, c[k]:c[k + 1]] for k in range(5)]
            w_all = jnp.concatenate([wcq, wckv, _pad_cols(wki, LANE), _pad_cols(wwi, LANE),
                                     _pad_heads(wqm, N_MEM_HEADS)], axis=1).astype(BF16)
            wuk = jnp.transpose(dsa_w_uk[j], (1, 2, 0))
            wuk = jnp.pad(wuk, ((0, 0), (0, LANE - HEAD_DIM), (0, 0))).astype(BF16)
            qa, qi, ckv, kidx, widx, qmem = _dsa_proj(
                h.reshape(bsz * t, d), norm_mix[layer][None], w_all, dsa_q_norm[j][None],
                dsa_kv_norm[j][None], _pad_cols(dsa_kidx_norm[j][None], LANE),
                _pad_heads(dsa_w_q_up[j], hh).astype(BF16), wuk,
                _pad_heads(dsa_w_q_idx[j], IDX_HEADS).astype(BF16), tm)
            mix = _dsa_attention(qi.reshape(bsz, t, -1), widx.reshape(bsz, t, LANE), qa.reshape(bsz, t, -1),
                                 kidx.reshape(bsz, t, LANE), ckv.reshape(bsz, t, LANE), tab,
                                 min(DSA_TOPK, t // 4))
            wv = jnp.transpose(dsa_w_uv[j], (1, 0, 2))
            w_uv = (jnp.eye(hh, dtype=F32)[:, None, :, None] * wv[:, :, None, :]
                    ).reshape(hh * KV_LORA, d_mix).astype(BF16)

        w_mix = (w_o[:d_mix] if w_uv is not None else _value_rows(w_o[:d_mix], hh)).astype(BF16)
        h = _post(h, mix, qmem.reshape(bsz, t, N_MEM_HEADS * LANE), kvm, w_uv, w_mix, w_mem_o, tm)
        last = layer == depth - 1
        h = _ffn(h.reshape(bsz * t, d), norm_ffn[layer][None], ffn_gate[layer].astype(BF16),
                 ffn_up[layer].astype(BF16), ffn_down[layer].astype(BF16), norm_final[None],
                 last, tm).reshape(bsz, t, d)
    return h.astype(x.dtype)
```

```python
import functools
import math

import numpy as np
import jax
import jax.numpy as jnp
from jax import lax
from jax.experimental import pallas as pl
from jax.experimental.pallas import tpu as pltpu

F32 = jnp.float32
BF16 = jnp.bfloat16
I32 = jnp.int32

NEG = -1e30
EPS = 1e-6
LANE = 128
SUBLANE = 8
HEAD_DIM = 64
N_MIX_HEADS = 12
N_MEM_HEADS = 4
N_REL_BUCKETS = 32
REL_MAX_EXACT = 16
REL_MAX_DIST = 2048
NSA_KV_HEADS = 2
NSA_GROUP = N_MIX_HEADS // NSA_KV_HEADS
N_BRANCH = 3
GATE_ROWS = 24
CMP_LEN = 32
CMP_STRIDE = 16
SEL_BLOCK = 64
N_SEL = 16
WINDOW = 512
FORCE_BONUS = 1e4
Q_LORA = 256
KV_LORA = 128
IDX_HEADS = 8
IDX_DIM = 64
DSA_TOPK = 256
TQ = 256
TK = 128
N_BIAS_TILES = REL_MAX_DIST // TK + 2
CMP_PAD = 512
NSA_VT_ROWS = 128
VMEM_LIMIT = 56 * 1024 * 1024
INT_MIN = -2 ** 31
F32_MIN = float(np.finfo(np.float32).min)


def _dot(a, b):
    return jnp.dot(a, b, preferred_element_type=F32)


def _dot_nt(a, b):
    return lax.dot_general(a, b, (((1,), (1,)), ((), ())), preferred_element_type=F32)


def _rms(x, gain, n=None):
    n = x.shape[-1] if n is None else n
    ms = jnp.sum(x * x, axis=-1, keepdims=True) * (1.0 / n)
    return x * lax.rsqrt(ms + EPS) * gain


def _rel_bucket_np(dist):
    n = np.maximum(dist, 0)
    nf = np.maximum(n, REL_MAX_EXACT).astype(np.float32)
    large = REL_MAX_EXACT + (np.log(nf / np.float32(REL_MAX_EXACT))
                             / np.float32(math.log(REL_MAX_DIST / REL_MAX_EXACT))
                             * np.float32(N_REL_BUCKETS - REL_MAX_EXACT)).astype(np.int32)
    large = np.minimum(large, N_REL_BUCKETS - 1)
    return np.where(n < REL_MAX_EXACT, n, large).astype(np.int32)


def _colsum(x):
    rows, n = x.shape
    return jnp.sum(jnp.sum(x.reshape(rows // SUBLANE, SUBLANE, n), axis=0), axis=0, keepdims=True)


def _key_to_float(v):
    bits = jnp.where(v >= 0, v, v ^ jnp.int32(0x7FFFFFFF))
    return pltpu.bitcast(bits, F32)


def _topk_cols(score, k, row_idx):
    n = score.shape[1]

    def vbody(it, v):
        cand = v + jnp.left_shift(jnp.int32(1), 31 - it)
        cnt = _colsum((score >= _key_to_float(cand)).astype(F32))
        return jnp.where(cnt >= k, cand, v)

    v = lax.fori_loop(0, 32, vbody, jnp.full((1, n), INT_MIN, I32))
    thr = _key_to_float(v)
    gt = score > thr
    eq = score == thr
    need = k - _colsum(gt.astype(F32))

    def cbody(it, c):
        cand = c + jnp.left_shift(jnp.int32(1), 6 - it)
        cnt = _colsum((eq & (row_idx < cand)).astype(F32))
        return jnp.where(cnt < need, cand, c)

    c = lax.fori_loop(0, 7, cbody, jnp.zeros((1, n), I32))
    return gt | (eq & (row_idx <= c))


def _attend_tile(n_heads, q_of, kv, vt, negm_ref, bias_of, m_ref, l_ref, acc_ref):
    logits = [_dot(kv, q_of(h)) for h in range(n_heads)]
    for h in range(n_heads):
        s = logits[h] + bias_of(h) + negm_ref[...]
        m_old = m_ref[h]
        m_new = jnp.maximum(m_old, jnp.max(s, axis=0, keepdims=True))
        alpha = jnp.exp(m_old - m_new)
        p = jnp.exp(s - m_new)
        l_ref[h] = alpha * l_ref[h] + jnp.sum(p, axis=0, keepdims=True)
        m_ref[h] = m_new
        acc_ref[h] = alpha * acc_ref[h] + _dot(vt, p.astype(BF16))


def _reset_state(m_ref, l_ref, acc_ref):
    m_ref[...] = jnp.full(m_ref.shape, F32_MIN, F32)
    l_ref[...] = jnp.zeros(l_ref.shape, F32)
    acc_ref[...] = jnp.zeros(acc_ref.shape, F32)


def _rms_proj_kernel(x_ref, g_ref, w_ref, o_ref):
    y = _rms(x_ref[...], g_ref[...]).astype(BF16)
    o_ref[...] = _dot(y, w_ref[...]).astype(o_ref.dtype)


def _rms_proj(x2d, gain, w_bf16, out_dtype, tm):
    m, d = x2d.shape
    n = w_bf16.shape[1]
    return pl.pallas_call(
        _rms_proj_kernel,
        grid=(m // tm,),
        in_specs=[pl.BlockSpec((tm, d), lambda i: (i, 0)),
                  pl.BlockSpec((1, d), lambda i: (0, 0)),
                  pl.BlockSpec((d, n), lambda i: (0, 0))],
        out_specs=pl.BlockSpec((tm, n), lambda i: (i, 0)),
        out_shape=jax.ShapeDtypeStruct((m, n), out_dtype),
        compiler_params=pltpu.CompilerParams(dimension_semantics=("arbitrary",),
                                             vmem_limit_bytes=VMEM_LIMIT),
        name="rms_proj",
    )(x2d, gain.reshape(1, d), w_bf16)


def _nsa_proj_kernel(x_ref, g_ref, w_ref, qt_ref, cmp_ref, kvs_ref, vst_ref, kvw_ref, vwt_ref,
                     glt_ref, qmem_ref):
    tm = x_ref.shape[1]
    y = _rms(x_ref[0], g_ref[...]).astype(BF16)
    off = 0
    for h in range(N_MIX_HEADS):
        acc = _dot(y, w_ref[:, off:off + LANE]) * HEAD_DIM ** -0.5
        qt_ref[0, h] = acc.T.astype(BF16)
        off += LANE
    cmp_ref[0] = _dot(y, w_ref[:, off:off + 2 * LANE])
    off += 2 * LANE
    for kv_ref, vt_ref in ((kvs_ref, vst_ref), (kvw_ref, vwt_ref)):
        for g in range(NSA_KV_HEADS):
            acc = _dot(y, w_ref[:, off:off + LANE])
            off += LANE
            kv_ref[0, :, g * LANE:(g + 1) * LANE] = acc.astype(BF16)
            v_t = acc.T[LANE - NSA_VT_ROWS:, :]
            for jt in range(tm // TK):
                vt_ref[0, g, jt] = v_t[:, jt * TK:(jt + 1) * TK].astype(BF16)
    for g in range(NSA_KV_HEADS):
        acc = _dot(y, w_ref[:, off:off + LANE])
        off += LANE
        glt_ref[0, g] = acc.T[:GATE_ROWS, :]
    qmem_ref[0] = (_dot(y, w_ref[:, off:off + N_MEM_HEADS * LANE]) * HEAD_DIM ** -0.5).astype(BF16)


def _nsa_proj(h, gain, w, tm):
    bsz, t, d = h.shape
    g = NSA_KV_HEADS
    nkt = t // TK
    row = lambda n: pl.BlockSpec((1, tm, n), lambda b, i: (b, i, 0))
    vt_spec = pl.BlockSpec((1, g, tm // TK, NSA_VT_ROWS, TK), lambda b, i: (b, 0, i, 0, 0))
    return pl.pallas_call(
        _nsa_proj_kernel,
        grid=(bsz, t // tm),
        in_specs=[pl.BlockSpec((1, tm, d), lambda b, i: (b, i, 0)),
                  pl.BlockSpec((1, d), lambda b, i: (0, 0)),
                  pl.BlockSpec(w.shape, lambda b, i: (0, 0))],
        out_specs=[pl.BlockSpec((1, N_MIX_HEADS, LANE, tm), lambda b, i: (b, 0, 0, i)),
                   row(2 * LANE), row(g * LANE), vt_spec, row(g * LANE), vt_spec,
                   pl.BlockSpec((1, g, GATE_ROWS, tm), lambda b, i: (b, 0, 0, i)),
                   row(N_MEM_HEADS * LANE)],
        out_shape=[jax.ShapeDtypeStruct((bsz, N_MIX_HEADS, LANE, t), BF16),
                   jax.ShapeDtypeStruct((bsz, t, 2 * LANE), F32),
                   jax.ShapeDtypeStruct((bsz, t, g * LANE), BF16),
                   jax.ShapeDtypeStruct((bsz, g, nkt, NSA_VT_ROWS, TK), BF16),
                   jax.ShapeDtypeStruct((bsz, t, g * LANE), BF16),
                   jax.ShapeDtypeStruct((bsz, g, nkt, NSA_VT_ROWS, TK), BF16),
                   jax.ShapeDtypeStruct((bsz, g, GATE_ROWS, t), F32),
                   jax.ShapeDtypeStruct((bsz, t, N_MEM_HEADS * LANE), BF16)],
        compiler_params=pltpu.CompilerParams(dimension_semantics=("arbitrary", "arbitrary"),
                                             vmem_limit_bytes=VMEM_LIMIT),
        name="nsa_proj",
    )(h, gain.reshape(1, d), w)


def _compress_kernel(xk_ref, xv_ref, pos_ref, w1_ref, b1_ref, w2_ref, b2_ref, o_ref):
    nc = xk_ref.shape[2]
    half = xk_ref.shape[3]
    out = None
    for j, x_ref in enumerate((xk_ref, xv_ref)):
        x = x_ref[0, 0]
        top = _dot((x + pos_ref[j, 0:1, :]).astype(BF16), w1_ref[j, :half, :])
        bot = _dot((x + pos_ref[j, 1:2, :]).astype(BF16), w1_ref[j, half:, :])
        pre = top + pltpu.roll(bot, nc - 1, axis=0) + b1_ref[j]
        hid = jax.nn.gelu(pre)
        res = _dot(hid.astype(BF16), w2_ref[j]) + b2_ref[j]
        out = res if out is None else out + res
    o_ref[0, 0] = out.astype(o_ref.dtype)


def _compress(x2, pos, w1, b1, w2, b2):
    bsz, _, nc, width = x2.shape
    g = NSA_KV_HEADS
    return pl.pallas_call(
        _compress_kernel,
        grid=(bsz, g),
        in_specs=[pl.BlockSpec((1, 1, nc, width), lambda b, gg: (b, gg, 0, 0)),
                  pl.BlockSpec((1, 1, nc, width), lambda b, gg: (b, gg + NSA_KV_HEADS, 0, 0)),
                  pl.BlockSpec(pos.shape, lambda b, gg: (0, 0, 0)),
                  pl.BlockSpec(w1.shape, lambda b, gg: (0, 0, 0)),
                  pl.BlockSpec(b1.shape, lambda b, gg: (0, 0, 0)),
                  pl.BlockSpec(w2.shape, lambda b, gg: (0, 0, 0)),
                  pl.BlockSpec(b2.shape, lambda b, gg: (0, 0, 0))],
        out_specs=pl.BlockSpec((1, 1, nc, LANE), lambda b, gg: (b, gg, 0, 0)),
        out_shape=jax.ShapeDtypeStruct((bsz, g, nc, LANE), BF16),
        compiler_params=pltpu.CompilerParams(dimension_semantics=("arbitrary", "arbitrary"),
                                             vmem_limit_bytes=VMEM_LIMIT),
        name="nsa_compress",
    )(x2, x2, pos, w1, b1, w2, b2)


def _nsa_kernel(q_ref, kvc_ref, vct_ref, kvs_ref, vst_ref, kvw_ref, vwt_ref, glt_ref, gbt_ref,
                tab_ref, tabc_ref, ovl_ref, o_ref,
                psum_ref, negsel_ref, negm_ref, m_ref, l_ref, acc_ref, ocmp_ref, oslc_ref, *, n_sel):
    r_heads = NSA_GROUP
    i = pl.program_id(2)
    qs = i * TQ
    t_row = qs + lax.broadcasted_iota(I32, (1, TQ), 1)
    krow = lax.broadcasted_iota(I32, (TK, TQ), 0)
    q_of = lambda h: q_ref[0, h]

    kvc = kvc_ref[0, 0]
    vct = vct_ref[0, 0]
    n_row = lax.broadcasted_iota(I32, (CMP_PAD, TQ), 0)
    negc = jnp.where((CMP_STRIDE * n_row + (CMP_LEN - 1)) <= t_row, 0.0, -jnp.inf)
    j0 = pl.multiple_of(CMP_PAD - (TQ // CMP_STRIDE) * i, SUBLANE)
    psum_ref[...] = jnp.zeros(psum_ref.shape, F32)

    for h in range(r_heads):
        s = _dot(kvc, q_of(h)) + tabc_ref[0, h, pl.ds(j0, CMP_PAD), :] + negc
        m = jnp.maximum(jnp.max(s, axis=0, keepdims=True), F32_MIN)
        e = jnp.exp(s - m)
        den = jnp.sum(e, axis=0, keepdims=True)
        p = e * (1.0 / jnp.where(den > 0.0, den, 1.0))
        ocmp_ref[h] = _dot(vct, p.astype(BF16))
        psum_ref[...] += p

    psum = psum_ref[...]
    ovl = ovl_ref[...]
    hi = psum.astype(BF16)
    rem1 = psum - hi.astype(F32)
    mid = rem1.astype(BF16)
    lo = (rem1 - mid.astype(F32)).astype(BF16)
    p_slc = _dot(ovl, hi) + _dot(ovl, mid) + _dot(ovl, lo)
    blk = lax.broadcasted_iota(I32, (LANE, TQ), 0)
    cur = jnp.right_shift(t_row, 6)
    forced = (blk == 0) | (blk == cur) | (blk == cur - 1)
    admissible = (blk * SEL_BLOCK) <= t_row
    score = jnp.where(admissible, p_slc + jnp.where(forced, FORCE_BONUS, 0.0), NEG)
    score = jnp.where(blk < n_sel, score, -jnp.inf)
    sel = _topk_cols(score, min(N_SEL, n_sel), blk) & (score > 0.5 * NEG)
    negsel_ref[...] = jnp.where(sel, 0.0, -jnp.inf)

    def bias_of(m_lo):
        m_a = jnp.clip(m_lo, 0, N_BIAS_TILES - 1)
        m_b = jnp.clip(m_lo + 1, 0, N_BIAS_TILES - 1)
        return lambda h: jnp.concatenate([tab_ref[0, m_a, h], tab_ref[0, m_b, h]], axis=1)

    _reset_state(m_ref, l_ref, acc_ref)

    def sel_body(kt, carry):
        k0 = pl.multiple_of(kt * TK, TK)
        rows = [jnp.broadcast_to(negsel_ref[pl.ds((TK // SEL_BLOCK) * kt + a, 1), :], (SEL_BLOCK, TQ))
                for a in range(TK // SEL_BLOCK)]
        negm_ref[...] = jnp.where((k0 + krow) <= t_row, jnp.concatenate(rows, axis=0), -jnp.inf)
        _attend_tile(r_heads, q_of, kvs_ref[0, pl.ds(k0, TK), :], vst_ref[0, 0, kt], negm_ref,
                     bias_of((TQ // TK) * i - kt), m_ref, l_ref, acc_ref)
        return carry

    lax.fori_loop(0, (TQ // TK) * (i + 1), sel_body, 0)
    for h in range(r_heads):
        oslc_ref[h] = acc_ref[h, NSA_VT_ROWS - HEAD_DIM:, :] * (1.0 / l_ref[h])

    _reset_state(m_ref, l_ref, acc_ref)
    kt_hi = (TQ // TK) * (i + 1) - 1

    def win_body(j, carry):
        kt = kt_hi - j
        k0 = pl.multiple_of(jnp.maximum(kt, 0) * TK, TK)
        dist = t_row - (k0 + krow)
        ok = (dist >= 0) & (dist < jnp.where(kt >= 0, WINDOW, 0))
        negm_ref[...] = jnp.where(ok, 0.0, -jnp.inf)
        _attend_tile(r_heads, q_of, kvw_ref[0, pl.ds(k0, TK), :], vwt_ref[0, 0, jnp.maximum(kt, 0)],
                     negm_ref, bias_of(j - 1), m_ref, l_ref, acc_ref)
        return carry

    lax.fori_loop(0, (WINDOW + TQ) // TK, win_body, 0)

    gates = jax.nn.sigmoid(glt_ref[0, 0] + gbt_ref[0])
    for pair in range(r_heads // 2):
        outs = []
        for r in (2 * pair, 2 * pair + 1):
            o_win = acc_ref[r, NSA_VT_ROWS - HEAD_DIM:, :] * (1.0 / l_ref[r])
            c = N_BRANCH * r
            outs.append(gates[c:c + 1] * ocmp_ref[r] + gates[c + 1:c + 2] * oslc_ref[r]
                        + gates[c + 2:c + 3] * o_win)
        o_ref[0, :, pair * LANE:(pair + 1) * LANE] = jnp.concatenate(outs, axis=0).T.astype(o_ref.dtype)


def _nsa_attention(qt, kvc, vct, kvs, vst, kvw, vwt, glt, gbt, tab, tabc, ovl, n_sel):
    bsz, _, _, t = qt.shape
    g, r = NSA_KV_HEADS, NSA_GROUP
    nkt = t // TK
    once = dict(pipeline_mode=pl.Buffered(1))
    return pl.pallas_call(
        functools.partial(_nsa_kernel, n_sel=n_sel),
        grid=(bsz, g, t // TQ),
        in_specs=[pl.BlockSpec((1, r, LANE, TQ), lambda b, gg, i: (b, gg, 0, i)),
                  pl.BlockSpec((1, 1, CMP_PAD, LANE), lambda b, gg, i: (b, gg, 0, 0)),
                  pl.BlockSpec((1, 1, HEAD_DIM, CMP_PAD), lambda b, gg, i: (b, gg, 0, 0)),
                  pl.BlockSpec((1, t, LANE), lambda b, gg, i: (b, 0, gg)),
                  pl.BlockSpec((1, 1, nkt, NSA_VT_ROWS, TK), lambda b, gg, i: (b, gg, 0, 0, 0)),
                  pl.BlockSpec((1, t, LANE), lambda b, gg, i: (b, 0, gg)),
                  pl.BlockSpec((1, 1, nkt, NSA_VT_ROWS, TK), lambda b, gg, i: (b, gg, 0, 0, 0)),
                  pl.BlockSpec((1, 1, GATE_ROWS, TQ), lambda b, gg, i: (b, gg, 0, i)),
                  pl.BlockSpec((1, GATE_ROWS, TQ), lambda b, gg, i: (gg, 0, 0)),
                  pl.BlockSpec((1, N_BIAS_TILES, r, TK, LANE), lambda b, gg, i: (gg, 0, 0, 0, 0), **once),
                  pl.BlockSpec((1, r, 2 * CMP_PAD, TQ), lambda b, gg, i: (gg, 0, 0, 0), **once),
                  pl.BlockSpec((LANE, CMP_PAD), lambda b, gg, i: (0, 0))],
        out_specs=pl.BlockSpec((1, TQ, r * HEAD_DIM), lambda b, gg, i: (b, i, gg)),
        out_shape=jax.ShapeDtypeStruct((bsz, t, g * r * HEAD_DIM), BF16),
        scratch_shapes=[pltpu.VMEM((CMP_PAD, TQ), F32), pltpu.VMEM((LANE, TQ), F32),
                        pltpu.VMEM((TK, TQ), F32), pltpu.VMEM((r, 1, TQ), F32), pltpu.VMEM((r, 1, TQ), F32),
                        pltpu.VMEM((r, NSA_VT_ROWS, TQ), F32), pltpu.VMEM((r, HEAD_DIM, TQ), F32),
                        pltpu.VMEM((r, HEAD_DIM, TQ), F32)],
        compiler_params=pltpu.CompilerParams(
            dimension_semantics=("arbitrary", "arbitrary", "arbitrary"),
            vmem_limit_bytes=VMEM_LIMIT),
        name="nsa_attention",
    )(qt, kvc, vct, kvs, vst, kvw, vwt, glt, gbt, tab, tabc, ovl)


def _dsa_proj_kernel(x_ref, g_ref, w_ref, qn_ref, kvn_ref, kin_ref, wqu_ref, wuk_ref, wqi_ref,
                     qat_ref, qit_ref, wit_ref, ckv_ref, ckvt_ref, kidx_ref, qmem_ref):
    tm = x_ref.shape[1]
    y = _rms(x_ref[0], g_ref[...]).astype(BF16)
    c_q = _rms(_dot(y, w_ref[:, 0:Q_LORA]), qn_ref[...]).astype(BF16)
    c_kv = _rms(_dot(y, w_ref[:, Q_LORA:Q_LORA + KV_LORA]), kvn_ref[...])
    ckv_ref[0] = c_kv.astype(BF16)
    c_kv_t = c_kv.T
    for jt in range(tm // TK):
        ckvt_ref[0, jt] = c_kv_t[:, jt * TK:(jt + 1) * TK].astype(BF16)
    off = Q_LORA + KV_LORA
    k_idx = _rms(_dot(y, w_ref[:, off:off + LANE]), kin_ref[...], n=IDX_DIM)
    kidx_ref[0] = k_idx.astype(BF16)
    off += LANE
    w_idx = _dot(y, w_ref[:, off:off + LANE]) * (IDX_HEADS ** -0.5 * IDX_DIM ** -0.5)
    wit_ref[0] = w_idx.T[:IDX_HEADS, :]
    off += LANE
    qmem_ref[0] = (_dot(y, w_ref[:, off:off + N_MEM_HEADS * LANE]) * HEAD_DIM ** -0.5).astype(BF16)
    for h in range(IDX_HEADS):
        qit_ref[0, h] = _dot(c_q, wqi_ref[:, h * LANE:(h + 1) * LANE]).T.astype(BF16)
    for h in range(N_MIX_HEADS):
        q_h = (_dot(c_q, wqu_ref[:, h * LANE:(h + 1) * LANE]) * HEAD_DIM ** -0.5).astype(BF16)
        qat_ref[0, h] = _dot(q_h, wuk_ref[h]).T.astype(BF16)


def _dsa_proj(h, gain, w, qn, kvn, kin, wqu, wuk, wqi, tm):
    bsz, t, d = h.shape
    nkt = t // TK
    full = lambda a: pl.BlockSpec(a.shape, lambda b, i: (0,) * a.ndim)
    row = lambda n: pl.BlockSpec((1, tm, n), lambda b, i: (b, i, 0))
    return pl.pallas_call(
        _dsa_proj_kernel,
        grid=(bsz, t // tm),
        in_specs=[pl.BlockSpec((1, tm, d), lambda b, i: (b, i, 0)), full(gain), full(w), full(qn),
                  full(kvn), full(kin), full(wqu), full(wuk), full(wqi)],
        out_specs=[pl.BlockSpec((1, N_MIX_HEADS, LANE, tm), lambda b, i: (b, 0, 0, i)),
                   pl.BlockSpec((1, IDX_HEADS, LANE, tm), lambda b, i: (b, 0, 0, i)),
                   pl.BlockSpec((1, IDX_HEADS, tm), lambda b, i: (b, 0, i)),
                   row(LANE),
                   pl.BlockSpec((1, tm // TK, LANE, TK), lambda b, i: (b, i, 0, 0)),
                   row(LANE), row(N_MEM_HEADS * LANE)],
        out_shape=[jax.ShapeDtypeStruct((bsz, N_MIX_HEADS, LANE, t), BF16),
                   jax.ShapeDtypeStruct((bsz, IDX_HEADS, LANE, t), BF16),
                   jax.ShapeDtypeStruct((bsz, IDX_HEADS, t), F32),
                   jax.ShapeDtypeStruct((bsz, t, LANE), BF16),
                   jax.ShapeDtypeStruct((bsz, nkt, LANE, TK), BF16),
                   jax.ShapeDtypeStruct((bsz, t, LANE), BF16),
                   jax.ShapeDtypeStruct((bsz, t, N_MEM_HEADS * LANE), BF16)],
        compiler_params=pltpu.CompilerParams(dimension_semantics=("arbitrary", "arbitrary"),
                                             vmem_limit_bytes=VMEM_LIMIT),
        name="dsa_proj",
    )(h, gain, w, qn, kvn, kin, wqu, wuk, wqi)


def _dsa_kernel(qi_ref, wi_ref, qa_ref, kidx_ref, ckv_ref, ckvt_ref, tab_ref, o_ref,
                sc_ref, negm_ref, m_ref, l_ref, acc_ref, *, topk):
    i = pl.program_id(1)
    qs = i * TQ
    n_tiles = (TQ // TK) * (i + 1)
    t_row = qs + lax.broadcasted_iota(I32, (1, TQ), 1)
    krow = lax.broadcasted_iota(I32, (TK, TQ), 0)
    k_f = float(topk)

    def score_body(kt, carry):
        k0 = pl.multiple_of(kt * TK, TK)
        kk = kidx_ref[0, pl.ds(k0, TK), :]
        sc = jnp.maximum(_dot(kk, qi_ref[0, 0]), 0.0) * wi_ref[0, 0:1, :]
        for h in range(1, IDX_HEADS):
            sc = sc + jnp.maximum(_dot(kk, qi_ref[0, h]), 0.0) * wi_ref[0, h:h + 1, :]
        sc_ref[kt] = jnp.where((k0 + krow) <= t_row, sc, NEG)
        return carry

    lax.fori_loop(0, n_tiles, score_body, 0)

    def count(pred):
        def body(kt, acc):
            hit = pred(sc_ref[kt], kt).astype(F32)
            return acc + jnp.sum(hit.reshape(TK // SUBLANE, SUBLANE, TQ), axis=0)
        acc = lax.fori_loop(0, n_tiles, body, jnp.zeros((SUBLANE, TQ), F32))
        return jnp.sum(acc, axis=0, keepdims=True)

    short = t_row < topk

    def v_cond(c):
        return (c[0] < 32) & (c[3] > 0)

    def v_body(c):
        it, v, cnt_v, _ = c
        cand = v + jnp.left_shift(jnp.int32(1), 31 - it)
        cand_f = _key_to_float(cand)
        cnt = count(lambda sc, kt: sc >= cand_f)
        take = cnt >= k_f
        v = jnp.where(take, cand, v)
        cnt_v = jnp.where(take, cnt, cnt_v)
        open_cols = jnp.sum(((cnt_v != k_f) & ~short).astype(I32))
        return it + 1, v, cnt_v, open_cols

    _, v, cnt_v, open_cols = lax.while_loop(
        v_cond, v_body,
        (jnp.int32(0), jnp.full((1, TQ), INT_MIN, I32), jnp.full((1, TQ), 1e9, F32), jnp.int32(1)))
    thr = _key_to_float(v)

    def tie_cut(_):
        need = k_f - count(lambda sc, kt: sc > thr)

        def c_body(it, c):
            cand = c + jnp.left_shift(jnp.int32(1), 13 - it)
            cnt = count(lambda sc, kt: (sc == thr) & ((kt * TK + krow) < cand))
            return jnp.where(cnt < need, cand, c)

        return lax.fori_loop(0, 14, c_body, jnp.zeros((1, TQ), I32))

    cut = lax.cond(open_cols > 0, tie_cut, lambda _: jnp.full((1, TQ), 2 ** 30, I32), 0)

    _reset_state(m_ref, l_ref, acc_ref)
    q_of = lambda h: qa_ref[0, h]

    def att_body(kt, carry):
        k0 = pl.multiple_of(kt * TK, TK)
        sc = sc_ref[kt]
        kpos = k0 + krow
        chosen = short | (sc > thr) | ((sc == thr) & (kpos <= cut))
        negm_ref[...] = jnp.where(chosen & (kpos <= t_row), 0.0, -jnp.inf)
        m_lo = (TQ // TK) * i - kt
        m_a = jnp.clip(m_lo, 0, N_BIAS_TILES - 1)
        m_b = jnp.clip(m_lo + 1, 0, N_BIAS_TILES - 1)
        bias_of = lambda h: jnp.concatenate([tab_ref[m_a, h], tab_ref[m_b, h]], axis=1)
        _attend_tile(N_MIX_HEADS, q_of, ckv_ref[0, pl.ds(k0, TK), :], ckvt_ref[0, kt], negm_ref,
                     bias_of, m_ref, l_ref, acc_ref)
        return carry

    lax.fori_loop(0, n_tiles, att_body, 0)
    for h in range(N_MIX_HEADS):
        out = acc_ref[h] * (1.0 / l_ref[h])
        o_ref[0, :, h * LANE:(h + 1) * LANE] = out.T.astype(o_ref.dtype)


def _dsa_attention(qit, wit, qat, kidx, ckv, ckvt, tab, topk):
    bsz, _, _, t = qat.shape
    nkt = t // TK
    return pl.pallas_call(
        functools.partial(_dsa_kernel, topk=topk),
        grid=(bsz, t // TQ),
        in_specs=[pl.BlockSpec((1, IDX_HEADS, LANE, TQ), lambda b, i: (b, 0, 0, i)),
                  pl.BlockSpec((1, IDX_HEADS, TQ), lambda b, i: (b, 0, i)),
                  pl.BlockSpec((1, N_MIX_HEADS, LANE, TQ), lambda b, i: (b, 0, 0, i)),
                  pl.BlockSpec((1, t, LANE), lambda b, i: (b, 0, 0)),
                  pl.BlockSpec((1, t, LANE), lambda b, i: (b, 0, 0)),
                  pl.BlockSpec((1, nkt, LANE, TK), lambda b, i: (b, 0, 0, 0)),
                  pl.BlockSpec(tab.shape, lambda b, i: (0, 0, 0, 0), pipeline_mode=pl.Buffered(1))],
        out_specs=pl.BlockSpec((1, TQ, N_MIX_HEADS * LANE), lambda b, i: (b, i, 0)),
        out_shape=jax.ShapeDtypeStruct((bsz, t, N_MIX_HEADS * LANE), BF16),
        scratch_shapes=[pltpu.VMEM((nkt, TK, TQ), F32), pltpu.VMEM((TK, TQ), F32),
                        pltpu.VMEM((N_MIX_HEADS, 1, TQ), F32), pltpu.VMEM((N_MIX_HEADS, 1, TQ), F32),
                        pltpu.VMEM((N_MIX_HEADS, KV_LORA, TQ), F32)],
        compiler_params=pltpu.CompilerParams(dimension_semantics=("arbitrary", "arbitrary"),
                                             vmem_limit_bytes=VMEM_LIMIT),
        name="dsa_attention",
    )(qit, wit, qat, kidx, ckv, ckvt, tab)


def _post_kernel(h_ref, mix_ref, qmem_ref, kvm_ref, *rest, has_uv):
    if has_uv:
        wuv_ref, wmix_ref, wmem_ref, o_ref = rest
    else:
        wmix_ref, wmem_ref, o_ref = rest
    mix = mix_ref[0]
    if has_uv:
        mix = _dot(mix, wuv_ref[...]).astype(BF16)
    upd = _dot(mix, wmix_ref[...])
    qm = qmem_ref[0]
    for hm in range(N_MEM_HEADS):
        sl = slice(hm * LANE, (hm + 1) * LANE)
        kv = kvm_ref[0, :, sl]
        s = _dot_nt(qm[:, sl], kv)
        e = jnp.exp(s - jnp.max(s, axis=-1, keepdims=True))
        p = e / jnp.sum(e, axis=-1, keepdims=True)
        o_h = _dot(p.astype(BF16), kv).astype(BF16)
        upd = upd + _dot(o_h, wmem_ref[sl, :])
    o_ref[0] = h_ref[0] + upd


def _post(h, mix, qmem, kvm, w_uv, w_mix, w_mem, tm):
    bsz, t, d = h.shape
    has_uv = w_uv is not None
    full = lambda a: pl.BlockSpec(a.shape, lambda b, i: (0,) * a.ndim)
    weights = ([w_uv] if has_uv else []) + [w_mix, w_mem]
    return pl.pallas_call(
        functools.partial(_post_kernel, has_uv=has_uv),
        grid=(bsz, t // tm),
        in_specs=[pl.BlockSpec((1, tm, d), lambda b, i: (b, i, 0)),
                  pl.BlockSpec((1, tm, mix.shape[2]), lambda b, i: (b, i, 0)),
                  pl.BlockSpec((1, tm, qmem.shape[2]), lambda b, i: (b, i, 0)),
                  pl.BlockSpec((1,) + kvm.shape[1:], lambda b, i: (b, 0, 0))]
                 + [full(w) for w in weights],
        out_specs=pl.BlockSpec((1, tm, d), lambda b, i: (b, i, 0)),
        out_shape=jax.ShapeDtypeStruct((bsz, t, d), F32),
        compiler_params=pltpu.CompilerParams(dimension_semantics=("arbitrary", "arbitrary"),
                                             vmem_limit_bytes=VMEM_LIMIT),
        name="mem_attn_out_proj",
    )(h, mix, qmem, kvm, *weights)


def _ffn_kernel(h_ref, g_ref, wg_ref, wu_ref, wd_ref, gf_ref, o_ref, *, final_norm):
    h = h_ref[...]
    hn = _rms(h, g_ref[...]).astype(BF16)
    act = (jax.nn.silu(_dot(hn, wg_ref[...])) * _dot(hn, wu_ref[...])).astype(BF16)
    out = h + _dot(act, wd_ref[...])
    if final_norm:
        out = _rms(out, gf_ref[...])
    o_ref[...] = out


def _ffn(h2d, gain, wg, wu, wd, gain_final, final_norm, tm):
    m, d = h2d.shape
    const = lambda a: pl.BlockSpec(a.shape, lambda i: (0,) * a.ndim, pipeline_mode=pl.Buffered(1))
    return pl.pallas_call(
        functools.partial(_ffn_kernel, final_norm=final_norm),
        grid=(m // tm,),
        in_specs=[pl.BlockSpec((tm, d), lambda i: (i, 0)), const(gain), const(wg), const(wu),
                  const(wd), const(gain_final)],
        out_specs=pl.BlockSpec((tm, d), lambda i: (i, 0)),
        out_shape=jax.ShapeDtypeStruct((m, d), F32),
        compiler_params=pltpu.CompilerParams(dimension_semantics=("arbitrary",),
                                             vmem_limit_bytes=VMEM_LIMIT),
        name="swiglu_ffn",
    )(h2d, gain, wg, wu, wd, gain_final)


def _pad_heads(w, n_heads):
    d_in = w.shape[0]
    w = w.reshape(d_in, n_heads, HEAD_DIM)
    return jnp.pad(w, ((0, 0), (0, 0), (0, LANE - HEAD_DIM))).reshape(d_in, n_heads * LANE)


def _pad_cols(w, n):
    return jnp.pad(w, ((0, 0), (0, n - w.shape[1])))


def _value_rows(w_rows, n_heads):
    d_out = w_rows.shape[1]
    w = w_rows.reshape(n_heads, HEAD_DIM, d_out)
    return jnp.pad(w, ((0, 0), (LANE - HEAD_DIM, 0), (0, 0))).reshape(n_heads * LANE, d_out)


def _interleave_kv(k, v, n_heads):
    d_in = k.shape[0]
    kv = jnp.concatenate([k.reshape(d_in, n_heads, HEAD_DIM), v.reshape(d_in, n_heads, HEAD_DIM)], axis=2)
    return kv.reshape(d_in, n_heads * LANE)


def _bucket_bias(rel_bias, bucket_np):
    onehot = jax.nn.one_hot(jnp.asarray(bucket_np.reshape(-1)), N_REL_BUCKETS, dtype=F32)
    out = jnp.dot(onehot, rel_bias, precision=lax.Precision.HIGHEST)
    return out.reshape(bucket_np.shape + (rel_bias.shape[1],))


def _bias_tiles(rel_bias):
    m = np.arange(N_BIAS_TILES)[:, None, None]
    dist = LANE * m + np.arange(LANE)[None, None, :] - np.arange(TK)[None, :, None]
    return jnp.transpose(_bucket_bias(rel_bias, _rel_bucket_np(dist)), (0, 3, 1, 2))


def _bias_cmp_table(rel_bias):
    rel = np.arange(2 * CMP_PAD) - CMP_PAD
    dist = np.arange(TQ)[None, :] - CMP_STRIDE * rel[:, None] - (CMP_LEN - 1)
    return jnp.transpose(_bucket_bias(rel_bias, _rel_bucket_np(dist)), (2, 0, 1))


def _overlap_matrix(t):
    n_cmp = (t - CMP_LEN) // CMP_STRIDE + 1
    n_sel = t // SEL_BLOCK
    cs = np.arange(CMP_PAD) * CMP_STRIDE
    ss = np.arange(LANE) * SEL_BLOCK
    ov = (cs[None, :] <= ss[:, None] + SEL_BLOCK - 1) & (cs[None, :] + CMP_LEN - 1 >= ss[:, None])
    ov &= (np.arange(CMP_PAD) < n_cmp)[None, :] & (np.arange(LANE) < n_sel)[:, None]
    return jnp.asarray(ov, BF16)


def kernel(x, mem, rel_bias, norm_mix, norm_ffn, norm_mem, w_mem_kv, w_out, ffn_gate, ffn_up, ffn_down,
           nsa_w_in, nsa_gate_b, nsa_cmp_pos_k, nsa_cmp_pos_v,
           nsa_cmp_k_w1, nsa_cmp_k_b1, nsa_cmp_k_w2, nsa_cmp_k_b2,
           nsa_cmp_v_w1, nsa_cmp_v_b1, nsa_cmp_v_w2, nsa_cmp_v_b2,
           dsa_w_in, dsa_q_norm, dsa_kv_norm, dsa_w_q_up, dsa_w_uk, dsa_w_uv, dsa_w_q_idx, dsa_kidx_norm,
           norm_final):
    bsz, t, d = x.shape
    m_len = mem.shape[1]
    depth = norm_mix.shape[0]
    g, r, hh = NSA_KV_HEADS, NSA_GROUP, N_MIX_HEADS
    d_mix = hh * HEAD_DIM
    kvw_ = g * HEAD_DIM
    assert t % 1024 == 0 and t // SEL_BLOCK <= LANE and t // CMP_STRIDE <= CMP_PAD
    tm = 512
    h = x.astype(F32)

    tab = _bias_tiles(rel_bias)
    tab_nsa = jnp.transpose(tab.reshape(N_BIAS_TILES, g, r, TK, LANE), (1, 0, 2, 3, 4))
    tabc = _bias_cmp_table(rel_bias).reshape(g, r, 2 * CMP_PAD, TQ)
    ovl = _overlap_matrix(t)

    for layer in range(depth):
        j = layer // 2
        wkv = w_mem_kv[layer]
        wkv = _interleave_kv(wkv[:, :N_MEM_HEADS * HEAD_DIM], wkv[:, N_MEM_HEADS * HEAD_DIM:], N_MEM_HEADS)
        kvm = _rms_proj(mem.reshape(bsz * m_len, d), norm_mem[layer], wkv.astype(BF16), BF16, m_len)
        kvm = kvm.reshape(bsz, m_len, N_MEM_HEADS * LANE)
        w_o = w_out[layer]
        w_mem_o = _value_rows(w_o[d_mix:], N_MEM_HEADS).astype(BF16)

        if layer % 2 == 0:
            w = nsa_w_in[j]
            c = np.cumsum([0, d_mix, kvw_, kvw_, kvw_, kvw_, kvw_, kvw_, hh * N_BRANCH,
                           N_MEM_HEADS * HEAD_DIM])
            wq, wkc, wvc, wks, wvs, wkw, wvw, wgl, wqm = [w[:, c[k]:c[k + 1]] for k in range(9)]
            n_g = r * N_BRANCH
            wgl = jnp.concatenate([_pad_cols(wgl[:, gg * n_g:(gg + 1) * n_g], LANE) for gg in range(g)], 1)
            w_all = jnp.concatenate([_pad_heads(wq, hh), wkc, wvc, _interleave_kv(wks, wvs, g),
                                     _interleave_kv(wkw, wvw, g), wgl, _pad_heads(wqm, N_MEM_HEADS)], axis=1)
            qt, cmp_raw, kvs, vst, kvw, vwt, glt, qmem = _nsa_proj(h, norm_mix[layer], w_all.astype(BF16), tm)
            nc = t // CMP_STRIDE
            x2 = jnp.transpose(cmp_raw.reshape(bsz, t, 2 * g, HEAD_DIM), (0, 2, 1, 3))
            x2 = x2.reshape(bsz, 2 * g, nc, CMP_STRIDE * HEAD_DIM)
            pos = jnp.stack([nsa_cmp_pos_k[j], nsa_cmp_pos_v[j]]).reshape(2, 2, CMP_STRIDE * HEAD_DIM)
            w1 = jnp.stack([nsa_cmp_k_w1[j], nsa_cmp_v_w1[j]]).astype(BF16)
            b1 = jnp.stack([nsa_cmp_k_b1[j], nsa_cmp_v_b1[j]])[:, None, :]
            w2 = jnp.stack([jnp.pad(nsa_cmp_k_w2[j], ((0, 0), (0, HEAD_DIM))),
                            jnp.pad(nsa_cmp_v_w2[j], ((0, 0), (HEAD_DIM, 0)))]).astype(BF16)
            b2 = jnp.stack([jnp.pad(nsa_cmp_k_b2[j], (0, HEAD_DIM)),
                            jnp.pad(nsa_cmp_v_b2[j], (HEAD_DIM, 0))])[:, None, :]
            kvc = _compress(x2, pos, w1, b1, w2, b2)
            kvc = jnp.pad(kvc, ((0, 0), (0, 0), (0, CMP_PAD - nc), (0, 0)))
            vct = jnp.transpose(kvc[..., HEAD_DIM:], (0, 1, 3, 2))
            gb = jnp.pad(nsa_gate_b[j].reshape(g, n_g), ((0, 0), (0, GATE_ROWS - n_g)))
            gbt = jnp.broadcast_to(gb[:, :, None], (g, GATE_ROWS, TQ))
            mix = _nsa_attention(qt, kvc, vct, kvs, vst, kvw, vwt, glt, gbt, tab_nsa, tabc, ovl,
                                 t // SEL_BLOCK)
            w_uv = None
            w_mix = w_o[:d_mix].astype(BF16)
        else:
            w = dsa_w_in[j]
            c = np.cumsum([0, Q_LORA, KV_LORA, IDX_DIM, IDX_HEADS, N_MEM_HEADS * HEAD_DIM])
            wcq, wckv, wki, wwi, wqm = [w[:, c[k]:c[k + 1]] for k in range(5)]
            w_all = jnp.concatenate([wcq, wckv, _pad_cols(wki, LANE), _pad_cols(wwi, LANE),
                                     _pad_heads(wqm, N_MEM_HEADS)], axis=1).astype(BF16)
            wuk = jnp.transpose(dsa_w_uk[j], (1, 2, 0))
            wuk = jnp.pad(wuk, ((0, 0), (0, LANE - HEAD_DIM), (0, 0))).astype(BF16)
            qat, qit, wit, ckv, ckvt, kidx, qmem = _dsa_proj(
                h, norm_mix[layer][None], w_all, dsa_q_norm[j][None], dsa_kv_norm[j][None],
                _pad_cols(dsa_kidx_norm[j][None], LANE), _pad_heads(dsa_w_q_up[j], hh).astype(BF16), wuk,
                _pad_heads(dsa_w_q_idx[j], IDX_HEADS).astype(BF16), tm)
            mix = _dsa_attention(qit, wit, qat, kidx, ckv, ckvt, tab, min(DSA_TOPK, t // 4))
            wv = jnp.transpose(dsa_w_uv[j], (1, 0, 2))
            w_uv = (jnp.eye(hh, dtype=F32)[:, None, :, None] * wv[:, :, None, :]
                    ).reshape(hh * KV_LORA, d_mix).astype(BF16)
            w_mix = w_o[:d_mix].astype(BF16)

        h = _post(h, mix, qmem, kvm, w_uv, w_mix, w_mem_o, tm)
        last = layer == depth - 1
        h = _ffn(h.reshape(bsz * t, d), norm_ffn[layer][None], ffn_gate[layer].astype(BF16),
                 ffn_up[layer].astype(BF16), ffn_down[layer].astype(BF16), norm_final[None],
                 last, tm).reshape(bsz, t, d)
    return h.astype(x.dtype)
```

```python
import functools
import math

import numpy as np
import jax
import jax.numpy as jnp
from jax import lax
from jax.experimental import pallas as pl
from jax.experimental.pallas import tpu as pltpu

F32 = jnp.float32
BF16 = jnp.bfloat16
I32 = jnp.int32

NEG = -1e30
EPS = 1e-6
LOG2E = math.log2(math.e)
LANE = 128
SUBLANE = 8
BF16_ROWS = 16
HEAD_DIM = 64
N_MIX_HEADS = 12
N_MEM_HEADS = 4
N_REL_BUCKETS = 32
REL_MAX_EXACT = 16
REL_MAX_DIST = 2048
NSA_KV_HEADS = 2
NSA_GROUP = N_MIX_HEADS // NSA_KV_HEADS
N_BRANCH = 3
GATE_ROWS = 24
CMP_LEN = 32
CMP_STRIDE = 16
SEL_BLOCK = 64
N_SEL = 16
WINDOW = 512
FORCE_BONUS = 1e4
Q_LORA = 256
KV_LORA = 128
IDX_HEADS = 8
IDX_DIM = 64
DSA_TOPK = 256
TQ = 256
TK = 256
KS = LANE
N_BIAS_TILES = REL_MAX_DIST // KS + 2
CMP_PAD = 512
NSA_VT_ROWS = 2 * HEAD_DIM
DSA_VT_ROWS = KV_LORA
VMEM_LIMIT = 56 * 1024 * 1024
INT_MIN = -2 ** 31
F32_MIN = float(np.finfo(np.float32).min)


def _dot(a, b):
    return jnp.dot(a, b, preferred_element_type=F32)


def _dot_nt(a, b):
    return lax.dot_general(a, b, (((1,), (1,)), ((), ())), preferred_element_type=F32)


def _rms(x, gain, n=None):
    n = x.shape[-1] if n is None else n
    ms = jnp.sum(x * x, axis=-1, keepdims=True) * (1.0 / n)
    return x * lax.rsqrt(ms + EPS) * gain


def _rel_bucket_np(dist):
    n = np.maximum(dist, 0)
    nf = np.maximum(n, REL_MAX_EXACT).astype(np.float32)
    large = REL_MAX_EXACT + (np.log(nf / np.float32(REL_MAX_EXACT))
                             / np.float32(math.log(REL_MAX_DIST / REL_MAX_EXACT))
                             * np.float32(N_REL_BUCKETS - REL_MAX_EXACT)).astype(np.int32)
    large = np.minimum(large, N_REL_BUCKETS - 1)
    return np.where(n < REL_MAX_EXACT, n, large).astype(np.int32)


def _colsum(x):
    rows, n = x.shape
    return jnp.sum(jnp.sum(x.reshape(rows // SUBLANE, SUBLANE, n), axis=0), axis=0, keepdims=True)


def _key_to_float(v):
    bits = jnp.where(v >= 0, v, v ^ jnp.int32(0x7FFFFFFF))
    return pltpu.bitcast(bits, F32)


def _topk_cols(score, k, row_idx):
    n = score.shape[1]

    def vbody(it, v):
        cand = v + jnp.left_shift(jnp.int32(1), 31 - it)
        cnt = _colsum((score >= _key_to_float(cand)).astype(F32))
        return jnp.where(cnt >= k, cand, v)

    v = lax.fori_loop(0, 32, vbody, jnp.full((1, n), INT_MIN, I32))
    thr = _key_to_float(v)
    gt = score > thr
    eq = score == thr
    need = k - _colsum(gt.astype(F32))

    def cbody(it, c):
        cand = c + jnp.left_shift(jnp.int32(1), 6 - it)
        cnt = _colsum((eq & (row_idx < cand)).astype(F32))
        return jnp.where(cnt < need, cand, c)

    c = lax.fori_loop(0, 7, cbody, jnp.zeros((1, n), I32))
    return gt | (eq & (row_idx <= c))


def _bias_block(tile_of, m0):
    idx = lambda m: jnp.clip(m, 0, N_BIAS_TILES - 1)
    return [jnp.concatenate([tile_of(idx(m0 + b - a)) for b in range(TQ // LANE)], axis=1)
            for a in range(TK // KS)]


def _attend_tiles(n_heads, n_tiles, q_of, kv_of, vt_of, bias_of, fill_mask, negm_ref, s_ref, tmax_ref,
                  m_ref, acc_ref, l_ref=None):
    parts = [slice(a * KS, (a + 1) * KS) for a in range(TK // KS)]
    m_ref[...] = jnp.full(m_ref.shape, F32_MIN, F32)
    acc_ref[...] = jnp.zeros(acc_ref.shape, F32)
    if l_ref is not None:
        l_ref[...] = jnp.zeros(l_ref.shape, F32)

    def logits_of(j):
        fill_mask(j)
        kv = kv_of(j)
        bias_h = bias_of(j)

        def run(h):
            bias = bias_h(h)
            tmax = None
            for a, sl in enumerate(parts):
                s_a = _dot(kv[sl], q_of(h)) + bias[a] + negm_ref[sl, :]
                s_ref[h, sl, :] = s_a
                mx = jnp.max(s_a, axis=0, keepdims=True)
                tmax = mx if tmax is None else jnp.maximum(tmax, mx)
            tmax_ref[h] = tmax
        return run

    def accumulate(j):
        vt = vt_of(j)

        def run(h):
            m_old = m_ref[h]
            m_new = jnp.maximum(m_old, tmax_ref[h])
            alpha = jnp.exp2(m_old - m_new)
            m_ref[h] = m_new
            p = [jnp.exp2(s_ref[h, sl, :] - m_new) for sl in parts]
            if l_ref is not None:
                l_ref[h] = alpha * l_ref[h] + _colsum(sum(p[1:], p[0]))
            acc = alpha * acc_ref[h]
            for sl, p_a in zip(parts, p):
                acc = acc + _dot(vt[:, sl], p_a.astype(BF16))
            acc_ref[h] = acc
        return run

    first = logits_of(0)
    for h in range(n_heads):
        first(h)

    def body(j, carry):
        consume = accumulate(j)
        produce = logits_of(jnp.minimum(j + 1, n_tiles - 1))
        for h in range(n_heads):
            consume(h)
            produce(h)
        return carry

    lax.fori_loop(0, n_tiles, body, 0)


def _rms_proj_kernel(x_ref, g_ref, w_ref, o_ref):
    y = _rms(x_ref[...], g_ref[...]).astype(BF16)
    o_ref[...] = _dot(y, w_ref[...]).astype(o_ref.dtype)


def _rms_proj(x2d, gain, w_bf16, out_dtype, tm):
    m, d = x2d.shape
    n = w_bf16.shape[1]
    return pl.pallas_call(
        _rms_proj_kernel,
        grid=(m // tm,),
        in_specs=[pl.BlockSpec((tm, d), lambda i: (i, 0)),
                  pl.BlockSpec((1, d), lambda i: (0, 0)),
                  pl.BlockSpec((d, n), lambda i: (0, 0))],
        out_specs=pl.BlockSpec((tm, n), lambda i: (i, 0)),
        out_shape=jax.ShapeDtypeStruct((m, n), out_dtype),
        compiler_params=pltpu.CompilerParams(dimension_semantics=("arbitrary",),
                                             vmem_limit_bytes=VMEM_LIMIT),
        name="rms_proj",
    )(x2d, gain.reshape(1, d), w_bf16)


def _nsa_proj_kernel(x_ref, g_ref, w_ref, qt_ref, cmp_ref, kvs_ref, vst_ref, kvw_ref, vwt_ref,
                     glt_ref, qmem_ref):
    tm = x_ref.shape[1]
    y = _rms(x_ref[0], g_ref[...]).astype(BF16)
    off = 0
    for h in range(N_MIX_HEADS):
        acc = _dot(y, w_ref[:, off:off + LANE]) * (HEAD_DIM ** -0.5 * LOG2E)
        qt_ref[0, h] = acc.T.astype(BF16)
        off += LANE
    cmp_ref[0] = _dot(y, w_ref[:, off:off + 2 * LANE])
    off += 2 * LANE
    value_row = lax.broadcasted_iota(I32, (LANE, tm), 0) >= HEAD_DIM
    for kv_ref, vt_ref in ((kvs_ref, vst_ref), (kvw_ref, vwt_ref)):
        for g in range(NSA_KV_HEADS):
            acc = _dot(y, w_ref[:, off:off + LANE])
            off += LANE
            kv_ref[0, :, g * LANE:(g + 1) * LANE] = acc.astype(BF16)
            v_t = jnp.where(value_row, acc.T, 1.0).astype(BF16)
            for jt in range(tm // TK):
                vt_ref[0, g, jt] = v_t[:, jt * TK:(jt + 1) * TK]
    for g in range(NSA_KV_HEADS):
        acc = _dot(y, w_ref[:, off:off + LANE])
        off += LANE
        glt_ref[0, g] = acc.T[:GATE_ROWS, :]
    qmem_ref[0] = (_dot(y, w_ref[:, off:off + N_MEM_HEADS * LANE]) * HEAD_DIM ** -0.5).astype(BF16)


def _nsa_proj(h, gain, w, tm):
    bsz, t, d = h.shape
    g = NSA_KV_HEADS
    nkt = t // TK
    row = lambda n: pl.BlockSpec((1, tm, n), lambda b, i: (b, i, 0))
    vt_spec = pl.BlockSpec((1, g, tm // TK, NSA_VT_ROWS, TK), lambda b, i: (b, 0, i, 0, 0))
    return pl.pallas_call(
        _nsa_proj_kernel,
        grid=(bsz, t // tm),
        in_specs=[pl.BlockSpec((1, tm, d), lambda b, i: (b, i, 0)),
                  pl.BlockSpec((1, d), lambda b, i: (0, 0)),
                  pl.BlockSpec(w.shape, lambda b, i: (0, 0))],
        out_specs=[pl.BlockSpec((1, N_MIX_HEADS, LANE, tm), lambda b, i: (b, 0, 0, i)),
                   row(2 * LANE), row(g * LANE), vt_spec, row(g * LANE), vt_spec,
                   pl.BlockSpec((1, g, GATE_ROWS, tm), lambda b, i: (b, 0, 0, i)),
                   row(N_MEM_HEADS * LANE)],
        out_shape=[jax.ShapeDtypeStruct((bsz, N_MIX_HEADS, LANE, t), BF16),
                   jax.ShapeDtypeStruct((bsz, t, 2 * LANE), F32),
                   jax.ShapeDtypeStruct((bsz, t, g * LANE), BF16),
                   jax.ShapeDtypeStruct((bsz, g, nkt, NSA_VT_ROWS, TK), BF16),
                   jax.ShapeDtypeStruct((bsz, t, g * LANE), BF16),
                   jax.ShapeDtypeStruct((bsz, g, nkt, NSA_VT_ROWS, TK), BF16),
                   jax.ShapeDtypeStruct((bsz, g, GATE_ROWS, t), F32),
                   jax.ShapeDtypeStruct((bsz, t, N_MEM_HEADS * LANE), BF16)],
        compiler_params=pltpu.CompilerParams(dimension_semantics=("arbitrary", "arbitrary"),
                                             vmem_limit_bytes=VMEM_LIMIT),
        name="nsa_proj",
    )(h, gain.reshape(1, d), w)


def _compress_kernel(xk_ref, xv_ref, pos_ref, w1_ref, b1_ref, w2_ref, b2_ref, o_ref):
    nc = xk_ref.shape[2]
    half = xk_ref.shape[3]
    out = None
    for j, x_ref in enumerate((xk_ref, xv_ref)):
        x = x_ref[0, 0]
        top = _dot((x + pos_ref[j, 0:1, :]).astype(BF16), w1_ref[j, :half, :])
        bot = _dot((x + pos_ref[j, 1:2, :]).astype(BF16), w1_ref[j, half:, :])
        pre = top + pltpu.roll(bot, nc - 1, axis=0) + b1_ref[j]
        hid = jax.nn.gelu(pre)
        res = _dot(hid.astype(BF16), w2_ref[j]) + b2_ref[j]
        out = res if out is None else out + res
    o_ref[0, 0] = out.astype(o_ref.dtype)


def _compress(x2, pos, w1, b1, w2, b2):
    bsz, _, nc, width = x2.shape
    g = NSA_KV_HEADS
    return pl.pallas_call(
        _compress_kernel,
        grid=(bsz, g),
        in_specs=[pl.BlockSpec((1, 1, nc, width), lambda b, gg: (b, gg, 0, 0)),
                  pl.BlockSpec((1, 1, nc, width), lambda b, gg: (b, gg + NSA_KV_HEADS, 0, 0)),
                  pl.BlockSpec(pos.shape, lambda b, gg: (0, 0, 0)),
                  pl.BlockSpec(w1.shape, lambda b, gg: (0, 0, 0)),
                  pl.BlockSpec(b1.shape, lambda b, gg: (0, 0, 0)),
                  pl.BlockSpec(w2.shape, lambda b, gg: (0, 0, 0)),
                  pl.BlockSpec(b2.shape, lambda b, gg: (0, 0, 0))],
        out_specs=pl.BlockSpec((1, 1, nc, LANE), lambda b, gg: (b, gg, 0, 0)),
        out_shape=jax.ShapeDtypeStruct((bsz, g, nc, LANE), BF16),
        compiler_params=pltpu.CompilerParams(dimension_semantics=("arbitrary", "arbitrary"),
                                             vmem_limit_bytes=VMEM_LIMIT),
        name="nsa_compress",
    )(x2, x2, pos, w1, b1, w2, b2)


def _nsa_kernel(q_ref, kvc_ref, vct_ref, kvs_ref, vst_ref, kvw_ref, vwt_ref, glt_ref, gbt_ref,
                tab_ref, tabc_ref, ovl_ref, o_ref,
                psum_ref, negsel_ref, negm_ref, s_ref, tmax_ref, m_ref, acc_ref, ocmp_ref, oslc_ref,
                *, n_sel):
    r_heads = NSA_GROUP
    i = pl.program_id(2)
    qs = i * TQ
    t_row = qs + lax.broadcasted_iota(I32, (1, TQ), 1)
    krow = lax.broadcasted_iota(I32, (TK, TQ), 0)
    q_of = lambda h: q_ref[0, h]
    val = slice(NSA_VT_ROWS - HEAD_DIM, NSA_VT_ROWS)

    def normalised(h):
        return acc_ref[h, val, :] * (1.0 / acc_ref[h, 0:1, :])

    kvc = kvc_ref[0, 0]
    vct = vct_ref[0, 0]
    n_row = lax.broadcasted_iota(I32, (CMP_PAD, TQ), 0)
    negc = jnp.where((CMP_STRIDE * n_row + (CMP_LEN - 1)) <= t_row, 0.0, -jnp.inf)
    j0 = pl.multiple_of(CMP_PAD - (TQ // CMP_STRIDE) * i, SUBLANE)
    psum_ref[...] = jnp.zeros(psum_ref.shape, F32)

    cparts = [slice(a * KS, (a + 1) * KS) for a in range(CMP_PAD // KS)]
    for h in range(r_heads):
        s = [_dot(kvc[sl], q_of(h)) + tabc_ref[0, h, pl.ds(j0 + sl.start, KS), :] + negc[sl] for sl in cparts]
        m = functools.reduce(jnp.maximum, [jnp.max(s_a, axis=0, keepdims=True) for s_a in s])
        m = jnp.maximum(m, F32_MIN)
        e = [jnp.exp2(s_a - m) for s_a in s]
        den = _colsum(functools.reduce(jnp.add, e))
        inv = 1.0 / jnp.where(den > 0.0, den, 1.0)
        acc = functools.reduce(jnp.add, [_dot(vct[:, sl], e_a.astype(BF16)) for sl, e_a in zip(cparts, e)])
        ocmp_ref[h] = acc[LANE - HEAD_DIM:] * inv
        for sl, e_a in zip(cparts, e):
            psum_ref[sl, :] += e_a * inv

    psum = psum_ref[...]
    ovl = ovl_ref[...]
    hi = psum.astype(BF16)
    rem1 = psum - hi.astype(F32)
    mid = rem1.astype(BF16)
    lo = (rem1 - mid.astype(F32)).astype(BF16)
    p_slc = _dot(ovl, hi) + _dot(ovl, mid) + _dot(ovl, lo)
    blk = lax.broadcasted_iota(I32, (LANE, TQ), 0)
    cur = jnp.right_shift(t_row, 6)
    forced = (blk == 0) | (blk == cur) | (blk == cur - 1)
    admissible = (blk * SEL_BLOCK) <= t_row
    score = jnp.where(admissible, p_slc + jnp.where(forced, FORCE_BONUS, 0.0), NEG)
    score = jnp.where(blk < n_sel, score, -jnp.inf)
    sel = _topk_cols(score, min(N_SEL, n_sel), blk) & (score > 0.5 * NEG)
    negsel_ref[...] = jnp.where(sel, 0.0, -jnp.inf)

    def bias_of(m0):
        return lambda h: _bias_block(lambda m: tab_ref[0, m, h], m0)

    def key_rows(ref, kt):
        return ref[0, pl.ds(pl.multiple_of(kt * TK, TK), TK), :]

    def sel_mask(kt):
        rows = [jnp.broadcast_to(negsel_ref[pl.ds((TK // SEL_BLOCK) * kt + a, 1), :], (SEL_BLOCK, TQ))
                for a in range(TK // SEL_BLOCK)]
        negm_ref[...] = jnp.where((kt * TK + krow) <= t_row, jnp.concatenate(rows, axis=0), -jnp.inf)

    _attend_tiles(r_heads, i + 1, q_of, lambda kt: key_rows(kvs_ref, kt), lambda kt: vst_ref[0, 0, kt],
                  lambda kt: bias_of((TQ // KS) * (i - kt)), sel_mask, negm_ref, s_ref, tmax_ref,
                  m_ref, acc_ref)
    for h in range(r_heads):
        oslc_ref[h] = normalised(h)

    def win_mask(j):
        kt = i - j
        dist = t_row - (jnp.maximum(kt, 0) * TK + krow)
        ok = (dist >= 0) & (dist < jnp.where(kt >= 0, WINDOW, 0))
        negm_ref[...] = jnp.where(ok, 0.0, -jnp.inf)

    _attend_tiles(r_heads, (WINDOW + TQ) // TK, q_of, lambda j: key_rows(kvw_ref, jnp.maximum(i - j, 0)),
                  lambda j: vwt_ref[0, 0, jnp.maximum(i - j, 0)], lambda j: bias_of((TQ // KS) * j),
                  win_mask, negm_ref, s_ref, tmax_ref, m_ref, acc_ref)

    gates = jax.nn.sigmoid(glt_ref[0, 0] + gbt_ref[0])
    for pair in range(r_heads // 2):
        outs = []
        for r in (2 * pair, 2 * pair + 1):
            c = N_BRANCH * r
            outs.append(gates[c:c + 1] * ocmp_ref[r] + gates[c + 1:c + 2] * oslc_ref[r]
                        + gates[c + 2:c + 3] * normalised(r))
        o_ref[0, :, pair * LANE:(pair + 1) * LANE] = jnp.concatenate(outs, axis=0).T.astype(o_ref.dtype)


def _nsa_attention(qt, kvc, vct, kvs, vst, kvw, vwt, glt, gbt, tab, tabc, ovl, n_sel):
    bsz, _, _, t = qt.shape
    g, r = NSA_KV_HEADS, NSA_GROUP
    nkt = t // TK
    once = dict(pipeline_mode=pl.Buffered(1))
    return pl.pallas_call(
        functools.partial(_nsa_kernel, n_sel=n_sel),
        grid=(bsz, g, t // TQ),
        in_specs=[pl.BlockSpec((1, r, LANE, TQ), lambda b, gg, i: (b, gg, 0, i)),
                  pl.BlockSpec((1, 1, CMP_PAD, LANE), lambda b, gg, i: (b, gg, 0, 0)),
                  pl.BlockSpec((1, 1, LANE, CMP_PAD), lambda b, gg, i: (b, gg, 0, 0)),
                  pl.BlockSpec((1, t, LANE), lambda b, gg, i: (b, 0, gg)),
                  pl.BlockSpec((1, 1, nkt, NSA_VT_ROWS, TK), lambda b, gg, i: (b, gg, 0, 0, 0)),
                  pl.BlockSpec((1, t, LANE), lambda b, gg, i: (b, 0, gg)),
                  pl.BlockSpec((1, 1, nkt, NSA_VT_ROWS, TK), lambda b, gg, i: (b, gg, 0, 0, 0)),
                  pl.BlockSpec((1, 1, GATE_ROWS, TQ), lambda b, gg, i: (b, gg, 0, i)),
                  pl.BlockSpec((1, GATE_ROWS, TQ), lambda b, gg, i: (gg, 0, 0)),
                  pl.BlockSpec((1, N_BIAS_TILES, r, KS, LANE), lambda b, gg, i: (gg, 0, 0, 0, 0), **once),
                  pl.BlockSpec((1, r, 2 * CMP_PAD, TQ), lambda b, gg, i: (gg, 0, 0, 0), **once),
                  pl.BlockSpec((LANE, CMP_PAD), lambda b, gg, i: (0, 0))],
        out_specs=pl.BlockSpec((1, TQ, r * HEAD_DIM), lambda b, gg, i: (b, i, gg)),
        out_shape=jax.ShapeDtypeStruct((bsz, t, g * r * HEAD_DIM), BF16),
        scratch_shapes=[pltpu.VMEM((CMP_PAD, TQ), F32), pltpu.VMEM((LANE, TQ), F32),
                        pltpu.VMEM((TK, TQ), F32), pltpu.VMEM((r, TK, TQ), F32),
                        pltpu.VMEM((r, 1, TQ), F32), pltpu.VMEM((r, 1, TQ), F32),
                        pltpu.VMEM((r, NSA_VT_ROWS, TQ), F32), pltpu.VMEM((r, HEAD_DIM, TQ), F32),
                        pltpu.VMEM((r, HEAD_DIM, TQ), F32)],
        compiler_params=pltpu.CompilerParams(
            dimension_semantics=("arbitrary", "arbitrary", "arbitrary"),
            vmem_limit_bytes=VMEM_LIMIT),
        name="nsa_attention",
    )(qt, kvc, vct, kvs, vst, kvw, vwt, glt, gbt, tab, tabc, ovl)


def _dsa_proj_kernel(x_ref, g_ref, w_ref, qn_ref, kvn_ref, kin_ref, wqu_ref, wuk_ref, wqi_ref,
                     qat_ref, qit_ref, wit_ref, ckv_ref, ckvt_ref, kidx_ref, qmem_ref):
    tm = x_ref.shape[1]
    y = _rms(x_ref[0], g_ref[...]).astype(BF16)
    c_q = _rms(_dot(y, w_ref[:, 0:Q_LORA]), qn_ref[...]).astype(BF16)
    c_kv = _rms(_dot(y, w_ref[:, Q_LORA:Q_LORA + KV_LORA]), kvn_ref[...])
    ckv_ref[0] = c_kv.astype(BF16)
    c_kv_t = c_kv.T.astype(BF16)
    for jt in range(tm // TK):
        ckvt_ref[0, jt] = c_kv_t[:, jt * TK:(jt + 1) * TK]
    off = Q_LORA + KV_LORA
    k_idx = _rms(_dot(y, w_ref[:, off:off + LANE]), kin_ref[...], n=IDX_DIM)
    kidx_ref[0] = k_idx.astype(BF16)
    off += LANE
    w_idx = _dot(y, w_ref[:, off:off + LANE]) * (IDX_HEADS ** -0.5 * IDX_DIM ** -0.5)
    wit_ref[0] = w_idx.T[:IDX_HEADS, :]
    off += LANE
    qmem_ref[0] = (_dot(y, w_ref[:, off:off + N_MEM_HEADS * LANE]) * HEAD_DIM ** -0.5).astype(BF16)
    for h in range(IDX_HEADS):
        qit_ref[0, h] = _dot(c_q, wqi_ref[:, h * LANE:(h + 1) * LANE]).T.astype(BF16)
    for h in range(N_MIX_HEADS):
        q_h = (_dot(c_q, wqu_ref[:, h * LANE:(h + 1) * LANE]) * HEAD_DIM ** -0.5).astype(BF16)
        qat_ref[0, h] = (_dot(q_h, wuk_ref[h]) * LOG2E).T.astype(BF16)


def _dsa_proj(h, gain, w, qn, kvn, kin, wqu, wuk, wqi, tm):
    bsz, t, d = h.shape
    nkt = t // TK
    full = lambda a: pl.BlockSpec(a.shape, lambda b, i: (0,) * a.ndim)
    row = lambda n: pl.BlockSpec((1, tm, n), lambda b, i: (b, i, 0))
    return pl.pallas_call(
        _dsa_proj_kernel,
        grid=(bsz, t // tm),
        in_specs=[pl.BlockSpec((1, tm, d), lambda b, i: (b, i, 0)), full(gain), full(w), full(qn),
                  full(kvn), full(kin), full(wqu), full(wuk), full(wqi)],
        out_specs=[pl.BlockSpec((1, N_MIX_HEADS, LANE, tm), lambda b, i: (b, 0, 0, i)),
                   pl.BlockSpec((1, IDX_HEADS, LANE, tm), lambda b, i: (b, 0, 0, i)),
                   pl.BlockSpec((1, IDX_HEADS, tm), lambda b, i: (b, 0, i)),
                   row(LANE),
                   pl.BlockSpec((1, tm // TK, DSA_VT_ROWS, TK), lambda b, i: (b, i, 0, 0)),
                   row(LANE), row(N_MEM_HEADS * LANE)],
        out_shape=[jax.ShapeDtypeStruct((bsz, N_MIX_HEADS, LANE, t), BF16),
                   jax.ShapeDtypeStruct((bsz, IDX_HEADS, LANE, t), BF16),
                   jax.ShapeDtypeStruct((bsz, IDX_HEADS, t), F32),
                   jax.ShapeDtypeStruct((bsz, t, LANE), BF16),
                   jax.ShapeDtypeStruct((bsz, nkt, DSA_VT_ROWS, TK), BF16),
                   jax.ShapeDtypeStruct((bsz, t, LANE), BF16),
                   jax.ShapeDtypeStruct((bsz, t, N_MEM_HEADS * LANE), BF16)],
        compiler_params=pltpu.CompilerParams(dimension_semantics=("arbitrary", "arbitrary"),
                                             vmem_limit_bytes=VMEM_LIMIT),
        name="dsa_proj",
    )(h, gain, w, qn, kvn, kin, wqu, wuk, wqi)


def _dsa_kernel(qi_ref, wi_ref, qa_ref, kidx_ref, ckv_ref, ckvt_ref, tab_ref, o_ref,
                sc_ref, negm_ref, s_ref, tmax_ref, m_ref, l_ref, acc_ref, *, topk):
    i = pl.program_id(1)
    qs = i * TQ
    n_tiles = i + 1
    sub = TK // KS
    t_row = qs + lax.broadcasted_iota(I32, (1, TQ), 1)
    krow = lax.broadcasted_iota(I32, (KS, TQ), 0)
    k_f = float(topk)

    def score_body(kt, carry):
        for a in range(sub):
            k0 = pl.multiple_of(kt * TK + a * KS, KS)
            kk = kidx_ref[0, pl.ds(k0, KS), :]
            sc = jnp.maximum(_dot(kk, qi_ref[0, 0]), 0.0) * wi_ref[0, 0:1, :]
            for h in range(1, IDX_HEADS):
                sc = sc + jnp.maximum(_dot(kk, qi_ref[0, h]), 0.0) * wi_ref[0, h:h + 1, :]
            sc_ref[sub * kt + a] = jnp.where((k0 + krow) <= t_row, sc, NEG)
        return carry

    lax.fori_loop(0, n_tiles, score_body, 0)

    def count(pred):
        def body(kt, acc):
            for a in range(sub):
                hit = pred(sc_ref[sub * kt + a], kt * TK + a * KS).astype(F32)
                acc = acc + jnp.sum(hit.reshape(KS // SUBLANE, SUBLANE, TQ), axis=0)
            return acc
        acc = lax.fori_loop(0, n_tiles, body, jnp.zeros((SUBLANE, TQ), F32))
        return jnp.sum(acc, axis=0, keepdims=True)

    short = t_row < topk

    def v_cond(c):
        return (c[0] < 32) & (c[3] > 0)

    def v_body(c):
        it, v, cnt_v, _ = c
        cand = v + jnp.left_shift(jnp.int32(1), 31 - it)
        cand_f = _key_to_float(cand)
        cnt = count(lambda sc, k0: sc >= cand_f)
        take = cnt >= k_f
        v = jnp.where(take, cand, v)
        cnt_v = jnp.where(take, cnt, cnt_v)
        open_cols = jnp.sum(((cnt_v != k_f) & ~short).astype(I32))
        return it + 1, v, cnt_v, open_cols

    _, v, cnt_v, open_cols = lax.while_loop(
        v_cond, v_body,
        (jnp.int32(0), jnp.full((1, TQ), INT_MIN, I32), jnp.full((1, TQ), 1e9, F32), jnp.int32(1)))
    thr = _key_to_float(v)

    def tie_cut(_):
        need = k_f - count(lambda sc, k0: sc > thr)

        def c_body(it, c):
            cand = c + jnp.left_shift(jnp.int32(1), 13 - it)
            cnt = count(lambda sc, k0: (sc == thr) & ((k0 + krow) < cand))
            return jnp.where(cnt < need, cand, c)

        return lax.fori_loop(0, 14, c_body, jnp.zeros((1, TQ), I32))

    cut = lax.cond(open_cols > 0, tie_cut, lambda _: jnp.full((1, TQ), 2 ** 30, I32), 0)

    def att_mask(kt):
        for a in range(sub):
            sc = sc_ref[sub * kt + a]
            kpos = kt * TK + a * KS + krow
            chosen = short | (sc > thr) | ((sc == thr) & (kpos <= cut))
            negm_ref[a * KS:(a + 1) * KS, :] = jnp.where(chosen & (kpos <= t_row), 0.0, -jnp.inf)

    _attend_tiles(N_MIX_HEADS, n_tiles, lambda h: qa_ref[0, h],
                  lambda kt: ckv_ref[0, pl.ds(pl.multiple_of(kt * TK, TK), TK), :],
                  lambda kt: ckvt_ref[0, kt],
                  lambda kt: (lambda h: _bias_block(lambda m: tab_ref[m, h], (TQ // KS) * (i - kt))),
                  att_mask, negm_ref, s_ref, tmax_ref, m_ref, acc_ref, l_ref)
    for h in range(N_MIX_HEADS):
        out = acc_ref[h] * (1.0 / l_ref[h])
        o_ref[0, :, h * LANE:(h + 1) * LANE] = out.T.astype(o_ref.dtype)


def _dsa_attention(qit, wit, qat, kidx, ckv, ckvt, tab, topk):
    bsz, _, _, t = qat.shape
    nkt = t // TK
    return pl.pallas_call(
        functools.partial(_dsa_kernel, topk=topk),
        grid=(bsz, t // TQ),
        in_specs=[pl.BlockSpec((1, IDX_HEADS, LANE, TQ), lambda b, i: (b, 0, 0, i)),
                  pl.BlockSpec((1, IDX_HEADS, TQ), lambda b, i: (b, 0, i)),
                  pl.BlockSpec((1, N_MIX_HEADS, LANE, TQ), lambda b, i: (b, 0, 0, i)),
                  pl.BlockSpec((1, t, LANE), lambda b, i: (b, 0, 0)),
                  pl.BlockSpec((1, t, LANE), lambda b, i: (b, 0, 0)),
                  pl.BlockSpec((1, nkt, DSA_VT_ROWS, TK), lambda b, i: (b, 0, 0, 0)),
                  pl.BlockSpec(tab.shape, lambda b, i: (0, 0, 0, 0), pipeline_mode=pl.Buffered(1))],
        out_specs=pl.BlockSpec((1, TQ, N_MIX_HEADS * LANE), lambda b, i: (b, i, 0)),
        out_shape=jax.ShapeDtypeStruct((bsz, t, N_MIX_HEADS * LANE), BF16),
        scratch_shapes=[pltpu.VMEM((t // KS, KS, TQ), F32), pltpu.VMEM((TK, TQ), F32),
                        pltpu.VMEM((N_MIX_HEADS, TK, TQ), F32), pltpu.VMEM((N_MIX_HEADS, 1, TQ), F32),
                        pltpu.VMEM((N_MIX_HEADS, 1, TQ), F32), pltpu.VMEM((N_MIX_HEADS, 1, TQ), F32),
                        pltpu.VMEM((N_MIX_HEADS, DSA_VT_ROWS, TQ), F32)],
        compiler_params=pltpu.CompilerParams(dimension_semantics=("arbitrary", "arbitrary"),
                                             vmem_limit_bytes=VMEM_LIMIT),
        name="dsa_attention",
    )(qit, wit, qat, kidx, ckv, ckvt, tab)


def _post_kernel(h_ref, mix_ref, qmem_ref, kvm_ref, *rest, has_uv):
    if has_uv:
        wuv_ref, wmix_ref, wmem_ref, o_ref = rest
    else:
        wmix_ref, wmem_ref, o_ref = rest
    mix = mix_ref[0]
    if has_uv:
        mix = _dot(mix, wuv_ref[...]).astype(BF16)
    upd = _dot(mix, wmix_ref[...])
    qm = qmem_ref[0]
    for hm in range(N_MEM_HEADS):
        sl = slice(hm * LANE, (hm + 1) * LANE)
        kv = kvm_ref[0, :, sl]
        s = _dot_nt(qm[:, sl], kv)
        e = jnp.exp(s - jnp.max(s, axis=-1, keepdims=True))
        p = e / jnp.sum(e, axis=-1, keepdims=True)
        o_h = _dot(p.astype(BF16), kv).astype(BF16)
        upd = upd + _dot(o_h, wmem_ref[sl, :])
    o_ref[0] = h_ref[0] + upd


def _post(h, mix, qmem, kvm, w_uv, w_mix, w_mem, tm):
    bsz, t, d = h.shape
    has_uv = w_uv is not None
    full = lambda a: pl.BlockSpec(a.shape, lambda b, i: (0,) * a.ndim)
    weights = ([w_uv] if has_uv else []) + [w_mix, w_mem]
    return pl.pallas_call(
        functools.partial(_post_kernel, has_uv=has_uv),
        grid=(bsz, t // tm),
        in_specs=[pl.BlockSpec((1, tm, d), lambda b, i: (b, i, 0)),
                  pl.BlockSpec((1, tm, mix.shape[2]), lambda b, i: (b, i, 0)),
                  pl.BlockSpec((1, tm, qmem.shape[2]), lambda b, i: (b, i, 0)),
                  pl.BlockSpec((1,) + kvm.shape[1:], lambda b, i: (b, 0, 0))]
                 + [full(w) for w in weights],
        out_specs=pl.BlockSpec((1, tm, d), lambda b, i: (b, i, 0)),
        out_shape=jax.ShapeDtypeStruct((bsz, t, d), F32),
        compiler_params=pltpu.CompilerParams(dimension_semantics=("arbitrary", "arbitrary"),
                                             vmem_limit_bytes=VMEM_LIMIT),
        name="mem_attn_out_proj",
    )(h, mix, qmem, kvm, *weights)


def _ffn_kernel(h_ref, g_ref, wg_ref, wu_ref, wd_ref, gf_ref, o_ref, *, final_norm):
    h = h_ref[...]
    hn = _rms(h, g_ref[...]).astype(BF16)
    act = (jax.nn.silu(_dot(hn, wg_ref[...])) * _dot(hn, wu_ref[...])).astype(BF16)
    out = h + _dot(act, wd_ref[...])
    if final_norm:
        out = _rms(out, gf_ref[...])
    o_ref[...] = out


def _ffn(h2d, gain, wg, wu, wd, gain_final, final_norm, tm):
    m, d = h2d.shape
    const = lambda a: pl.BlockSpec(a.shape, lambda i: (0,) * a.ndim, pipeline_mode=pl.Buffered(1))
    return pl.pallas_call(
        functools.partial(_ffn_kernel, final_norm=final_norm),
        grid=(m // tm,),
        in_specs=[pl.BlockSpec((tm, d), lambda i: (i, 0)), const(gain), const(wg), const(wu),
                  const(wd), const(gain_final)],
        out_specs=pl.BlockSpec((tm, d), lambda i: (i, 0)),
        out_shape=jax.ShapeDtypeStruct((m, d), F32),
        compiler_params=pltpu.CompilerParams(dimension_semantics=("arbitrary",),
                                             vmem_limit_bytes=VMEM_LIMIT),
        name="swiglu_ffn",
    )(h2d, gain, wg, wu, wd, gain_final)


def _pad_heads(w, n_heads):
    d_in = w.shape[0]
    w = w.reshape(d_in, n_heads, HEAD_DIM)
    return jnp.pad(w, ((0, 0), (0, 0), (0, LANE - HEAD_DIM))).reshape(d_in, n_heads * LANE)


def _pad_cols(w, n):
    return jnp.pad(w, ((0, 0), (0, n - w.shape[1])))


def _value_rows(w_rows, n_heads):
    d_out = w_rows.shape[1]
    w = w_rows.reshape(n_heads, HEAD_DIM, d_out)
    return jnp.pad(w, ((0, 0), (LANE - HEAD_DIM, 0), (0, 0))).reshape(n_heads * LANE, d_out)


def _interleave_kv(k, v, n_heads):
    d_in = k.shape[0]
    kv = jnp.concatenate([k.reshape(d_in, n_heads, HEAD_DIM), v.reshape(d_in, n_heads, HEAD_DIM)], axis=2)
    return kv.reshape(d_in, n_heads * LANE)


def _bucket_bias(rel_bias, bucket_np):
    onehot = jax.nn.one_hot(jnp.asarray(bucket_np.reshape(-1)), N_REL_BUCKETS, dtype=F32)
    out = jnp.dot(onehot, rel_bias, precision=lax.Precision.HIGHEST) * LOG2E
    return out.reshape(bucket_np.shape + (rel_bias.shape[1],))


def _bias_tiles(rel_bias):
    m = np.arange(N_BIAS_TILES)[:, None, None]
    dist = LANE * m + np.arange(LANE)[None, None, :] - np.arange(KS)[None, :, None]
    return jnp.transpose(_bucket_bias(rel_bias, _rel_bucket_np(dist)), (0, 3, 1, 2))


def _bias_cmp_table(rel_bias):
    rel = np.arange(2 * CMP_PAD) - CMP_PAD
    dist = np.arange(TQ)[None, :] - CMP_STRIDE * rel[:, None] - (CMP_LEN - 1)
    return jnp.transpose(_bucket_bias(rel_bias, _rel_bucket_np(dist)), (2, 0, 1))


def _overlap_matrix(t):
    n_cmp = (t - CMP_LEN) // CMP_STRIDE + 1
    n_sel = t // SEL_BLOCK
    cs = np.arange(CMP_PAD) * CMP_STRIDE
    ss = np.arange(LANE) * SEL_BLOCK
    ov = (cs[None, :] <= ss[:, None] + SEL_BLOCK - 1) & (cs[None, :] + CMP_LEN - 1 >= ss[:, None])
    ov &= (np.arange(CMP_PAD) < n_cmp)[None, :] & (np.arange(LANE) < n_sel)[:, None]
    return jnp.asarray(ov, BF16)


def kernel(x, mem, rel_bias, norm_mix, norm_ffn, norm_mem, w_mem_kv, w_out, ffn_gate, ffn_up, ffn_down,
           nsa_w_in, nsa_gate_b, nsa_cmp_pos_k, nsa_cmp_pos_v,
           nsa_cmp_k_w1, nsa_cmp_k_b1, nsa_cmp_k_w2, nsa_cmp_k_b2,
           nsa_cmp_v_w1, nsa_cmp_v_b1, nsa_cmp_v_w2, nsa_cmp_v_b2,
           dsa_w_in, dsa_q_norm, dsa_kv_norm, dsa_w_q_up, dsa_w_uk, dsa_w_uv, dsa_w_q_idx, dsa_kidx_norm,
           norm_final):
    bsz, t, d = x.shape
    m_len = mem.shape[1]
    depth = norm_mix.shape[0]
    g, r, hh = NSA_KV_HEADS, NSA_GROUP, N_MIX_HEADS
    d_mix = hh * HEAD_DIM
    kvw_ = g * HEAD_DIM
    assert t % 1024 == 0 and t // SEL_BLOCK <= LANE and t // CMP_STRIDE <= CMP_PAD
    tm = 512
    h = x.astype(F32)

    tab = _bias_tiles(rel_bias)
    tab_nsa = jnp.transpose(tab.reshape(N_BIAS_TILES, g, r, KS, LANE), (1, 0, 2, 3, 4))
    tabc = _bias_cmp_table(rel_bias).reshape(g, r, 2 * CMP_PAD, TQ)
    ovl = _overlap_matrix(t)

    for layer in range(depth):
        j = layer // 2
        wkv = w_mem_kv[layer]
        wkv = _interleave_kv(wkv[:, :N_MEM_HEADS * HEAD_DIM], wkv[:, N_MEM_HEADS * HEAD_DIM:], N_MEM_HEADS)
        kvm = _rms_proj(mem.reshape(bsz * m_len, d), norm_mem[layer], wkv.astype(BF16), BF16, m_len)
        kvm = kvm.reshape(bsz, m_len, N_MEM_HEADS * LANE)
        w_o = w_out[layer]
        w_mem_o = _value_rows(w_o[d_mix:], N_MEM_HEADS).astype(BF16)
        w_mix = w_o[:d_mix].astype(BF16)

        if layer % 2 == 0:
            w = nsa_w_in[j]
            c = np.cumsum([0, d_mix, kvw_, kvw_, kvw_, kvw_, kvw_, kvw_, hh * N_BRANCH,
                           N_MEM_HEADS * HEAD_DIM])
            wq, wkc, wvc, wks, wvs, wkw, wvw, wgl, wqm = [w[:, c[k]:c[k + 1]] for k in range(9)]
            n_g = r * N_BRANCH
            wgl = jnp.concatenate([_pad_cols(wgl[:, gg * n_g:(gg + 1) * n_g], LANE) for gg in range(g)], 1)
            w_all = jnp.concatenate([_pad_heads(wq, hh), wkc, wvc, _interleave_kv(wks, wvs, g),
                                     _interleave_kv(wkw, wvw, g), wgl, _pad_heads(wqm, N_MEM_HEADS)], axis=1)
            qt, cmp_raw, kvs, vst, kvw, vwt, glt, qmem = _nsa_proj(h, norm_mix[layer], w_all.astype(BF16), tm)
            nc = t // CMP_STRIDE
            x2 = jnp.transpose(cmp_raw.reshape(bsz, t, 2 * g, HEAD_DIM), (0, 2, 1, 3))
            x2 = x2.reshape(bsz, 2 * g, nc, CMP_STRIDE * HEAD_DIM)
            pos = jnp.stack([nsa_cmp_pos_k[j], nsa_cmp_pos_v[j]]).reshape(2, 2, CMP_STRIDE * HEAD_DIM)
            w1 = jnp.stack([nsa_cmp_k_w1[j], nsa_cmp_v_w1[j]]).astype(BF16)
            b1 = jnp.stack([nsa_cmp_k_b1[j], nsa_cmp_v_b1[j]])[:, None, :]
            w2 = jnp.stack([jnp.pad(nsa_cmp_k_w2[j], ((0, 0), (0, HEAD_DIM))),
                            jnp.pad(nsa_cmp_v_w2[j], ((0, 0), (HEAD_DIM, 0)))]).astype(BF16)
            b2 = jnp.stack([jnp.pad(nsa_cmp_k_b2[j], (0, HEAD_DIM)),
                            jnp.pad(nsa_cmp_v_b2[j], (HEAD_DIM, 0))])[:, None, :]
            kvc = _compress(x2, pos, w1, b1, w2, b2)
            kvc = jnp.pad(kvc, ((0, 0), (0, 0), (0, CMP_PAD - nc), (0, 0)))
            vct = jnp.transpose(kvc, (0, 1, 3, 2))
            gb = jnp.pad(nsa_gate_b[j].reshape(g, n_g), ((0, 0), (0, GATE_ROWS - n_g)))
            gbt = jnp.broadcast_to(gb[:, :, None], (g, GATE_ROWS, TQ))
            mix = _nsa_attention(qt, kvc, vct, kvs, vst, kvw, vwt, glt, gbt, tab_nsa, tabc, ovl,
                                 t // SEL_BLOCK)
            w_uv = None
        else:
            w = dsa_w_in[j]
            c = np.cumsum([0, Q_LORA, KV_LORA, IDX_DIM, IDX_HEADS, N_MEM_HEADS * HEAD_DIM])
            wcq, wckv, wki, wwi, wqm = [w[:, c[k]:c[k + 1]] for k in range(5)]
            w_all = jnp.concatenate([wcq, wckv, _pad_cols(wki, LANE), _pad_cols(wwi, LANE),
                                     _pad_heads(wqm, N_MEM_HEADS)], axis=1).astype(BF16)
            wuk = jnp.transpose(dsa_w_uk[j], (1, 2, 0))
            wuk = jnp.pad(wuk, ((0, 0), (0, LANE - HEAD_DIM), (0, 0))).astype(BF16)
            qat, qit, wit, ckv, ckvt, kidx, qmem = _dsa_proj(
                h, norm_mix[layer][None], w_all, dsa_q_norm[j][None], dsa_kv_norm[j][None],
                _pad_cols(dsa_kidx_norm[j][None], LANE), _pad_heads(dsa_w_q_up[j], hh).astype(BF16), wuk,
                _pad_heads(dsa_w_q_idx[j], IDX_HEADS).astype(BF16), tm)
            mix = _dsa_attention(qit, wit, qat, kidx, ckv, ckvt, tab, min(DSA_TOPK, t // 4))
            wv = jnp.transpose(dsa_w_uv[j], (1, 0, 2))
            w_uv = (jnp.eye(hh, dtype=F32)[:, None, :, None] * wv[:, :, None, :]
                    ).reshape(hh * KV_LORA, d_mix).astype(BF16)

        h = _post(h, mix, qmem, kvm, w_uv, w_mix, w_mem_o, tm)
        last = layer == depth - 1
        h = _ffn(h.reshape(bsz * t, d), norm_ffn[layer][None], ffn_gate[layer].astype(BF16),
                 ffn_up[layer].astype(BF16), ffn_down[layer].astype(BF16), norm_final[None],
                 last, tm).reshape(bsz, t, d)
    return h.astype(x.dtype)
```

```python
import functools
import math

import numpy as np
import jax
import jax.numpy as jnp
from jax import lax
from jax.experimental import pallas as pl
from jax.experimental.pallas import tpu as pltpu

F32 = jnp.float32
BF16 = jnp.bfloat16
I32 = jnp.int32

NEG = -1e30
EPS = 1e-6
LOG2E = math.log2(math.e)
LANE = 128
SUBLANE = 8
BF16_ROWS = 16
HEAD_DIM = 64
N_MIX_HEADS = 12
N_MEM_HEADS = 4
N_REL_BUCKETS = 32
REL_MAX_EXACT = 16
REL_MAX_DIST = 2048
NSA_KV_HEADS = 2
NSA_GROUP = N_MIX_HEADS // NSA_KV_HEADS
N_BRANCH = 3
GATE_ROWS = 24
CMP_LEN = 32
CMP_STRIDE = 16
SEL_BLOCK = 64
N_SEL = 16
WINDOW = 512
FORCE_BONUS = 1e4
Q_LORA = 256
KV_LORA = 128
IDX_HEADS = 8
IDX_DIM = 64
DSA_TOPK = 256
TQ = 256
TK = 256
KS = LANE
N_BIAS_TILES = REL_MAX_DIST // KS + 2
CMP_PAD = 512
NSA_VT_ROWS = 2 * HEAD_DIM
DSA_VT_ROWS = KV_LORA
VMEM_LIMIT = 56 * 1024 * 1024
INT_MIN = -2 ** 31
F32_MIN = float(np.finfo(np.float32).min)


def _dot(a, b):
    return jnp.dot(a, b, preferred_element_type=F32)


def _dot_nt(a, b):
    return lax.dot_general(a, b, (((1,), (1,)), ((), ())), preferred_element_type=F32)


def _rms(x, gain, n=None):
    n = x.shape[-1] if n is None else n
    ms = jnp.sum(x * x, axis=-1, keepdims=True) * (1.0 / n)
    return x * lax.rsqrt(ms + EPS) * gain


def _rel_bucket_np(dist):
    n = np.maximum(dist, 0)
    nf = np.maximum(n, REL_MAX_EXACT).astype(np.float32)
    large = REL_MAX_EXACT + (np.log(nf / np.float32(REL_MAX_EXACT))
                             / np.float32(math.log(REL_MAX_DIST / REL_MAX_EXACT))
                             * np.float32(N_REL_BUCKETS - REL_MAX_EXACT)).astype(np.int32)
    large = np.minimum(large, N_REL_BUCKETS - 1)
    return np.where(n < REL_MAX_EXACT, n, large).astype(np.int32)


def _spread_heads(x):
    rows, n = x.shape
    low = lax.broadcasted_iota(I32, (rows, LANE), 1) < HEAD_DIM
    out = []
    for j in range(n // LANE):
        pair = x[:, j * LANE:(j + 1) * LANE]
        out += [jnp.where(low, pair, 0.0), jnp.where(low, pltpu.roll(pair, HEAD_DIM, axis=1), 0.0)]
    return jnp.concatenate(out, axis=1)


def _colsum(x):
    rows, n = x.shape
    return jnp.sum(jnp.sum(x.reshape(rows // SUBLANE, SUBLANE, n), axis=0), axis=0, keepdims=True)


def _key_to_float(v):
    bits = jnp.where(v >= 0, v, v ^ jnp.int32(0x7FFFFFFF))
    return pltpu.bitcast(bits, F32)


def _topk_cols(score, k, row_idx):
    n = score.shape[1]

    def vbody(it, v):
        cand = v + jnp.left_shift(jnp.int32(1), 31 - it)
        cnt = _colsum((score >= _key_to_float(cand)).astype(F32))
        return jnp.where(cnt >= k, cand, v)

    v = lax.fori_loop(0, 32, vbody, jnp.full((1, n), INT_MIN, I32))
    thr = _key_to_float(v)
    gt = score > thr
    eq = score == thr
    need = k - _colsum(gt.astype(F32))

    def cbody(it, c):
        cand = c + jnp.left_shift(jnp.int32(1), 6 - it)
        cnt = _colsum((eq & (row_idx < cand)).astype(F32))
        return jnp.where(cnt < need, cand, c)

    c = lax.fori_loop(0, 7, cbody, jnp.zeros((1, n), I32))
    return gt | (eq & (row_idx <= c))


def _bias_block(tile_of, m0):
    idx = lambda m: jnp.clip(m, 0, N_BIAS_TILES - 1)
    return [jnp.concatenate([tile_of(idx(m0 + b - a)) for b in range(TQ // LANE)], axis=1)
            for a in range(TK // KS)]


def _attend_tiles(n_heads, n_tiles, q_of, kv_of, vt_of, bias_of, fill_mask, negm_ref, s_ref, tmax_ref,
                  m_ref, acc_ref, l_ref=None):
    parts = [slice(a * KS, (a + 1) * KS) for a in range(TK // KS)]
    m_ref[...] = jnp.full(m_ref.shape, F32_MIN, F32)
    acc_ref[...] = jnp.zeros(acc_ref.shape, F32)
    if l_ref is not None:
        l_ref[...] = jnp.zeros(l_ref.shape, F32)

    def logits_of(j):
        fill_mask(j)
        kv = kv_of(j)
        bias_h = bias_of(j)

        def run(h):
            bias = bias_h(h)
            tmax = None
            for a, sl in enumerate(parts):
                s_a = _dot(kv[sl], q_of(h)) + bias[a] + negm_ref[sl, :]
                s_ref[h, sl, :] = s_a
                mx = jnp.max(s_a, axis=0, keepdims=True)
                tmax = mx if tmax is None else jnp.maximum(tmax, mx)
            tmax_ref[h] = tmax
        return run

    def accumulate(j):
        vt = vt_of(j)

        def run(h):
            m_old = m_ref[h]
            m_new = jnp.maximum(m_old, tmax_ref[h])
            alpha = jnp.exp2(m_old - m_new)
            m_ref[h] = m_new
            p = [jnp.exp2(s_ref[h, sl, :] - m_new) for sl in parts]
            if l_ref is not None:
                l_ref[h] = alpha * l_ref[h] + _colsum(sum(p[1:], p[0]))
            acc = alpha * acc_ref[h]
            for sl, p_a in zip(parts, p):
                acc = acc + _dot(vt[:, sl], p_a.astype(BF16))
            acc_ref[h] = acc
        return run

    first = logits_of(0)
    for h in range(n_heads):
        first(h)

    def body(j, carry):
        consume = accumulate(j)
        produce = logits_of(jnp.minimum(j + 1, n_tiles - 1))
        for h in range(n_heads):
            consume(h)
            produce(h)
        return carry

    lax.fori_loop(0, n_tiles, body, 0)


def _rms_proj_kernel(x_ref, g_ref, w_ref, o_ref):
    y = _rms(x_ref[...], g_ref[...]).astype(BF16)
    o_ref[...] = _dot(y, w_ref[...]).astype(o_ref.dtype)


def _rms_proj(x2d, gain, w_bf16, out_dtype, tm):
    m, d = x2d.shape
    n = w_bf16.shape[1]
    return pl.pallas_call(
        _rms_proj_kernel,
        grid=(m // tm,),
        in_specs=[pl.BlockSpec((tm, d), lambda i: (i, 0)),
                  pl.BlockSpec((1, d), lambda i: (0, 0)),
                  pl.BlockSpec((d, n), lambda i: (0, 0))],
        out_specs=pl.BlockSpec((tm, n), lambda i: (i, 0)),
        out_shape=jax.ShapeDtypeStruct((m, n), out_dtype),
        compiler_params=pltpu.CompilerParams(dimension_semantics=("arbitrary",),
                                             vmem_limit_bytes=VMEM_LIMIT),
        name="rms_proj",
    )(x2d, gain.reshape(1, d), w_bf16)


def _nsa_proj_kernel(x_ref, g_ref, w_ref, qt_ref, cmp_ref, kvs_ref, vst_ref, kvw_ref, vwt_ref,
                     glt_ref, qmem_ref):
    tm = x_ref.shape[1]
    y = _rms(x_ref[0], g_ref[...]).astype(BF16)
    value_row = lax.broadcasted_iota(I32, (LANE, tm), 0) >= HEAD_DIM
    off = 0
    for pair in range(N_MIX_HEADS // 2):
        acc_t = (_dot(y, w_ref[:, off:off + LANE]) * (HEAD_DIM ** -0.5 * LOG2E)).T
        qt_ref[0, 2 * pair] = jnp.where(value_row, 0.0, acc_t).astype(BF16)
        qt_ref[0, 2 * pair + 1] = jnp.where(value_row, 0.0, pltpu.roll(acc_t, HEAD_DIM, axis=0)).astype(BF16)
        off += LANE
    cmp_ref[0] = _dot(y, w_ref[:, off:off + 2 * LANE])
    off += 2 * LANE
    for kv_ref, vt_ref in ((kvs_ref, vst_ref), (kvw_ref, vwt_ref)):
        for g in range(NSA_KV_HEADS):
            acc = _dot(y, w_ref[:, off:off + LANE])
            off += LANE
            kv_ref[0, :, g * LANE:(g + 1) * LANE] = acc.astype(BF16)
            v_t = jnp.where(value_row, acc.T, 1.0).astype(BF16)
            for jt in range(tm // TK):
                vt_ref[0, g, jt] = v_t[:, jt * TK:(jt + 1) * TK]
    for g in range(NSA_KV_HEADS):
        acc = _dot(y, w_ref[:, off:off + LANE])
        off += LANE
        glt_ref[0, g] = acc.T[:GATE_ROWS, :]
    qmem_ref[0] = _spread_heads(_dot(y, w_ref[:, off:off + N_MEM_HEADS * HEAD_DIM]) * HEAD_DIM ** -0.5
                                ).astype(BF16)


def _nsa_proj(h, gain, w, tm):
    bsz, t, d = h.shape
    g = NSA_KV_HEADS
    nkt = t // TK
    row = lambda n: pl.BlockSpec((1, tm, n), lambda b, i: (b, i, 0))
    vt_spec = pl.BlockSpec((1, g, tm // TK, NSA_VT_ROWS, TK), lambda b, i: (b, 0, i, 0, 0))
    return pl.pallas_call(
        _nsa_proj_kernel,
        grid=(bsz, t // tm),
        in_specs=[pl.BlockSpec((1, tm, d), lambda b, i: (b, i, 0)),
                  pl.BlockSpec((1, d), lambda b, i: (0, 0)),
                  pl.BlockSpec(w.shape, lambda b, i: (0, 0))],
        out_specs=[pl.BlockSpec((1, N_MIX_HEADS, LANE, tm), lambda b, i: (b, 0, 0, i)),
                   row(2 * LANE), row(g * LANE), vt_spec, row(g * LANE), vt_spec,
                   pl.BlockSpec((1, g, GATE_ROWS, tm), lambda b, i: (b, 0, 0, i)),
                   row(N_MEM_HEADS * LANE)],
        out_shape=[jax.ShapeDtypeStruct((bsz, N_MIX_HEADS, LANE, t), BF16),
                   jax.ShapeDtypeStruct((bsz, t, 2 * LANE), F32),
                   jax.ShapeDtypeStruct((bsz, t, g * LANE), BF16),
                   jax.ShapeDtypeStruct((bsz, g, nkt, NSA_VT_ROWS, TK), BF16),
                   jax.ShapeDtypeStruct((bsz, t, g * LANE), BF16),
                   jax.ShapeDtypeStruct((bsz, g, nkt, NSA_VT_ROWS, TK), BF16),
                   jax.ShapeDtypeStruct((bsz, g, GATE_ROWS, t), F32),
                   jax.ShapeDtypeStruct((bsz, t, N_MEM_HEADS * LANE), BF16)],
        compiler_params=pltpu.CompilerParams(dimension_semantics=("arbitrary", "arbitrary"),
                                             vmem_limit_bytes=VMEM_LIMIT),
        name="nsa_proj",
    )(h, gain.reshape(1, d), w)


def _compress_kernel(xk_ref, xv_ref, pos_ref, w1_ref, b1_ref, w2_ref, b2_ref, o_ref):
    nc = xk_ref.shape[2]
    half = xk_ref.shape[3]
    out = None
    for j, x_ref in enumerate((xk_ref, xv_ref)):
        x = x_ref[0, 0]
        top = _dot((x + pos_ref[j, 0:1, :]).astype(BF16), w1_ref[j, :half, :])
        bot = _dot((x + pos_ref[j, 1:2, :]).astype(BF16), w1_ref[j, half:, :])
        pre = top + pltpu.roll(bot, nc - 1, axis=0) + b1_ref[j]
        hid = jax.nn.gelu(pre)
        res = _dot(hid.astype(BF16), w2_ref[j]) + b2_ref[j]
        out = res if out is None else out + res
    o_ref[0, 0] = out.astype(o_ref.dtype)


def _compress(x2, pos, w1, b1, w2, b2):
    bsz, _, nc, width = x2.shape
    g = NSA_KV_HEADS
    return pl.pallas_call(
        _compress_kernel,
        grid=(bsz, g),
        in_specs=[pl.BlockSpec((1, 1, nc, width), lambda b, gg: (b, gg, 0, 0)),
                  pl.BlockSpec((1, 1, nc, width), lambda b, gg: (b, gg + NSA_KV_HEADS, 0, 0)),
                  pl.BlockSpec(pos.shape, lambda b, gg: (0, 0, 0)),
                  pl.BlockSpec(w1.shape, lambda b, gg: (0, 0, 0)),
                  pl.BlockSpec(b1.shape, lambda b, gg: (0, 0, 0)),
                  pl.BlockSpec(w2.shape, lambda b, gg: (0, 0, 0)),
                  pl.BlockSpec(b2.shape, lambda b, gg: (0, 0, 0))],
        out_specs=pl.BlockSpec((1, 1, nc, LANE), lambda b, gg: (b, gg, 0, 0)),
        out_shape=jax.ShapeDtypeStruct((bsz, g, nc, LANE), BF16),
        compiler_params=pltpu.CompilerParams(dimension_semantics=("arbitrary", "arbitrary"),
                                             vmem_limit_bytes=VMEM_LIMIT),
        name="nsa_compress",
    )(x2, x2, pos, w1, b1, w2, b2)


def _nsa_kernel(q_ref, kvc_ref, vct_ref, kvs_ref, vst_ref, kvw_ref, vwt_ref, glt_ref, gbt_ref,
                tab_ref, tabc_ref, ovl_ref, o_ref,
                psum_ref, negsel_ref, negm_ref, s_ref, tmax_ref, m_ref, acc_ref, ocmp_ref, oslc_ref,
                *, n_sel):
    r_heads = NSA_GROUP
    i = pl.program_id(2)
    qs = i * TQ
    t_row = qs + lax.broadcasted_iota(I32, (1, TQ), 1)
    krow = lax.broadcasted_iota(I32, (TK, TQ), 0)
    q_of = lambda h: q_ref[0, h]
    val = slice(NSA_VT_ROWS - HEAD_DIM, NSA_VT_ROWS)

    def normalised(h):
        return acc_ref[h, val, :] * (1.0 / acc_ref[h, 0:1, :])

    kvc = kvc_ref[0, 0]
    vct = vct_ref[0, 0]
    n_row = lax.broadcasted_iota(I32, (CMP_PAD, TQ), 0)
    negc = jnp.where((CMP_STRIDE * n_row + (CMP_LEN - 1)) <= t_row, 0.0, -jnp.inf)
    j0 = pl.multiple_of(CMP_PAD - (TQ // CMP_STRIDE) * i, SUBLANE)
    psum_ref[...] = jnp.zeros(psum_ref.shape, F32)

    cparts = [slice(a * KS, (a + 1) * KS) for a in range(CMP_PAD // KS)]
    for h in range(r_heads):
        s = [_dot(kvc[sl], q_of(h)) + tabc_ref[0, h, pl.ds(j0 + sl.start, KS), :] + negc[sl] for sl in cparts]
        m = functools.reduce(jnp.maximum, [jnp.max(s_a, axis=0, keepdims=True) for s_a in s])
        m = jnp.maximum(m, F32_MIN)
        e = [jnp.exp2(s_a - m) for s_a in s]
        den = _colsum(functools.reduce(jnp.add, e))
        inv = 1.0 / jnp.where(den > 0.0, den, 1.0)
        acc = functools.reduce(jnp.add, [_dot(vct[:, sl], e_a.astype(BF16)) for sl, e_a in zip(cparts, e)])
        ocmp_ref[h] = acc[LANE - HEAD_DIM:] * inv
        for sl, e_a in zip(cparts, e):
            psum_ref[sl, :] += e_a * inv

    psum = psum_ref[...]
    ovl = ovl_ref[...]
    hi = psum.astype(BF16)
    rem1 = psum - hi.astype(F32)
    mid = rem1.astype(BF16)
    lo = (rem1 - mid.astype(F32)).astype(BF16)
    p_slc = _dot(ovl, hi) + _dot(ovl, mid) + _dot(ovl, lo)
    blk = lax.broadcasted_iota(I32, (LANE, TQ), 0)
    cur = jnp.right_shift(t_row, 6)
    forced = (blk == 0) | (blk == cur) | (blk == cur - 1)
    admissible = (blk * SEL_BLOCK) <= t_row
    score = jnp.where(admissible, p_slc + jnp.where(forced, FORCE_BONUS, 0.0), NEG)
    score = jnp.where(blk < n_sel, score, -jnp.inf)
    sel = _topk_cols(score, min(N_SEL, n_sel), blk) & (score > 0.5 * NEG)
    negsel_ref[...] = jnp.where(sel, 0.0, -jnp.inf)

    def bias_of(m0):
        return lambda h: _bias_block(lambda m: tab_ref[0, m, h], m0)

    def key_rows(ref, kt):
        return ref[0, pl.ds(pl.multiple_of(kt * TK, TK), TK), :]

    def sel_mask(kt):
        rows = [jnp.broadcast_to(negsel_ref[pl.ds((TK // SEL_BLOCK) * kt + a, 1), :], (SEL_BLOCK, TQ))
                for a in range(TK // SEL_BLOCK)]
        negm_ref[...] = jnp.where((kt * TK + krow) <= t_row, jnp.concatenate(rows, axis=0), -jnp.inf)

    _attend_tiles(r_heads, i + 1, q_of, lambda kt: key_rows(kvs_ref, kt), lambda kt: vst_ref[0, 0, kt],
                  lambda kt: bias_of((TQ // KS) * (i - kt)), sel_mask, negm_ref, s_ref, tmax_ref,
                  m_ref, acc_ref)
    for h in range(r_heads):
        oslc_ref[h] = normalised(h)

    def win_mask(j):
        kt = i - j
        dist = t_row - (jnp.maximum(kt, 0) * TK + krow)
        ok = (dist >= 0) & (dist < jnp.where(kt >= 0, WINDOW, 0))
        negm_ref[...] = jnp.where(ok, 0.0, -jnp.inf)

    _attend_tiles(r_heads, (WINDOW + TQ) // TK, q_of, lambda j: key_rows(kvw_ref, jnp.maximum(i - j, 0)),
                  lambda j: vwt_ref[0, 0, jnp.maximum(i - j, 0)], lambda j: bias_of((TQ // KS) * j),
                  win_mask, negm_ref, s_ref, tmax_ref, m_ref, acc_ref)

    gates = jax.nn.sigmoid(glt_ref[0, 0] + gbt_ref[0])
    for pair in range(r_heads // 2):
        outs = []
        for r in (2 * pair, 2 * pair + 1):
            c = N_BRANCH * r
            outs.append(gates[c:c + 1] * ocmp_ref[r] + gates[c + 1:c + 2] * oslc_ref[r]
                        + gates[c + 2:c + 3] * normalised(r))
        o_ref[0, :, pair * LANE:(pair + 1) * LANE] = jnp.concatenate(outs, axis=0).T.astype(o_ref.dtype)


def _nsa_attention(qt, kvc, vct, kvs, vst, kvw, vwt, glt, gbt, tab, tabc, ovl, n_sel):
    bsz, _, _, t = qt.shape
    g, r = NSA_KV_HEADS, NSA_GROUP
    nkt = t // TK
    once = dict(pipeline_mode=pl.Buffered(1))
    return pl.pallas_call(
        functools.partial(_nsa_kernel, n_sel=n_sel),
        grid=(bsz, g, t // TQ),
        in_specs=[pl.BlockSpec((1, r, LANE, TQ), lambda b, gg, i: (b, gg, 0, i)),
                  pl.BlockSpec((1, 1, CMP_PAD, LANE), lambda b, gg, i: (b, gg, 0, 0)),
                  pl.BlockSpec((1, 1, LANE, CMP_PAD), lambda b, gg, i: (b, gg, 0, 0)),
                  pl.BlockSpec((1, t, LANE), lambda b, gg, i: (b, 0, gg)),
                  pl.BlockSpec((1, 1, nkt, NSA_VT_ROWS, TK), lambda b, gg, i: (b, gg, 0, 0, 0)),
                  pl.BlockSpec((1, t, LANE), lambda b, gg, i: (b, 0, gg)),
                  pl.BlockSpec((1, 1, nkt, NSA_VT_ROWS, TK), lambda b, gg, i: (b, gg, 0, 0, 0)),
                  pl.BlockSpec((1, 1, GATE_ROWS, TQ), lambda b, gg, i: (b, gg, 0, i)),
                  pl.BlockSpec((1, GATE_ROWS, TQ), lambda b, gg, i: (gg, 0, 0)),
                  pl.BlockSpec((1, N_BIAS_TILES, r, KS, LANE), lambda b, gg, i: (gg, 0, 0, 0, 0), **once),
                  pl.BlockSpec((1, r, 2 * CMP_PAD, TQ), lambda b, gg, i: (gg, 0, 0, 0), **once),
                  pl.BlockSpec((LANE, CMP_PAD), lambda b, gg, i: (0, 0))],
        out_specs=pl.BlockSpec((1, TQ, r * HEAD_DIM), lambda b, gg, i: (b, i, gg)),
        out_shape=jax.ShapeDtypeStruct((bsz, t, g * r * HEAD_DIM), BF16),
        scratch_shapes=[pltpu.VMEM((CMP_PAD, TQ), F32), pltpu.VMEM((LANE, TQ), F32),
                        pltpu.VMEM((TK, TQ), F32), pltpu.VMEM((r, TK, TQ), F32),
                        pltpu.VMEM((r, 1, TQ), F32), pltpu.VMEM((r, 1, TQ), F32),
                        pltpu.VMEM((r, NSA_VT_ROWS, TQ), F32), pltpu.VMEM((r, HEAD_DIM, TQ), F32),
                        pltpu.VMEM((r, HEAD_DIM, TQ), F32)],
        compiler_params=pltpu.CompilerParams(
            dimension_semantics=("arbitrary", "arbitrary", "arbitrary"),
            vmem_limit_bytes=VMEM_LIMIT),
        name="nsa_attention",
    )(qt, kvc, vct, kvs, vst, kvw, vwt, glt, gbt, tab, tabc, ovl)


def _dsa_proj_kernel(x_ref, g_ref, w_ref, qn_ref, kvn_ref, kin_ref, wqu_ref, wuk_ref, wqi_ref,
                     qat_ref, qit_ref, wit_ref, ckv_ref, ckvt_ref, kidx_ref, qmem_ref):
    tm = x_ref.shape[1]
    y = _rms(x_ref[0], g_ref[...]).astype(BF16)
    c_q = _rms(_dot(y, w_ref[:, 0:Q_LORA]), qn_ref[...]).astype(BF16)
    c_kv = _rms(_dot(y, w_ref[:, Q_LORA:Q_LORA + KV_LORA]), kvn_ref[...])
    ckv_ref[0] = c_kv.astype(BF16)
    c_kv_t = c_kv.T.astype(BF16)
    for jt in range(tm // TK):
        ckvt_ref[0, jt] = c_kv_t[:, jt * TK:(jt + 1) * TK]
    off = Q_LORA + KV_LORA
    k_idx = _rms(_dot(y, w_ref[:, off:off + LANE]), kin_ref[...], n=IDX_DIM)
    kidx_ref[0] = k_idx.astype(BF16)
    off += LANE
    w_idx = _dot(y, w_ref[:, off:off + LANE]) * (IDX_HEADS ** -0.5 * IDX_DIM ** -0.5)
    wit_ref[0] = w_idx.T[:IDX_HEADS, :]
    off += LANE
    qmem_ref[0] = _spread_heads(_dot(y, w_ref[:, off:off + N_MEM_HEADS * HEAD_DIM]) * HEAD_DIM ** -0.5
                                ).astype(BF16)
    value_row = lax.broadcasted_iota(I32, (LANE, tm), 0) >= IDX_DIM
    for pair in range(IDX_HEADS // 2):
        acc_t = _dot(c_q, wqi_ref[:, pair * LANE:(pair + 1) * LANE]).T
        qit_ref[0, 2 * pair] = jnp.where(value_row, 0.0, acc_t).astype(BF16)
        qit_ref[0, 2 * pair + 1] = jnp.where(value_row, 0.0, pltpu.roll(acc_t, IDX_DIM, axis=0)).astype(BF16)
    for pair in range(N_MIX_HEADS // 2):
        q_pair = (_dot(c_q, wqu_ref[:, pair * LANE:(pair + 1) * LANE]) * HEAD_DIM ** -0.5).astype(BF16)
        for h in (2 * pair, 2 * pair + 1):
            qat_ref[0, h] = (_dot(q_pair, wuk_ref[h]) * LOG2E).T.astype(BF16)


def _dsa_proj(h, gain, w, qn, kvn, kin, wqu, wuk, wqi, tm):
    bsz, t, d = h.shape
    nkt = t // TK
    full = lambda a: pl.BlockSpec(a.shape, lambda b, i: (0,) * a.ndim)
    row = lambda n: pl.BlockSpec((1, tm, n), lambda b, i: (b, i, 0))
    return pl.pallas_call(
        _dsa_proj_kernel,
        grid=(bsz, t // tm),
        in_specs=[pl.BlockSpec((1, tm, d), lambda b, i: (b, i, 0)), full(gain), full(w), full(qn),
                  full(kvn), full(kin), full(wqu), full(wuk), full(wqi)],
        out_specs=[pl.BlockSpec((1, N_MIX_HEADS, LANE, tm), lambda b, i: (b, 0, 0, i)),
                   pl.BlockSpec((1, IDX_HEADS, LANE, tm), lambda b, i: (b, 0, 0, i)),
                   pl.BlockSpec((1, IDX_HEADS, tm), lambda b, i: (b, 0, i)),
                   row(LANE),
                   pl.BlockSpec((1, tm // TK, DSA_VT_ROWS, TK), lambda b, i: (b, i, 0, 0)),
                   row(LANE), row(N_MEM_HEADS * LANE)],
        out_shape=[jax.ShapeDtypeStruct((bsz, N_MIX_HEADS, LANE, t), BF16),
                   jax.ShapeDtypeStruct((bsz, IDX_HEADS, LANE, t), BF16),
                   jax.ShapeDtypeStruct((bsz, IDX_HEADS, t), F32),
                   jax.ShapeDtypeStruct((bsz, t, LANE), BF16),
                   jax.ShapeDtypeStruct((bsz, nkt, DSA_VT_ROWS, TK), BF16),
                   jax.ShapeDtypeStruct((bsz, t, LANE), BF16),
                   jax.ShapeDtypeStruct((bsz, t, N_MEM_HEADS * LANE), BF16)],
        compiler_params=pltpu.CompilerParams(dimension_semantics=("arbitrary", "arbitrary"),
                                             vmem_limit_bytes=VMEM_LIMIT),
        name="dsa_proj",
    )(h, gain, w, qn, kvn, kin, wqu, wuk, wqi)


def _dsa_kernel(qi_ref, wi_ref, qa_ref, kidx_ref, ckv_ref, ckvt_ref, tab_ref, o_ref,
                sc_ref, negm_ref, s_ref, tmax_ref, m_ref, l_ref, acc_ref, *, topk):
    i = pl.program_id(1)
    qs = i * TQ
    n_tiles = i + 1
    sub = TK // KS
    t_row = qs + lax.broadcasted_iota(I32, (1, TQ), 1)
    krow = lax.broadcasted_iota(I32, (KS, TQ), 0)
    k_f = float(topk)

    grouped = lambda x: x.reshape(KS // SUBLANE, SUBLANE, TQ)

    def score_body(kt, carry):
        n_pos, p_min, p_max = carry
        for a in range(sub):
            k0 = pl.multiple_of(kt * TK + a * KS, KS)
            kk = kidx_ref[0, pl.ds(k0, KS), :]
            sc = jnp.maximum(_dot(kk, qi_ref[0, 0]), 0.0) * wi_ref[0, 0:1, :]
            for h in range(1, IDX_HEADS):
                sc = sc + jnp.maximum(_dot(kk, qi_ref[0, h]), 0.0) * wi_ref[0, h:h + 1, :]
            sc = jnp.where((k0 + krow) <= t_row, sc, NEG)
            sc_ref[sub * kt + a] = sc
            pos = sc > 0.0
            n_pos = n_pos + jnp.sum(grouped(pos.astype(F32)), axis=0)
            p_min = jnp.minimum(p_min, jnp.min(grouped(jnp.where(pos, sc, jnp.inf)), axis=0))
            p_max = jnp.maximum(p_max, jnp.max(grouped(sc), axis=0))
        return n_pos, p_min, p_max

    n_pos, p_min, p_max = lax.fori_loop(
        0, n_tiles, score_body,
        (jnp.zeros((SUBLANE, TQ), F32), jnp.full((SUBLANE, TQ), jnp.inf, F32),
         jnp.full((SUBLANE, TQ), NEG, F32)))
    n_pos = jnp.sum(n_pos, axis=0, keepdims=True)
    p_min = jnp.min(p_min, axis=0, keepdims=True)
    p_max = jnp.max(p_max, axis=0, keepdims=True)

    def count(pred):
        def body(kt, acc):
            for a in range(sub):
                hit = pred(sc_ref[sub * kt + a], kt * TK + a * KS).astype(F32)
                acc = acc + jnp.sum(grouped(hit), axis=0)
            return acc
        acc = lax.fori_loop(0, n_tiles, body, jnp.zeros((SUBLANE, TQ), F32))
        return jnp.sum(acc, axis=0, keepdims=True)

    short = t_row < topk
    bounded = short | (n_pos >= k_f)
    key_lo = lax.bitcast_convert_type(jnp.min(jnp.where(short, jnp.inf, p_min)), I32)
    key_hi = lax.bitcast_convert_type(jnp.max(jnp.where(short, 0.0, jnp.maximum(p_max, 0.0))), I32)
    differ = (key_lo ^ key_hi).astype(F32)
    top_bit = jnp.clip(jnp.right_shift(lax.bitcast_convert_type(differ, I32), 23) - 127, 0, 30)
    use_bounds = (jnp.sum((~bounded).astype(I32)) == 0) & (key_lo <= key_hi)
    prefix = jnp.where(use_bounds, key_hi & ~(jnp.left_shift(jnp.int32(2), top_bit) - 1), INT_MIN)
    first_pass = jnp.where(use_bounds, 31 - top_bit, 0)
    passes_per_check = 4

    def v_cond(c):
        return (c[0] < 32) & (c[3] > 0)

    def v_body(c):
        it, v, cnt_v, _ = c
        for _ in range(passes_per_check):
            live = it < 32
            cand = v + jnp.left_shift(jnp.int32(1), jnp.maximum(31 - it, 0))
            cand_f = _key_to_float(cand)
            cnt = count(lambda sc, k0: sc >= cand_f)
            take = (cnt >= k_f) & live
            v = jnp.where(take, cand, v)
            cnt_v = jnp.where(take, cnt, cnt_v)
            it = it + 1
        open_cols = jnp.sum(((cnt_v != k_f) & ~short).astype(I32))
        return it, v, cnt_v, open_cols

    _, v, cnt_v, open_cols = lax.while_loop(
        v_cond, v_body,
        (first_pass, jnp.full((1, TQ), prefix, I32), jnp.full((1, TQ), 1e9, F32), jnp.int32(1)))
    thr = _key_to_float(v)

    def tie_cut(_):
        need = k_f - count(lambda sc, k0: sc > thr)

        def c_body(it, c):
            cand = c + jnp.left_shift(jnp.int32(1), 13 - it)
            cnt = count(lambda sc, k0: (sc == thr) & ((k0 + krow) < cand))
            return jnp.where(cnt < need, cand, c)

        return lax.fori_loop(0, 14, c_body, jnp.zeros((1, TQ), I32))

    cut = lax.cond(open_cols > 0, tie_cut, lambda _: jnp.full((1, TQ), 2 ** 30, I32), 0)

    def att_mask(kt):
        for a in range(sub):
            sc = sc_ref[sub * kt + a]
            kpos = kt * TK + a * KS + krow
            chosen = short | (sc > thr) | ((sc == thr) & (kpos <= cut))
            negm_ref[a * KS:(a + 1) * KS, :] = jnp.where(chosen & (kpos <= t_row), 0.0, -jnp.inf)

    _attend_tiles(N_MIX_HEADS, n_tiles, lambda h: qa_ref[0, h],
                  lambda kt: ckv_ref[0, pl.ds(pl.multiple_of(kt * TK, TK), TK), :],
                  lambda kt: ckvt_ref[0, kt],
                  lambda kt: (lambda h: _bias_block(lambda m: tab_ref[m, h], (TQ // KS) * (i - kt))),
                  att_mask, negm_ref, s_ref, tmax_ref, m_ref, acc_ref, l_ref)
    for h in range(N_MIX_HEADS):
        out = acc_ref[h] * (1.0 / l_ref[h])
        o_ref[0, :, h * LANE:(h + 1) * LANE] = out.T.astype(o_ref.dtype)


def _dsa_attention(qit, wit, qat, kidx, ckv, ckvt, tab, topk):
    bsz, _, _, t = qat.shape
    nkt = t // TK
    return pl.pallas_call(
        functools.partial(_dsa_kernel, topk=topk),
        grid=(bsz, t // TQ),
        in_specs=[pl.BlockSpec((1, IDX_HEADS, LANE, TQ), lambda b, i: (b, 0, 0, i)),
                  pl.BlockSpec((1, IDX_HEADS, TQ), lambda b, i: (b, 0, i)),
                  pl.BlockSpec((1, N_MIX_HEADS, LANE, TQ), lambda b, i: (b, 0, 0, i)),
                  pl.BlockSpec((1, t, LANE), lambda b, i: (b, 0, 0)),
                  pl.BlockSpec((1, t, LANE), lambda b, i: (b, 0, 0)),
                  pl.BlockSpec((1, nkt, DSA_VT_ROWS, TK), lambda b, i: (b, 0, 0, 0)),
                  pl.BlockSpec(tab.shape, lambda b, i: (0, 0, 0, 0), pipeline_mode=pl.Buffered(1))],
        out_specs=pl.BlockSpec((1, TQ, N_MIX_HEADS * LANE), lambda b, i: (b, i, 0)),
        out_shape=jax.ShapeDtypeStruct((bsz, t, N_MIX_HEADS * LANE), BF16),
        scratch_shapes=[pltpu.VMEM((t // KS, KS, TQ), F32), pltpu.VMEM((TK, TQ), F32),
                        pltpu.VMEM((N_MIX_HEADS, TK, TQ), F32), pltpu.VMEM((N_MIX_HEADS, 1, TQ), F32),
                        pltpu.VMEM((N_MIX_HEADS, 1, TQ), F32), pltpu.VMEM((N_MIX_HEADS, 1, TQ), F32),
                        pltpu.VMEM((N_MIX_HEADS, DSA_VT_ROWS, TQ), F32)],
        compiler_params=pltpu.CompilerParams(dimension_semantics=("arbitrary", "arbitrary"),
                                             vmem_limit_bytes=VMEM_LIMIT),
        name="dsa_attention",
    )(qit, wit, qat, kidx, ckv, ckvt, tab)


def _post_kernel(h_ref, mix_ref, qmem_ref, kvm_ref, *rest, has_uv):
    if has_uv:
        wuv_ref, wmix_ref, wmem_ref, o_ref = rest
    else:
        wmix_ref, wmem_ref, o_ref = rest
    mix = mix_ref[0]
    if has_uv:
        mix = _dot(mix, wuv_ref[...]).astype(BF16)
    upd = _dot(mix, wmix_ref[...])
    qm = qmem_ref[0]
    for hm in range(N_MEM_HEADS):
        sl = slice(hm * LANE, (hm + 1) * LANE)
        kv = kvm_ref[0, :, sl]
        s = _dot_nt(qm[:, sl], kv)
        e = jnp.exp(s - jnp.max(s, axis=-1, keepdims=True))
        p = e / jnp.sum(e, axis=-1, keepdims=True)
        o_h = _dot(p.astype(BF16), kv).astype(BF16)
        upd = upd + _dot(o_h, wmem_ref[sl, :])
    o_ref[0] = h_ref[0] + upd


def _post(h, mix, qmem, kvm, w_uv, w_mix, w_mem, tm):
    bsz, t, d = h.shape
    has_uv = w_uv is not None
    full = lambda a: pl.BlockSpec(a.shape, lambda b, i: (0,) * a.ndim)
    weights = ([w_uv] if has_uv else []) + [w_mix, w_mem]
    return pl.pallas_call(
        functools.partial(_post_kernel, has_uv=has_uv),
        grid=(bsz, t // tm),
        in_specs=[pl.BlockSpec((1, tm, d), lambda b, i: (b, i, 0)),
                  pl.BlockSpec((1, tm, mix.shape[2]), lambda b, i: (b, i, 0)),
                  pl.BlockSpec((1, tm, qmem.shape[2]), lambda b, i: (b, i, 0)),
                  pl.BlockSpec((1,) + kvm.shape[1:], lambda b, i: (b, 0, 0))]
                 + [full(w) for w in weights],
        out_specs=pl.BlockSpec((1, tm, d), lambda b, i: (b, i, 0)),
        out_shape=jax.ShapeDtypeStruct((bsz, t, d), F32),
        compiler_params=pltpu.CompilerParams(dimension_semantics=("arbitrary", "arbitrary"),
                                             vmem_limit_bytes=VMEM_LIMIT),
        name="mem_attn_out_proj",
    )(h, mix, qmem, kvm, *weights)


def _ffn_kernel(h_ref, g_ref, wg_ref, wu_ref, wd_ref, gf_ref, o_ref, *, final_norm):
    h = h_ref[...]
    hn = _rms(h, g_ref[...]).astype(BF16)
    act = (jax.nn.silu(_dot(hn, wg_ref[...])) * _dot(hn, wu_ref[...])).astype(BF16)
    out = h + _dot(act, wd_ref[...])
    if final_norm:
        out = _rms(out, gf_ref[...])
    o_ref[...] = out


def _ffn(h2d, gain, wg, wu, wd, gain_final, final_norm, tm):
    m, d = h2d.shape
    const = lambda a: pl.BlockSpec(a.shape, lambda i: (0,) * a.ndim, pipeline_mode=pl.Buffered(1))
    return pl.pallas_call(
        functools.partial(_ffn_kernel, final_norm=final_norm),
        grid=(m // tm,),
        in_specs=[pl.BlockSpec((tm, d), lambda i: (i, 0)), const(gain), const(wg), const(wu),
                  const(wd), const(gain_final)],
        out_specs=pl.BlockSpec((tm, d), lambda i: (i, 0)),
        out_shape=jax.ShapeDtypeStruct((m, d), F32),
        compiler_params=pltpu.CompilerParams(dimension_semantics=("arbitrary",),
                                             vmem_limit_bytes=VMEM_LIMIT),
        name="swiglu_ffn",
    )(h2d, gain, wg, wu, wd, gain_final)


def _pad_cols(w, n):
    return jnp.pad(w, ((0, 0), (0, n - w.shape[1])))


def _value_rows(w_rows, n_heads):
    d_out = w_rows.shape[1]
    w = w_rows.reshape(n_heads, HEAD_DIM, d_out)
    return jnp.pad(w, ((0, 0), (LANE - HEAD_DIM, 0), (0, 0))).reshape(n_heads * LANE, d_out)


def _interleave_kv(k, v, n_heads):
    d_in = k.shape[0]
    kv = jnp.concatenate([k.reshape(d_in, n_heads, HEAD_DIM), v.reshape(d_in, n_heads, HEAD_DIM)], axis=2)
    return kv.reshape(d_in, n_heads * LANE)


def _bucket_bias(rel_bias, bucket_np):
    onehot = jax.nn.one_hot(jnp.asarray(bucket_np.reshape(-1)), N_REL_BUCKETS, dtype=F32)
    out = jnp.dot(onehot, rel_bias, precision=lax.Precision.HIGHEST) * LOG2E
    return out.reshape(bucket_np.shape + (rel_bias.shape[1],))


def _bias_tiles(rel_bias):
    m = np.arange(N_BIAS_TILES)[:, None, None]
    dist = LANE * m + np.arange(LANE)[None, None, :] - np.arange(KS)[None, :, None]
    return jnp.transpose(_bucket_bias(rel_bias, _rel_bucket_np(dist)), (0, 3, 1, 2))


def _bias_cmp_table(rel_bias):
    rel = np.arange(2 * CMP_PAD) - CMP_PAD
    dist = np.arange(TQ)[None, :] - CMP_STRIDE * rel[:, None] - (CMP_LEN - 1)
    return jnp.transpose(_bucket_bias(rel_bias, _rel_bucket_np(dist)), (2, 0, 1))


def _overlap_matrix(t):
    n_cmp = (t - CMP_LEN) // CMP_STRIDE + 1
    n_sel = t // SEL_BLOCK
    cs = np.arange(CMP_PAD) * CMP_STRIDE
    ss = np.arange(LANE) * SEL_BLOCK
    ov = (cs[None, :] <= ss[:, None] + SEL_BLOCK - 1) & (cs[None, :] + CMP_LEN - 1 >= ss[:, None])
    ov &= (np.arange(CMP_PAD) < n_cmp)[None, :] & (np.arange(LANE) < n_sel)[:, None]
    return jnp.asarray(ov, BF16)


def kernel(x, mem, rel_bias, norm_mix, norm_ffn, norm_mem, w_mem_kv, w_out, ffn_gate, ffn_up, ffn_down,
           nsa_w_in, nsa_gate_b, nsa_cmp_pos_k, nsa_cmp_pos_v,
           nsa_cmp_k_w1, nsa_cmp_k_b1, nsa_cmp_k_w2, nsa_cmp_k_b2,
           nsa_cmp_v_w1, nsa_cmp_v_b1, nsa_cmp_v_w2, nsa_cmp_v_b2,
           dsa_w_in, dsa_q_norm, dsa_kv_norm, dsa_w_q_up, dsa_w_uk, dsa_w_uv, dsa_w_q_idx, dsa_kidx_norm,
           norm_final):
    bsz, t, d = x.shape
    m_len = mem.shape[1]
    depth = norm_mix.shape[0]
    g, r, hh = NSA_KV_HEADS, NSA_GROUP, N_MIX_HEADS
    d_mix = hh * HEAD_DIM
    kvw_ = g * HEAD_DIM
    assert t % 1024 == 0 and t // SEL_BLOCK <= LANE and t // CMP_STRIDE <= CMP_PAD
    tm = 512
    h = x.astype(F32)

    tab = _bias_tiles(rel_bias)
    tab_nsa = jnp.transpose(tab.reshape(N_BIAS_TILES, g, r, KS, LANE), (1, 0, 2, 3, 4))
    tabc = _bias_cmp_table(rel_bias).reshape(g, r, 2 * CMP_PAD, TQ)
    ovl = _overlap_matrix(t)

    for layer in range(depth):
        j = layer // 2
        wkv = w_mem_kv[layer]
        wkv = _interleave_kv(wkv[:, :N_MEM_HEADS * HEAD_DIM], wkv[:, N_MEM_HEADS * HEAD_DIM:], N_MEM_HEADS)
        kvm = _rms_proj(mem.reshape(bsz * m_len, d), norm_mem[layer], wkv.astype(BF16), BF16, m_len)
        kvm = kvm.reshape(bsz, m_len, N_MEM_HEADS * LANE)
        w_o = w_out[layer]
        w_mem_o = _value_rows(w_o[d_mix:], N_MEM_HEADS).astype(BF16)
        w_mix = w_o[:d_mix].astype(BF16)

        if layer % 2 == 0:
            w = nsa_w_in[j]
            c = np.cumsum([0, d_mix, kvw_, kvw_, kvw_, kvw_, kvw_, kvw_, hh * N_BRANCH,
                           N_MEM_HEADS * HEAD_DIM])
            wq, wkc, wvc, wks, wvs, wkw, wvw, wgl, wqm = [w[:, c[k]:c[k + 1]] for k in range(9)]
            n_g = r * N_BRANCH
            wgl = jnp.concatenate([_pad_cols(wgl[:, gg * n_g:(gg + 1) * n_g], LANE) for gg in range(g)], 1)
            w_all = jnp.concatenate([wq, wkc, wvc, _interleave_kv(wks, wvs, g),
                                     _interleave_kv(wkw, wvw, g), wgl, wqm], axis=1)
            qt, cmp_raw, kvs, vst, kvw, vwt, glt, qmem = _nsa_proj(h, norm_mix[layer], w_all.astype(BF16), tm)
            nc = t // CMP_STRIDE
            x2 = jnp.transpose(cmp_raw.reshape(bsz, t, 2 * g, HEAD_DIM), (0, 2, 1, 3))
            x2 = x2.reshape(bsz, 2 * g, nc, CMP_STRIDE * HEAD_DIM)
            pos = jnp.stack([nsa_cmp_pos_k[j], nsa_cmp_pos_v[j]]).reshape(2, 2, CMP_STRIDE * HEAD_DIM)
            w1 = jnp.stack([nsa_cmp_k_w1[j], nsa_cmp_v_w1[j]]).astype(BF16)
            b1 = jnp.stack([nsa_cmp_k_b1[j], nsa_cmp_v_b1[j]])[:, None, :]
            w2 = jnp.stack([jnp.pad(nsa_cmp_k_w2[j], ((0, 0), (0, HEAD_DIM))),
                            jnp.pad(nsa_cmp_v_w2[j], ((0, 0), (HEAD_DIM, 0)))]).astype(BF16)
            b2 = jnp.stack([jnp.pad(nsa_cmp_k_b2[j], (0, HEAD_DIM)),
                            jnp.pad(nsa_cmp_v_b2[j], (HEAD_DIM, 0))])[:, None, :]
            kvc = _compress(x2, pos, w1, b1, w2, b2)
            kvc = jnp.pad(kvc, ((0, 0), (0, 0), (0, CMP_PAD - nc), (0, 0)))
            vct = jnp.transpose(kvc, (0, 1, 3, 2))
            gb = jnp.pad(nsa_gate_b[j].reshape(g, n_g), ((0, 0), (0, GATE_ROWS - n_g)))
            gbt = jnp.broadcast_to(gb[:, :, None], (g, GATE_ROWS, TQ))
            mix = _nsa_attention(qt, kvc, vct, kvs, vst, kvw, vwt, glt, gbt, tab_nsa, tabc, ovl,
                                 t // SEL_BLOCK)
            w_uv = None
        else:
            w = dsa_w_in[j]
            c = np.cumsum([0, Q_LORA, KV_LORA, IDX_DIM, IDX_HEADS, N_MEM_HEADS * HEAD_DIM])
            wcq, wckv, wki, wwi, wqm = [w[:, c[k]:c[k + 1]] for k in range(5)]
            w_all = jnp.concatenate([wcq, wckv, _pad_cols(wki, LANE), _pad_cols(wwi, LANE),
                                     wqm], axis=1).astype(BF16)
            wuk = jnp.transpose(dsa_w_uk[j], (1, 2, 0))
            wuk = jnp.stack([jnp.pad(wuk[hd], ((HEAD_DIM * (hd % 2), HEAD_DIM * (1 - hd % 2)), (0, 0)))
                             for hd in range(hh)]).astype(BF16)
            qat, qit, wit, ckv, ckvt, kidx, qmem = _dsa_proj(
                h, norm_mix[layer][None], w_all, dsa_q_norm[j][None], dsa_kv_norm[j][None],
                _pad_cols(dsa_kidx_norm[j][None], LANE), dsa_w_q_up[j].astype(BF16), wuk,
                dsa_w_q_idx[j].astype(BF16), tm)
            mix = _dsa_attention(qit, wit, qat, kidx, ckv, ckvt, tab, min(DSA_TOPK, t // 4))
            wv = jnp.transpose(dsa_w_uv[j], (1, 0, 2))
            w_uv = (jnp.eye(hh, dtype=F32)[:, None, :, None] * wv[:, :, None, :]
                    ).reshape(hh * KV_LORA, d_mix).astype(BF16)

        h = _post(h, mix, qmem, kvm, w_uv, w_mix, w_mem_o, tm)
        last = layer == depth - 1
        h = _ffn(h.reshape(bsz * t, d), norm_ffn[layer][None], ffn_gate[layer].astype(BF16),
                 ffn_up[layer].astype(BF16), ffn_down[layer].astype(BF16), norm_final[None],
                 last, tm).reshape(bsz, t, d)
    return h.astype(x.dtype)
```

```python
import functools
import math

import numpy as np
import jax
import jax.numpy as jnp
from jax import lax
from jax.experimental import pallas as pl
from jax.experimental.pallas import tpu as pltpu

F32 = jnp.float32
BF16 = jnp.bfloat16
I32 = jnp.int32

NEG = -1e30
EPS = 1e-6
LOG2E = math.log2(math.e)
LANE = 128
SUBLANE = 8
BF16_ROWS = 16
HEAD_DIM = 64
N_MIX_HEADS = 12
N_MEM_HEADS = 4
N_REL_BUCKETS = 32
REL_MAX_EXACT = 16
REL_MAX_DIST = 2048
NSA_KV_HEADS = 2
NSA_GROUP = N_MIX_HEADS // NSA_KV_HEADS
N_BRANCH = 3
GATE_ROWS = 24
CMP_LEN = 32
CMP_STRIDE = 16
SEL_BLOCK = 64
N_SEL = 16
WINDOW = 512
FORCE_BONUS = 1e4
Q_LORA = 256
KV_LORA = 128
IDX_HEADS = 8
IDX_DIM = 64
DSA_TOPK = 256
TQ = 256
TK = 256
KS = LANE
N_BIAS_TILES = REL_MAX_DIST // KS + 2
CMP_PAD = 512
NSA_VT_ROWS = HEAD_DIM
DSA_VT_ROWS = KV_LORA
VMEM_LIMIT = 56 * 1024 * 1024
INT_MIN = -2 ** 31
F32_MIN = float(np.finfo(np.float32).min)


def _dot(a, b):
    return jnp.dot(a, b, preferred_element_type=F32)


def _dot_nt(a, b):
    return lax.dot_general(a, b, (((1,), (1,)), ((), ())), preferred_element_type=F32)


def _rms(x, gain, n=None):
    n = x.shape[-1] if n is None else n
    ms = jnp.sum(x * x, axis=-1, keepdims=True) * (1.0 / n)
    return x * lax.rsqrt(ms + EPS) * gain


def _rel_bucket_np(dist):
    n = np.maximum(dist, 0)
    nf = np.maximum(n, REL_MAX_EXACT).astype(np.float32)
    large = REL_MAX_EXACT + (np.log(nf / np.float32(REL_MAX_EXACT))
                             / np.float32(math.log(REL_MAX_DIST / REL_MAX_EXACT))
                             * np.float32(N_REL_BUCKETS - REL_MAX_EXACT)).astype(np.int32)
    large = np.minimum(large, N_REL_BUCKETS - 1)
    return np.where(n < REL_MAX_EXACT, n, large).astype(np.int32)


def _spread_heads(x):
    rows, n = x.shape
    low = lax.broadcasted_iota(I32, (rows, LANE), 1) < HEAD_DIM
    out = []
    for j in range(n // LANE):
        pair = x[:, j * LANE:(j + 1) * LANE]
        out += [jnp.where(low, pair, 0.0), jnp.where(low, pltpu.roll(pair, HEAD_DIM, axis=1), 0.0)]
    return jnp.concatenate(out, axis=1)


def _colsum(x):
    rows, n = x.shape
    return jnp.sum(jnp.sum(x.reshape(rows // SUBLANE, SUBLANE, n), axis=0), axis=0, keepdims=True)


def _key_to_float(v):
    bits = jnp.where(v >= 0, v, v ^ jnp.int32(0x7FFFFFFF))
    return pltpu.bitcast(bits, F32)


def _topk_cols(score, k, row_idx):
    n = score.shape[1]

    def vbody(it, v):
        cand = v + jnp.left_shift(jnp.int32(1), 31 - it)
        cnt = _colsum((score >= _key_to_float(cand)).astype(F32))
        return jnp.where(cnt >= k, cand, v)

    v = lax.fori_loop(0, 32, vbody, jnp.full((1, n), INT_MIN, I32))
    thr = _key_to_float(v)
    gt = score > thr
    eq = score == thr
    need = k - _colsum(gt.astype(F32))

    def cbody(it, c):
        cand = c + jnp.left_shift(jnp.int32(1), 6 - it)
        cnt = _colsum((eq & (row_idx < cand)).astype(F32))
        return jnp.where(cnt < need, cand, c)

    c = lax.fori_loop(0, 7, cbody, jnp.zeros((1, n), I32))
    return gt | (eq & (row_idx <= c))


def _bias_block(tile_of, m0):
    idx = lambda m: jnp.clip(m, 0, N_BIAS_TILES - 1)
    return [jnp.concatenate([tile_of(idx(m0 + b - a)) for b in range(TQ // LANE)], axis=1)
            for a in range(TK // KS)]


def _attend_tiles(n_heads, n_tiles, q_of, kv_of, vt_of, bias_of, fill_mask, negm_ref, s_ref, tmax_ref,
                  m_ref, l_ref, acc_ref):
    parts = [slice(a * KS, (a + 1) * KS) for a in range(TK // KS)]
    m_ref[...] = jnp.full(m_ref.shape, F32_MIN, F32)
    l_ref[...] = jnp.zeros(l_ref.shape, F32)
    acc_ref[...] = jnp.zeros(acc_ref.shape, F32)

    def logits_of(j):
        fill_mask(j)
        kv = kv_of(j)
        bias_h = bias_of(j)

        def run(h):
            bias = bias_h(h)
            tmax = None
            for a, sl in enumerate(parts):
                s_a = _dot(kv[sl], q_of(h)) + bias[a] + negm_ref[sl, :]
                s_ref[h, sl, :] = s_a
                mx = jnp.max(s_a, axis=0, keepdims=True)
                tmax = mx if tmax is None else jnp.maximum(tmax, mx)
            tmax_ref[h] = tmax
        return run

    def accumulate(j):
        vt = vt_of(j)

        def run(h):
            m_old = m_ref[h]
            m_new = jnp.maximum(m_old, tmax_ref[h])
            alpha = jnp.exp2(m_old - m_new)
            m_ref[h] = m_new
            p = [jnp.exp2(s_ref[h, sl, :] - m_new) for sl in parts]
            l_ref[h] = alpha * l_ref[h] + _colsum(sum(p[1:], p[0]))
            acc = alpha * acc_ref[h]
            for sl, p_a in zip(parts, p):
                acc = acc + _dot(vt[:, sl], p_a.astype(BF16))
            acc_ref[h] = acc
        return run

    first = logits_of(0)
    for h in range(n_heads):
        first(h)

    def body(j, carry):
        consume = accumulate(j)
        produce = logits_of(jnp.minimum(j + 1, n_tiles - 1))
        for h in range(n_heads):
            consume(h)
            produce(h)
        return carry

    lax.fori_loop(0, n_tiles, body, 0)


def _rms_proj_kernel(x_ref, g_ref, w_ref, o_ref):
    y = _rms(x_ref[...], g_ref[...]).astype(BF16)
    o_ref[...] = _dot(y, w_ref[...]).astype(o_ref.dtype)


def _rms_proj(x2d, gain, w_bf16, out_dtype, tm):
    m, d = x2d.shape
    n = w_bf16.shape[1]
    return pl.pallas_call(
        _rms_proj_kernel,
        grid=(m // tm,),
        in_specs=[pl.BlockSpec((tm, d), lambda i: (i, 0)),
                  pl.BlockSpec((1, d), lambda i: (0, 0)),
                  pl.BlockSpec((d, n), lambda i: (0, 0))],
        out_specs=pl.BlockSpec((tm, n), lambda i: (i, 0)),
        out_shape=jax.ShapeDtypeStruct((m, n), out_dtype),
        compiler_params=pltpu.CompilerParams(dimension_semantics=("arbitrary",),
                                             vmem_limit_bytes=VMEM_LIMIT),
        name="rms_proj",
    )(x2d, gain.reshape(1, d), w_bf16)


def _nsa_proj_kernel(x_ref, g_ref, w_ref, qt_ref, cmp_ref, kvs_ref, vst_ref, kvw_ref, vwt_ref,
                     glt_ref, qmem_ref):
    tm = x_ref.shape[1]
    y = _rms(x_ref[0], g_ref[...]).astype(BF16)
    value_row = lax.broadcasted_iota(I32, (LANE, tm), 0) >= HEAD_DIM
    off = 0
    for pair in range(N_MIX_HEADS // 2):
        acc_t = (_dot(y, w_ref[:, off:off + LANE]) * (HEAD_DIM ** -0.5 * LOG2E)).T
        qt_ref[0, 2 * pair] = jnp.where(value_row, 0.0, acc_t).astype(BF16)
        qt_ref[0, 2 * pair + 1] = jnp.where(value_row, 0.0, pltpu.roll(acc_t, HEAD_DIM, axis=0)).astype(BF16)
        off += LANE
    cmp_ref[0] = _dot(y, w_ref[:, off:off + 2 * LANE])
    off += 2 * LANE
    for kv_ref, vt_ref in ((kvs_ref, vst_ref), (kvw_ref, vwt_ref)):
        for g in range(NSA_KV_HEADS):
            acc = _dot(y, w_ref[:, off:off + LANE])
            off += LANE
            kv_ref[0, :, g * LANE:(g + 1) * LANE] = acc.astype(BF16)
            v_t = acc.T[HEAD_DIM:, :].astype(BF16)
            for jt in range(tm // TK):
                vt_ref[0, g, jt] = v_t[:, jt * TK:(jt + 1) * TK]
    for g in range(NSA_KV_HEADS):
        acc = _dot(y, w_ref[:, off:off + LANE])
        off += LANE
        glt_ref[0, g] = acc.T[:GATE_ROWS, :]
    qmem_ref[0] = _spread_heads(_dot(y, w_ref[:, off:off + N_MEM_HEADS * HEAD_DIM]) * HEAD_DIM ** -0.5
                                ).astype(BF16)


def _nsa_proj(h, gain, w, tm):
    bsz, t, d = h.shape
    g = NSA_KV_HEADS
    nkt = t // TK
    row = lambda n: pl.BlockSpec((1, tm, n), lambda b, i: (b, i, 0))
    vt_spec = pl.BlockSpec((1, g, tm // TK, NSA_VT_ROWS, TK), lambda b, i: (b, 0, i, 0, 0))
    return pl.pallas_call(
        _nsa_proj_kernel,
        grid=(bsz, t // tm),
        in_specs=[pl.BlockSpec((1, tm, d), lambda b, i: (b, i, 0)),
                  pl.BlockSpec((1, d), lambda b, i: (0, 0)),
                  pl.BlockSpec(w.shape, lambda b, i: (0, 0))],
        out_specs=[pl.BlockSpec((1, N_MIX_HEADS, LANE, tm), lambda b, i: (b, 0, 0, i)),
                   row(2 * LANE), row(g * LANE), vt_spec, row(g * LANE), vt_spec,
                   pl.BlockSpec((1, g, GATE_ROWS, tm), lambda b, i: (b, 0, 0, i)),
                   row(N_MEM_HEADS * LANE)],
        out_shape=[jax.ShapeDtypeStruct((bsz, N_MIX_HEADS, LANE, t), BF16),
                   jax.ShapeDtypeStruct((bsz, t, 2 * LANE), F32),
                   jax.ShapeDtypeStruct((bsz, t, g * LANE), BF16),
                   jax.ShapeDtypeStruct((bsz, g, nkt, NSA_VT_ROWS, TK), BF16),
                   jax.ShapeDtypeStruct((bsz, t, g * LANE), BF16),
                   jax.ShapeDtypeStruct((bsz, g, nkt, NSA_VT_ROWS, TK), BF16),
                   jax.ShapeDtypeStruct((bsz, g, GATE_ROWS, t), F32),
                   jax.ShapeDtypeStruct((bsz, t, N_MEM_HEADS * LANE), BF16)],
        compiler_params=pltpu.CompilerParams(dimension_semantics=("arbitrary", "arbitrary"),
                                             vmem_limit_bytes=VMEM_LIMIT),
        name="nsa_proj",
    )(h, gain.reshape(1, d), w)


def _compress_kernel(xk_ref, xv_ref, pos_ref, w1_ref, b1_ref, w2_ref, b2_ref, o_ref):
    nc = xk_ref.shape[2]
    half = xk_ref.shape[3]
    out = None
    for j, x_ref in enumerate((xk_ref, xv_ref)):
        x = x_ref[0, 0]
        top = _dot((x + pos_ref[j, 0:1, :]).astype(BF16), w1_ref[j, :half, :])
        bot = _dot((x + pos_ref[j, 1:2, :]).astype(BF16), w1_ref[j, half:, :])
        pre = top + pltpu.roll(bot, nc - 1, axis=0) + b1_ref[j]
        hid = jax.nn.gelu(pre)
        res = _dot(hid.astype(BF16), w2_ref[j]) + b2_ref[j]
        out = res if out is None else out + res
    o_ref[0, 0] = out.astype(o_ref.dtype)


def _compress(x2, pos, w1, b1, w2, b2):
    bsz, _, nc, width = x2.shape
    g = NSA_KV_HEADS
    return pl.pallas_call(
        _compress_kernel,
        grid=(bsz, g),
        in_specs=[pl.BlockSpec((1, 1, nc, width), lambda b, gg: (b, gg, 0, 0)),
                  pl.BlockSpec((1, 1, nc, width), lambda b, gg: (b, gg + NSA_KV_HEADS, 0, 0)),
                  pl.BlockSpec(pos.shape, lambda b, gg: (0, 0, 0)),
                  pl.BlockSpec(w1.shape, lambda b, gg: (0, 0, 0)),
                  pl.BlockSpec(b1.shape, lambda b, gg: (0, 0, 0)),
                  pl.BlockSpec(w2.shape, lambda b, gg: (0, 0, 0)),
                  pl.BlockSpec(b2.shape, lambda b, gg: (0, 0, 0))],
        out_specs=pl.BlockSpec((1, 1, nc, LANE), lambda b, gg: (b, gg, 0, 0)),
        out_shape=jax.ShapeDtypeStruct((bsz, g, nc, LANE), BF16),
        compiler_params=pltpu.CompilerParams(dimension_semantics=("arbitrary", "arbitrary"),
                                             vmem_limit_bytes=VMEM_LIMIT),
        name="nsa_compress",
    )(x2, x2, pos, w1, b1, w2, b2)


def _nsa_kernel(q_ref, kvc_ref, vct_ref, kvs_ref, vst_ref, kvw_ref, vwt_ref, glt_ref, gbt_ref,
                tab_ref, tabc_ref, ovl_ref, o_ref,
                psum_ref, negsel_ref, negm_ref, s_ref, tmax_ref, m_ref, l_ref, acc_ref, ocmp_ref, oslc_ref,
                *, n_sel):
    r_heads = NSA_GROUP
    i = pl.program_id(2)
    qs = i * TQ
    t_row = qs + lax.broadcasted_iota(I32, (1, TQ), 1)
    krow = lax.broadcasted_iota(I32, (TK, TQ), 0)
    q_of = lambda h: q_ref[0, h]

    def normalised(h):
        return acc_ref[h] * (1.0 / l_ref[h])

    kvc = kvc_ref[0, 0]
    vct = vct_ref[0, 0]
    n_row = lax.broadcasted_iota(I32, (CMP_PAD, TQ), 0)
    negc = jnp.where((CMP_STRIDE * n_row + (CMP_LEN - 1)) <= t_row, 0.0, -jnp.inf)
    j0 = pl.multiple_of(CMP_PAD - (TQ // CMP_STRIDE) * i, SUBLANE)
    psum_ref[...] = jnp.zeros(psum_ref.shape, F32)

    cparts = [slice(a * KS, (a + 1) * KS) for a in range(CMP_PAD // KS)]
    for h in range(r_heads):
        s = [_dot(kvc[sl], q_of(h)) + tabc_ref[0, h, pl.ds(j0 + sl.start, KS), :] + negc[sl] for sl in cparts]
        m = functools.reduce(jnp.maximum, [jnp.max(s_a, axis=0, keepdims=True) for s_a in s])
        m = jnp.maximum(m, F32_MIN)
        e = [jnp.exp2(s_a - m) for s_a in s]
        den = _colsum(functools.reduce(jnp.add, e))
        inv = 1.0 / jnp.where(den > 0.0, den, 1.0)
        acc = functools.reduce(jnp.add, [_dot(vct[:, sl], e_a.astype(BF16)) for sl, e_a in zip(cparts, e)])
        ocmp_ref[h] = acc[LANE - HEAD_DIM:] * inv
        for sl, e_a in zip(cparts, e):
            psum_ref[sl, :] += e_a * inv

    psum = psum_ref[...]
    ovl = ovl_ref[...]
    hi = psum.astype(BF16)
    rem1 = psum - hi.astype(F32)
    mid = rem1.astype(BF16)
    lo = (rem1 - mid.astype(F32)).astype(BF16)
    p_slc = _dot(ovl, hi) + _dot(ovl, mid) + _dot(ovl, lo)
    blk = lax.broadcasted_iota(I32, (LANE, TQ), 0)
    cur = jnp.right_shift(t_row, 6)
    forced = (blk == 0) | (blk == cur) | (blk == cur - 1)
    admissible = (blk * SEL_BLOCK) <= t_row
    score = jnp.where(admissible, p_slc + jnp.where(forced, FORCE_BONUS, 0.0), NEG)
    score = jnp.where(blk < n_sel, score, -jnp.inf)
    sel = _topk_cols(score, min(N_SEL, n_sel), blk) & (score > 0.5 * NEG)
    negsel_ref[...] = jnp.where(sel, 0.0, -jnp.inf)

    def bias_of(m0):
        return lambda h: _bias_block(lambda m: tab_ref[0, m, h], m0)

    def key_rows(ref, kt):
        return ref[0, pl.ds(pl.multiple_of(kt * TK, TK), TK), :]

    def sel_mask(kt):
        rows = [jnp.broadcast_to(negsel_ref[pl.ds((TK // SEL_BLOCK) * kt + a, 1), :], (SEL_BLOCK, TQ))
                for a in range(TK // SEL_BLOCK)]
        negm_ref[...] = jnp.where((kt * TK + krow) <= t_row, jnp.concatenate(rows, axis=0), -jnp.inf)

    _attend_tiles(r_heads, i + 1, q_of, lambda kt: key_rows(kvs_ref, kt), lambda kt: vst_ref[0, 0, kt],
                  lambda kt: bias_of((TQ // KS) * (i - kt)), sel_mask, negm_ref, s_ref, tmax_ref,
                  m_ref, l_ref, acc_ref)
    for h in range(r_heads):
        oslc_ref[h] = normalised(h)

    def win_mask(j):
        kt = i - j
        dist = t_row - (jnp.maximum(kt, 0) * TK + krow)
        ok = (dist >= 0) & (dist < jnp.where(kt >= 0, WINDOW, 0))
        negm_ref[...] = jnp.where(ok, 0.0, -jnp.inf)

    _attend_tiles(r_heads, (WINDOW + TQ) // TK, q_of, lambda j: key_rows(kvw_ref, jnp.maximum(i - j, 0)),
                  lambda j: vwt_ref[0, 0, jnp.maximum(i - j, 0)], lambda j: bias_of((TQ // KS) * j),
                  win_mask, negm_ref, s_ref, tmax_ref, m_ref, l_ref, acc_ref)

    gates = jax.nn.sigmoid(glt_ref[0, 0] + gbt_ref[0])
    for pair in range(r_heads // 2):
        outs = []
        for r in (2 * pair, 2 * pair + 1):
            c = N_BRANCH * r
            outs.append(gates[c:c + 1] * ocmp_ref[r] + gates[c + 1:c + 2] * oslc_ref[r]
                        + gates[c + 2:c + 3] * normalised(r))
        o_ref[0, :, pair * LANE:(pair + 1) * LANE] = jnp.concatenate(outs, axis=0).T.astype(o_ref.dtype)


def _nsa_attention(qt, kvc, vct, kvs, vst, kvw, vwt, glt, gbt, tab, tabc, ovl, n_sel):
    bsz, _, _, t = qt.shape
    g, r = NSA_KV_HEADS, NSA_GROUP
    nkt = t // TK
    once = dict(pipeline_mode=pl.Buffered(1))
    return pl.pallas_call(
        functools.partial(_nsa_kernel, n_sel=n_sel),
        grid=(bsz, g, t // TQ),
        in_specs=[pl.BlockSpec((1, r, LANE, TQ), lambda b, gg, i: (b, gg, 0, i)),
                  pl.BlockSpec((1, 1, CMP_PAD, LANE), lambda b, gg, i: (b, gg, 0, 0)),
                  pl.BlockSpec((1, 1, LANE, CMP_PAD), lambda b, gg, i: (b, gg, 0, 0)),
                  pl.BlockSpec((1, t, LANE), lambda b, gg, i: (b, 0, gg)),
                  pl.BlockSpec((1, 1, nkt, NSA_VT_ROWS, TK), lambda b, gg, i: (b, gg, 0, 0, 0)),
                  pl.BlockSpec((1, t, LANE), lambda b, gg, i: (b, 0, gg)),
                  pl.BlockSpec((1, 1, nkt, NSA_VT_ROWS, TK), lambda b, gg, i: (b, gg, 0, 0, 0)),
                  pl.BlockSpec((1, 1, GATE_ROWS, TQ), lambda b, gg, i: (b, gg, 0, i)),
                  pl.BlockSpec((1, GATE_ROWS, TQ), lambda b, gg, i: (gg, 0, 0)),
                  pl.BlockSpec((1, N_BIAS_TILES, r, KS, LANE), lambda b, gg, i: (gg, 0, 0, 0, 0), **once),
                  pl.BlockSpec((1, r, 2 * CMP_PAD, TQ), lambda b, gg, i: (gg, 0, 0, 0), **once),
                  pl.BlockSpec((LANE, CMP_PAD), lambda b, gg, i: (0, 0))],
        out_specs=pl.BlockSpec((1, TQ, r * HEAD_DIM), lambda b, gg, i: (b, i, gg)),
        out_shape=jax.ShapeDtypeStruct((bsz, t, g * r * HEAD_DIM), BF16),
        scratch_shapes=[pltpu.VMEM((CMP_PAD, TQ), F32), pltpu.VMEM((LANE, TQ), F32),
                        pltpu.VMEM((TK, TQ), F32), pltpu.VMEM((r, TK, TQ), F32),
                        pltpu.VMEM((r, 1, TQ), F32), pltpu.VMEM((r, 1, TQ), F32), pltpu.VMEM((r, 1, TQ), F32),
                        pltpu.VMEM((r, NSA_VT_ROWS, TQ), F32), pltpu.VMEM((r, HEAD_DIM, TQ), F32),
                        pltpu.VMEM((r, HEAD_DIM, TQ), F32)],
        compiler_params=pltpu.CompilerParams(
            dimension_semantics=("arbitrary", "arbitrary", "arbitrary"),
            vmem_limit_bytes=VMEM_LIMIT),
        name="nsa_attention",
    )(qt, kvc, vct, kvs, vst, kvw, vwt, glt, gbt, tab, tabc, ovl)


def _dsa_proj_kernel(x_ref, g_ref, w_ref, qn_ref, kvn_ref, kin_ref, wqu_ref, wuk_ref, wqi_ref,
                     qat_ref, qit_ref, wit_ref, ckv_ref, ckvt_ref, kidx_ref, qmem_ref):
    tm = x_ref.shape[1]
    y = _rms(x_ref[0], g_ref[...]).astype(BF16)
    c_q = _rms(_dot(y, w_ref[:, 0:Q_LORA]), qn_ref[...]).astype(BF16)
    c_kv = _rms(_dot(y, w_ref[:, Q_LORA:Q_LORA + KV_LORA]), kvn_ref[...])
    ckv_ref[0] = c_kv.astype(BF16)
    c_kv_t = c_kv.T.astype(BF16)
    for jt in range(tm // TK):
        ckvt_ref[0, jt] = c_kv_t[:, jt * TK:(jt + 1) * TK]
    off = Q_LORA + KV_LORA
    k_idx = _rms(_dot(y, w_ref[:, off:off + LANE]), kin_ref[...], n=IDX_DIM)
    kidx_ref[0] = k_idx.astype(BF16)
    off += LANE
    w_idx = _dot(y, w_ref[:, off:off + LANE]) * (IDX_HEADS ** -0.5 * IDX_DIM ** -0.5)
    wit_ref[0] = w_idx.T[:IDX_HEADS, :]
    off += LANE
    qmem_ref[0] = _spread_heads(_dot(y, w_ref[:, off:off + N_MEM_HEADS * HEAD_DIM]) * HEAD_DIM ** -0.5
                                ).astype(BF16)
    value_row = lax.broadcasted_iota(I32, (LANE, tm), 0) >= IDX_DIM
    for pair in range(IDX_HEADS // 2):
        acc_t = _dot(c_q, wqi_ref[:, pair * LANE:(pair + 1) * LANE]).T
        qit_ref[0, 2 * pair] = jnp.where(value_row, 0.0, acc_t).astype(BF16)
        qit_ref[0, 2 * pair + 1] = jnp.where(value_row, 0.0, pltpu.roll(acc_t, IDX_DIM, axis=0)).astype(BF16)
    for pair in range(N_MIX_HEADS // 2):
        q_pair = (_dot(c_q, wqu_ref[:, pair * LANE:(pair + 1) * LANE]) * HEAD_DIM ** -0.5).astype(BF16)
        for h in (2 * pair, 2 * pair + 1):
            qat_ref[0, h] = (_dot(q_pair, wuk_ref[h]) * LOG2E).T.astype(BF16)


def _dsa_proj(h, gain, w, qn, kvn, kin, wqu, wuk, wqi, tm):
    bsz, t, d = h.shape
    nkt = t // TK
    full = lambda a: pl.BlockSpec(a.shape, lambda b, i: (0,) * a.ndim)
    row = lambda n: pl.BlockSpec((1, tm, n), lambda b, i: (b, i, 0))
    return pl.pallas_call(
        _dsa_proj_kernel,
        grid=(bsz, t // tm),
        in_specs=[pl.BlockSpec((1, tm, d), lambda b, i: (b, i, 0)), full(gain), full(w), full(qn),
                  full(kvn), full(kin), full(wqu), full(wuk), full(wqi)],
        out_specs=[pl.BlockSpec((1, N_MIX_HEADS, LANE, tm), lambda b, i: (b, 0, 0, i)),
                   pl.BlockSpec((1, IDX_HEADS, LANE, tm), lambda b, i: (b, 0, 0, i)),
                   pl.BlockSpec((1, IDX_HEADS, tm), lambda b, i: (b, 0, i)),
                   row(LANE),
                   pl.BlockSpec((1, tm // TK, DSA_VT_ROWS, TK), lambda b, i: (b, i, 0, 0)),
                   row(LANE), row(N_MEM_HEADS * LANE)],
        out_shape=[jax.ShapeDtypeStruct((bsz, N_MIX_HEADS, LANE, t), BF16),
                   jax.ShapeDtypeStruct((bsz, IDX_HEADS, LANE, t), BF16),
                   jax.ShapeDtypeStruct((bsz, IDX_HEADS, t), F32),
                   jax.ShapeDtypeStruct((bsz, t, LANE), BF16),
                   jax.ShapeDtypeStruct((bsz, nkt, DSA_VT_ROWS, TK), BF16),
                   jax.ShapeDtypeStruct((bsz, t, LANE), BF16),
                   jax.ShapeDtypeStruct((bsz, t, N_MEM_HEADS * LANE), BF16)],
        compiler_params=pltpu.CompilerParams(dimension_semantics=("arbitrary", "arbitrary"),
                                             vmem_limit_bytes=VMEM_LIMIT),
        name="dsa_proj",
    )(h, gain, w, qn, kvn, kin, wqu, wuk, wqi)


def _dsa_kernel(qi_ref, wi_ref, qa_ref, kidx_ref, ckv_ref, ckvt_ref, tab_ref, o_ref,
                sc_ref, sc_hi_ref, negm_ref, s_ref, tmax_ref, m_ref, l_ref, acc_ref, *, topk):
    i = pl.program_id(1)
    qs = i * TQ
    n_tiles = i + 1
    sub = TK // KS
    t_row = qs + lax.broadcasted_iota(I32, (1, TQ), 1)
    krow = lax.broadcasted_iota(I32, (KS, TQ), 0)
    k_f = float(topk)

    grouped = lambda x: x.reshape(KS // SUBLANE, SUBLANE, TQ)

    def score_body(kt, carry):
        for a in range(sub):
            k0 = pl.multiple_of(kt * TK + a * KS, KS)
            kk = kidx_ref[0, pl.ds(k0, KS), :]
            sc = jnp.maximum(_dot(kk, qi_ref[0, 0]), 0.0) * wi_ref[0, 0:1, :]
            for h in range(1, IDX_HEADS):
                sc = sc + jnp.maximum(_dot(kk, qi_ref[0, h]), 0.0) * wi_ref[0, h:h + 1, :]
            sc = jnp.where((k0 + krow) <= t_row, sc, NEG)
            sc_ref[sub * kt + a] = sc
            sc_hi_ref[sub * kt + a] = sc.astype(BF16)
        return carry

    lax.fori_loop(0, n_tiles, score_body, 0)

    def count(pred):
        def body(kt, acc):
            for a in range(sub):
                hit = pred(sc_ref[sub * kt + a], kt * TK + a * KS).astype(F32)
                acc = acc + jnp.sum(grouped(hit), axis=0)
            return acc
        acc = lax.fori_loop(0, n_tiles, body, jnp.zeros((SUBLANE, TQ), F32))
        return jnp.sum(acc, axis=0, keepdims=True)

    def count_rounded(cand_f):
        cand = jnp.broadcast_to(cand_f, (BF16_ROWS, TQ)).astype(BF16)
        one, zero = jnp.ones((), BF16), jnp.zeros((), BF16)

        def body(kt, acc):
            for a in range(sub):
                tile = sc_hi_ref[sub * kt + a].reshape(KS // BF16_ROWS, BF16_ROWS, TQ)
                hit = jnp.where(tile >= cand[None], one, zero)
                part = functools.reduce(jnp.add, [hit[r] for r in range(KS // BF16_ROWS)])
                acc = acc + part.astype(F32)
            return acc
        acc = lax.fori_loop(0, n_tiles, body, jnp.zeros((BF16_ROWS, TQ), F32))
        return jnp.sum(acc, axis=0, keepdims=True)

    short = t_row < topk
    half_step = 1 << 15

    def bf16_key_to_float(v16):
        raw16 = jnp.where(v16 >= 0, v16, v16 ^ jnp.int32(0x7FFF))
        return pltpu.bitcast(jnp.left_shift(raw16, 16), F32)

    def high_body(it, v16):
        cand = v16 + jnp.left_shift(jnp.int32(1), 15 - it)
        return jnp.where(count_rounded(bf16_key_to_float(cand)) >= k_f, cand, v16)

    v16 = lax.fori_loop(0, 16, high_body, jnp.full((1, TQ), -(1 << 15), I32))
    key_g = jnp.where(v16 >= 0, jnp.left_shift(v16, 16), jnp.left_shift(v16, 16) | jnp.int32(0xFFFF))

    def low_pass(bit, v, cnt_v):
        cand = v + jnp.left_shift(jnp.int32(1), bit)
        cand_f = _key_to_float(cand)
        cnt = count(lambda sc, k0: sc >= cand_f)
        take = cnt >= k_f
        return jnp.where(take, cand, v), jnp.where(take, cnt, cnt_v)

    def open_count(cnt_v):
        return jnp.sum(((cnt_v != k_f) & ~short).astype(I32))

    passes_per_check = 4
    v, cnt_v = low_pass(16, key_g - half_step, jnp.full((1, TQ), 1e9, F32))

    def v_cond(c):
        return (c[0] >= 0) & (c[3] > 0)

    def v_body(c):
        bit, v, cnt_v, _ = c
        for step in range(passes_per_check):
            v, cnt_v = low_pass(bit - step, v, cnt_v)
        return bit - passes_per_check, v, cnt_v, open_count(cnt_v)

    _, v, cnt_v, open_cols = lax.while_loop(v_cond, v_body, (jnp.int32(15), v, cnt_v, open_count(cnt_v)))
    thr = _key_to_float(v)

    def tie_cut(_):
        need = k_f - count(lambda sc, k0: sc > thr)

        def c_body(it, c):
            cand = c + jnp.left_shift(jnp.int32(1), 13 - it)
            cnt = count(lambda sc, k0: (sc == thr) & ((k0 + krow) < cand))
            return jnp.where(cnt < need, cand, c)

        return lax.fori_loop(0, 14, c_body, jnp.zeros((1, TQ), I32))

    cut = lax.cond(open_cols > 0, tie_cut, lambda _: jnp.full((1, TQ), 2 ** 30, I32), 0)

    def att_mask(kt):
        for a in range(sub):
            sc = sc_ref[sub * kt + a]
            kpos = kt * TK + a * KS + krow
            chosen = short | (sc > thr) | ((sc == thr) & (kpos <= cut))
            negm_ref[a * KS:(a + 1) * KS, :] = jnp.where(chosen & (kpos <= t_row), 0.0, -jnp.inf)

    _attend_tiles(N_MIX_HEADS, n_tiles, lambda h: qa_ref[0, h],
                  lambda kt: ckv_ref[0, pl.ds(pl.multiple_of(kt * TK, TK), TK), :],
                  lambda kt: ckvt_ref[0, kt],
                  lambda kt: (lambda h: _bias_block(lambda m: tab_ref[m, h], (TQ // KS) * (i - kt))),
                  att_mask, negm_ref, s_ref, tmax_ref, m_ref, l_ref, acc_ref)
    for h in range(N_MIX_HEADS):
        out = acc_ref[h] * (1.0 / l_ref[h])
        o_ref[0, :, h * LANE:(h + 1) * LANE] = out.T.astype(o_ref.dtype)


def _dsa_attention(qit, wit, qat, kidx, ckv, ckvt, tab, topk):
    bsz, _, _, t = qat.shape
    nkt = t // TK
    return pl.pallas_call(
        functools.partial(_dsa_kernel, topk=topk),
        grid=(bsz, t // TQ),
        in_specs=[pl.BlockSpec((1, IDX_HEADS, LANE, TQ), lambda b, i: (b, 0, 0, i)),
                  pl.BlockSpec((1, IDX_HEADS, TQ), lambda b, i: (b, 0, i)),
                  pl.BlockSpec((1, N_MIX_HEADS, LANE, TQ), lambda b, i: (b, 0, 0, i)),
                  pl.BlockSpec((1, t, LANE), lambda b, i: (b, 0, 0)),
                  pl.BlockSpec((1, t, LANE), lambda b, i: (b, 0, 0)),
                  pl.BlockSpec((1, nkt, DSA_VT_ROWS, TK), lambda b, i: (b, 0, 0, 0)),
                  pl.BlockSpec(tab.shape, lambda b, i: (0, 0, 0, 0), pipeline_mode=pl.Buffered(1))],
        out_specs=pl.BlockSpec((1, TQ, N_MIX_HEADS * LANE), lambda b, i: (b, i, 0)),
        out_shape=jax.ShapeDtypeStruct((bsz, t, N_MIX_HEADS * LANE), BF16),
        scratch_shapes=[pltpu.VMEM((t // KS, KS, TQ), F32), pltpu.VMEM((t // KS, KS, TQ), BF16),
                        pltpu.VMEM((TK, TQ), F32),
                        pltpu.VMEM((N_MIX_HEADS, TK, TQ), F32), pltpu.VMEM((N_MIX_HEADS, 1, TQ), F32),
                        pltpu.VMEM((N_MIX_HEADS, 1, TQ), F32), pltpu.VMEM((N_MIX_HEADS, 1, TQ), F32),
                        pltpu.VMEM((N_MIX_HEADS, DSA_VT_ROWS, TQ), F32)],
        compiler_params=pltpu.CompilerParams(dimension_semantics=("arbitrary", "arbitrary"),
                                             vmem_limit_bytes=VMEM_LIMIT),
        name="dsa_attention",
    )(qit, wit, qat, kidx, ckv, ckvt, tab)


def _post_kernel(h_ref, mix_ref, qmem_ref, kvm_ref, *rest, has_uv):
    if has_uv:
        wuv_ref, wmix_ref, wmem_ref, o_ref = rest
    else:
        wmix_ref, wmem_ref, o_ref = rest
    mix = mix_ref[0]
    if has_uv:
        mix = _dot(mix, wuv_ref[...]).astype(BF16)
    upd = _dot(mix, wmix_ref[...])
    qm = qmem_ref[0]
    for hm in range(N_MEM_HEADS):
        sl = slice(hm * LANE, (hm + 1) * LANE)
        kv = kvm_ref[0, :, sl]
        s = _dot_nt(qm[:, sl], kv)
        e = jnp.exp(s - jnp.max(s, axis=-1, keepdims=True))
        p = e / jnp.sum(e, axis=-1, keepdims=True)
        o_h = _dot(p.astype(BF16), kv).astype(BF16)
        upd = upd + _dot(o_h, wmem_ref[sl, :])
    o_ref[0] = h_ref[0] + upd


def _post(h, mix, qmem, kvm, w_uv, w_mix, w_mem, tm):
    bsz, t, d = h.shape
    has_uv = w_uv is not None
    full = lambda a: pl.BlockSpec(a.shape, lambda b, i: (0,) * a.ndim)
    weights = ([w_uv] if has_uv else []) + [w_mix, w_mem]
    return pl.pallas_call(
        functools.partial(_post_kernel, has_uv=has_uv),
        grid=(bsz, t // tm),
        in_specs=[pl.BlockSpec((1, tm, d), lambda b, i: (b, i, 0)),
                  pl.BlockSpec((1, tm, mix.shape[2]), lambda b, i: (b, i, 0)),
                  pl.BlockSpec((1, tm, qmem.shape[2]), lambda b, i: (b, i, 0)),
                  pl.BlockSpec((1,) + kvm.shape[1:], lambda b, i: (b, 0, 0))]
                 + [full(w) for w in weights],
        out_specs=pl.BlockSpec((1, tm, d), lambda b, i: (b, i, 0)),
        out_shape=jax.ShapeDtypeStruct((bsz, t, d), F32),
        compiler_params=pltpu.CompilerParams(dimension_semantics=("arbitrary", "arbitrary"),
                                             vmem_limit_bytes=VMEM_LIMIT),
        name="mem_attn_out_proj",
    )(h, mix, qmem, kvm, *weights)


def _ffn_kernel(h_ref, g_ref, wg_ref, wu_ref, wd_ref, gf_ref, o_ref, *, final_norm):
    h = h_ref[...]
    hn = _rms(h, g_ref[...]).astype(BF16)
    act = (jax.nn.silu(_dot(hn, wg_ref[...])) * _dot(hn, wu_ref[...])).astype(BF16)
    out = h + _dot(act, wd_ref[...])
    if final_norm:
        out = _rms(out, gf_ref[...])
    o_ref[...] = out


def _ffn(h2d, gain, wg, wu, wd, gain_final, final_norm, tm):
    m, d = h2d.shape
    const = lambda a: pl.BlockSpec(a.shape, lambda i: (0,) * a.ndim, pipeline_mode=pl.Buffered(1))
    return pl.pallas_call(
        functools.partial(_ffn_kernel, final_norm=final_norm),
        grid=(m // tm,),
        in_specs=[pl.BlockSpec((tm, d), lambda i: (i, 0)), const(gain), const(wg), const(wu),
                  const(wd), const(gain_final)],
        out_specs=pl.BlockSpec((tm, d), lambda i: (i, 0)),
        out_shape=jax.ShapeDtypeStruct((m, d), F32),
        compiler_params=pltpu.CompilerParams(dimension_semantics=("arbitrary",),
                                             vmem_limit_bytes=VMEM_LIMIT),
        name="swiglu_ffn",
    )(h2d, gain, wg, wu, wd, gain_final)


def _pad_cols(w, n):
    return jnp.pad(w, ((0, 0), (0, n - w.shape[1])))


def _value_rows(w_rows, n_heads):
    d_out = w_rows.shape[1]
    w = w_rows.reshape(n_heads, HEAD_DIM, d_out)
    return jnp.pad(w, ((0, 0), (LANE - HEAD_DIM, 0), (0, 0))).reshape(n_heads * LANE, d_out)


def _interleave_kv(k, v, n_heads):
    d_in = k.shape[0]
    kv = jnp.concatenate([k.reshape(d_in, n_heads, HEAD_DIM), v.reshape(d_in, n_heads, HEAD_DIM)], axis=2)
    return kv.reshape(d_in, n_heads * LANE)


def _bucket_bias(rel_bias, bucket_np):
    onehot = jax.nn.one_hot(jnp.asarray(bucket_np.reshape(-1)), N_REL_BUCKETS, dtype=F32)
    out = jnp.dot(onehot, rel_bias, precision=lax.Precision.HIGHEST) * LOG2E
    return out.reshape(bucket_np.shape + (rel_bias.shape[1],))


def _bias_tiles(rel_bias):
    m = np.arange(N_BIAS_TILES)[:, None, None]
    dist = LANE * m + np.arange(LANE)[None, None, :] - np.arange(KS)[None, :, None]
    return jnp.transpose(_bucket_bias(rel_bias, _rel_bucket_np(dist)), (0, 3, 1, 2))


def _bias_cmp_table(rel_bias):
    rel = np.arange(2 * CMP_PAD) - CMP_PAD
    dist = np.arange(TQ)[None, :] - CMP_STRIDE * rel[:, None] - (CMP_LEN - 1)
    return jnp.transpose(_bucket_bias(rel_bias, _rel_bucket_np(dist)), (2, 0, 1))


def _overlap_matrix(t):
    n_cmp = (t - CMP_LEN) // CMP_STRIDE + 1
    n_sel = t // SEL_BLOCK
    cs = np.arange(CMP_PAD) * CMP_STRIDE
    ss = np.arange(LANE) * SEL_BLOCK
    ov = (cs[None, :] <= ss[:, None] + SEL_BLOCK - 1) & (cs[None, :] + CMP_LEN - 1 >= ss[:, None])
    ov &= (np.arange(CMP_PAD) < n_cmp)[None, :] & (np.arange(LANE) < n_sel)[:, None]
    return jnp.asarray(ov, BF16)


def kernel(x, mem, rel_bias, norm_mix, norm_ffn, norm_mem, w_mem_kv, w_out, ffn_gate, ffn_up, ffn_down,
           nsa_w_in, nsa_gate_b, nsa_cmp_pos_k, nsa_cmp_pos_v,
           nsa_cmp_k_w1, nsa_cmp_k_b1, nsa_cmp_k_w2, nsa_cmp_k_b2,
           nsa_cmp_v_w1, nsa_cmp_v_b1, nsa_cmp_v_w2, nsa_cmp_v_b2,
           dsa_w_in, dsa_q_norm, dsa_kv_norm, dsa_w_q_up, dsa_w_uk, dsa_w_uv, dsa_w_q_idx, dsa_kidx_norm,
           norm_final):
    bsz, t, d = x.shape
    m_len = mem.shape[1]
    depth = norm_mix.shape[0]
    g, r, hh = NSA_KV_HEADS, NSA_GROUP, N_MIX_HEADS
    d_mix = hh * HEAD_DIM
    kvw_ = g * HEAD_DIM
    assert t % 1024 == 0 and t // SEL_BLOCK <= LANE and t // CMP_STRIDE <= CMP_PAD
    tm = 512
    h = x.astype(F32)

    tab = _bias_tiles(rel_bias)
    tab_nsa = jnp.transpose(tab.reshape(N_BIAS_TILES, g, r, KS, LANE), (1, 0, 2, 3, 4))
    tabc = _bias_cmp_table(rel_bias).reshape(g, r, 2 * CMP_PAD, TQ)
    ovl = _overlap_matrix(t)

    for layer in range(depth):
        j = layer // 2
        wkv = w_mem_kv[layer]
        wkv = _interleave_kv(wkv[:, :N_MEM_HEADS * HEAD_DIM], wkv[:, N_MEM_HEADS * HEAD_DIM:], N_MEM_HEADS)
        kvm = _rms_proj(mem.reshape(bsz * m_len, d), norm_mem[layer], wkv.astype(BF16), BF16, m_len)
        kvm = kvm.reshape(bsz, m_len, N_MEM_HEADS * LANE)
        w_o = w_out[layer]
        w_mem_o = _value_rows(w_o[d_mix:], N_MEM_HEADS).astype(BF16)
        w_mix = w_o[:d_mix].astype(BF16)

        if layer % 2 == 0:
            w = nsa_w_in[j]
            c = np.cumsum([0, d_mix, kvw_, kvw_, kvw_, kvw_, kvw_, kvw_, hh * N_BRANCH,
                           N_MEM_HEADS * HEAD_DIM])
            wq, wkc, wvc, wks, wvs, wkw, wvw, wgl, wqm = [w[:, c[k]:c[k + 1]] for k in range(9)]
            n_g = r * N_BRANCH
            wgl = jnp.concatenate([_pad_cols(wgl[:, gg * n_g:(gg + 1) * n_g], LANE) for gg in range(g)], 1)
            w_all = jnp.concatenate([wq, wkc, wvc, _interleave_kv(wks, wvs, g),
                                     _interleave_kv(wkw, wvw, g), wgl, wqm], axis=1)
            qt, cmp_raw, kvs, vst, kvw, vwt, glt, qmem = _nsa_proj(h, norm_mix[layer], w_all.astype(BF16), tm)
            nc = t // CMP_STRIDE
            x2 = jnp.transpose(cmp_raw.reshape(bsz, t, 2 * g, HEAD_DIM), (0, 2, 1, 3))
            x2 = x2.reshape(bsz, 2 * g, nc, CMP_STRIDE * HEAD_DIM)
            pos = jnp.stack([nsa_cmp_pos_k[j], nsa_cmp_pos_v[j]]).reshape(2, 2, CMP_STRIDE * HEAD_DIM)
            w1 = jnp.stack([nsa_cmp_k_w1[j], nsa_cmp_v_w1[j]]).astype(BF16)
            b1 = jnp.stack([nsa_cmp_k_b1[j], nsa_cmp_v_b1[j]])[:, None, :]
            w2 = jnp.stack([jnp.pad(nsa_cmp_k_w2[j], ((0, 0), (0, HEAD_DIM))),
                            jnp.pad(nsa_cmp_v_w2[j], ((0, 0), (HEAD_DIM, 0)))]).astype(BF16)
            b2 = jnp.stack([jnp.pad(nsa_cmp_k_b2[j], (0, HEAD_DIM)),
                            jnp.pad(nsa_cmp_v_b2[j], (HEAD_DIM, 0))])[:, None, :]
            kvc = _compress(x2, pos, w1, b1, w2, b2)
            kvc = jnp.pad(kvc, ((0, 0), (0, 0), (0, CMP_PAD - nc), (0, 0)))
            vct = jnp.transpose(kvc, (0, 1, 3, 2))
            gb = jnp.pad(nsa_gate_b[j].reshape(g, n_g), ((0, 0), (0, GATE_ROWS - n_g)))
            gbt = jnp.broadcast_to(gb[:, :, None], (g, GATE_ROWS, TQ))
            mix = _nsa_attention(qt, kvc, vct, kvs, vst, kvw, vwt, glt, gbt, tab_nsa, tabc, ovl,
                                 t // SEL_BLOCK)
            w_uv = None
        else:
            w = dsa_w_in[j]
            c = np.cumsum([0, Q_LORA, KV_LORA, IDX_DIM, IDX_HEADS, N_MEM_HEADS * HEAD_DIM])
            wcq, wckv, wki, wwi, wqm = [w[:, c[k]:c[k + 1]] for k in range(5)]
            w_all = jnp.concatenate([wcq, wckv, _pad_cols(wki, LANE), _pad_cols(wwi, LANE),
                                     wqm], axis=1).astype(BF16)
            wuk = jnp.transpose(dsa_w_uk[j], (1, 2, 0))
            wuk = jnp.stack([jnp.pad(wuk[hd], ((HEAD_DIM * (hd % 2), HEAD_DIM * (1 - hd % 2)), (0, 0)))
                             for hd in range(hh)]).astype(BF16)
            qat, qit, wit, ckv, ckvt, kidx, qmem = _dsa_proj(
                h, norm_mix[layer][None], w_all, dsa_q_norm[j][None], dsa_kv_norm[j][None],
                _pad_cols(dsa_kidx_norm[j][None], LANE), dsa_w_q_up[j].astype(BF16), wuk,
                dsa_w_q_idx[j].astype(BF16), tm)
            mix = _dsa_attention(qit, wit, qat, kidx, ckv, ckvt, tab, min(DSA_TOPK, t // 4))
            wv = jnp.transpose(dsa_w_uv[j], (1, 0, 2))
            w_uv = (jnp.eye(hh, dtype=F32)[:, None, :, None] * wv[:, :, None, :]
                    ).reshape(hh * KV_LORA, d_mix).astype(BF16)

        h = _post(h, mix, qmem, kvm, w_uv, w_mix, w_mem_o, tm)
        last = layer == depth - 1
        h = _ffn(h.reshape(bsz * t, d), norm_ffn[layer][None], ffn_gate[layer].astype(BF16),
                 ffn_up[layer].astype(BF16), ffn_down[layer].astype(BF16), norm_final[None],
                 last, tm).reshape(bsz, t, d)
    return h.astype(x.dtype)
```

```python
import functools
import math

import numpy as np
import jax
import jax.numpy as jnp
from jax import lax
from jax.experimental import pallas as pl
from jax.experimental.pallas import tpu as pltpu

F32 = jnp.float32
BF16 = jnp.bfloat16
I32 = jnp.int32

NEG = -1e30
EPS = 1e-6
LOG2E = math.log2(math.e)
LANE = 128
SUBLANE = 8
BF16_ROWS = 16
HEAD_DIM = 64
N_MIX_HEADS = 12
N_MEM_HEADS = 4
N_REL_BUCKETS = 32
REL_MAX_EXACT = 16
REL_MAX_DIST = 2048
NSA_KV_HEADS = 2
NSA_GROUP = N_MIX_HEADS // NSA_KV_HEADS
N_BRANCH = 3
GATE_ROWS = 24
CMP_LEN = 32
CMP_STRIDE = 16
SEL_BLOCK = 64
N_SEL = 16
WINDOW = 512
FORCE_BONUS = 1e4
Q_LORA = 256
KV_LORA = 128
IDX_HEADS = 8
IDX_DIM = 64
DSA_TOPK = 256
TQ = 256
TK = 256
KS = LANE
N_BIAS_TILES = REL_MAX_DIST // KS + 2
CMP_PAD = 512
NSA_VT_ROWS = HEAD_DIM
DSA_VT_ROWS = KV_LORA
VMEM_LIMIT = 56 * 1024 * 1024
INT_MIN = -2 ** 31
F32_MIN = float(np.finfo(np.float32).min)


def _dot(a, b):
    return jnp.dot(a, b, preferred_element_type=F32)


def _dot_nt(a, b):
    return lax.dot_general(a, b, (((1,), (1,)), ((), ())), preferred_element_type=F32)


def _rms(x, gain, n=None):
    n = x.shape[-1] if n is None else n
    ms = jnp.sum(x * x, axis=-1, keepdims=True) * (1.0 / n)
    return x * lax.rsqrt(ms + EPS) * gain


def _rel_bucket_np(dist):
    n = np.maximum(dist, 0)
    nf = np.maximum(n, REL_MAX_EXACT).astype(np.float32)
    large = REL_MAX_EXACT + (np.log(nf / np.float32(REL_MAX_EXACT))
                             / np.float32(math.log(REL_MAX_DIST / REL_MAX_EXACT))
                             * np.float32(N_REL_BUCKETS - REL_MAX_EXACT)).astype(np.int32)
    large = np.minimum(large, N_REL_BUCKETS - 1)
    return np.where(n < REL_MAX_EXACT, n, large).astype(np.int32)


def _spread_heads(x):
    rows, n = x.shape
    low = lax.broadcasted_iota(I32, (rows, LANE), 1) < HEAD_DIM
    out = []
    for j in range(n // LANE):
        pair = x[:, j * LANE:(j + 1) * LANE]
        out += [jnp.where(low, pair, 0.0), jnp.where(low, pltpu.roll(pair, HEAD_DIM, axis=1), 0.0)]
    return jnp.concatenate(out, axis=1)


def _colsum(x):
    rows, n = x.shape
    return jnp.sum(jnp.sum(x.reshape(rows // SUBLANE, SUBLANE, n), axis=0), axis=0, keepdims=True)


def _key_to_float(v):
    bits = jnp.where(v >= 0, v, v ^ jnp.int32(0x7FFFFFFF))
    return pltpu.bitcast(bits, F32)


def _topk_cols(score, k, row_idx):
    n = score.shape[1]

    def vbody(it, v):
        cand = v + jnp.left_shift(jnp.int32(1), 31 - it)
        cnt = _colsum((score >= _key_to_float(cand)).astype(F32))
        return jnp.where(cnt >= k, cand, v)

    v = lax.fori_loop(0, 32, vbody, jnp.full((1, n), INT_MIN, I32))
    thr = _key_to_float(v)
    gt = score > thr
    eq = score == thr
    need = k - _colsum(gt.astype(F32))

    def cbody(it, c):
        cand = c + jnp.left_shift(jnp.int32(1), 6 - it)
        cnt = _colsum((eq & (row_idx < cand)).astype(F32))
        return jnp.where(cnt < need, cand, c)

    c = lax.fori_loop(0, 7, cbody, jnp.zeros((1, n), I32))
    return gt | (eq & (row_idx <= c))


def _bias_block(tile_of, m0):
    idx = lambda m: jnp.clip(m, 0, N_BIAS_TILES - 1)
    return [jnp.concatenate([tile_of(idx(m0 + b - a)) for b in range(TQ // LANE)], axis=1)
            for a in range(TK // KS)]


def _attend_tiles(n_heads, n_tiles, q_of, kv_of, vt_of, bias_of, fill_mask, negm_ref, s_ref, tmax_ref,
                  m_ref, l_ref, acc_ref):
    parts = [slice(a * KS, (a + 1) * KS) for a in range(TK // KS)]
    m_ref[...] = jnp.full(m_ref.shape, F32_MIN, F32)
    l_ref[...] = jnp.zeros(l_ref.shape, F32)
    acc_ref[...] = jnp.zeros(acc_ref.shape, F32)

    def logits_of(j):
        fill_mask(j)
        kv = kv_of(j)
        bias_h = bias_of(j)

        def run(h):
            bias = bias_h(h)
            tmax = None
            for a, sl in enumerate(parts):
                s_a = _dot(kv[sl], q_of(h)) + bias[a] + negm_ref[sl, :]
                s_ref[h, sl, :] = s_a
                mx = jnp.max(s_a, axis=0, keepdims=True)
                tmax = mx if tmax is None else jnp.maximum(tmax, mx)
            tmax_ref[h] = tmax
        return run

    def accumulate(j):
        vt = vt_of(j)

        def run(h):
            m_old = m_ref[h]
            m_new = jnp.maximum(m_old, tmax_ref[h])
            alpha = jnp.exp2(m_old - m_new)
            m_ref[h] = m_new
            p = [jnp.exp2(s_ref[h, sl, :] - m_new) for sl in parts]
            l_ref[h] = alpha * l_ref[h] + _colsum(sum(p[1:], p[0]))
            acc = alpha * acc_ref[h]
            for sl, p_a in zip(parts, p):
                acc = acc + _dot(vt[:, sl], p_a.astype(BF16))
            acc_ref[h] = acc
        return run

    first = logits_of(0)
    for h in range(n_heads):
        first(h)

    def body(j, carry):
        consume = accumulate(j)
        produce = logits_of(jnp.minimum(j + 1, n_tiles - 1))
        for h in range(n_heads):
            consume(h)
            produce(h)
        return carry

    lax.fori_loop(0, n_tiles, body, 0)


def _rms_proj_kernel(x_ref, g_ref, w_ref, o_ref):
    y = _rms(x_ref[...], g_ref[...]).astype(BF16)
    o_ref[...] = _dot(y, w_ref[...]).astype(o_ref.dtype)


def _rms_proj(x2d, gain, w_bf16, out_dtype, tm):
    m, d = x2d.shape
    n = w_bf16.shape[1]
    return pl.pallas_call(
        _rms_proj_kernel,
        grid=(m // tm,),
        in_specs=[pl.BlockSpec((tm, d), lambda i: (i, 0)),
                  pl.BlockSpec((1, d), lambda i: (0, 0)),
                  pl.BlockSpec((d, n), lambda i: (0, 0))],
        out_specs=pl.BlockSpec((tm, n), lambda i: (i, 0)),
        out_shape=jax.ShapeDtypeStruct((m, n), out_dtype),
        compiler_params=pltpu.CompilerParams(dimension_semantics=("arbitrary",),
                                             vmem_limit_bytes=VMEM_LIMIT),
        name="rms_proj",
    )(x2d, gain.reshape(1, d), w_bf16)


def _nsa_proj_kernel(x_ref, g_ref, w_ref, qt_ref, cmp_ref, kvs_ref, vst_ref, kvw_ref, vwt_ref,
                     glt_ref, qmem_ref):
    tm = x_ref.shape[1]
    y = _rms(x_ref[0], g_ref[...]).astype(BF16)
    value_row = lax.broadcasted_iota(I32, (LANE, tm), 0) >= HEAD_DIM
    off = 0
    for pair in range(N_MIX_HEADS // 2):
        acc_t = (_dot(y, w_ref[:, off:off + LANE]) * (HEAD_DIM ** -0.5 * LOG2E)).T
        qt_ref[0, 2 * pair] = jnp.where(value_row, 0.0, acc_t).astype(BF16)
        qt_ref[0, 2 * pair + 1] = jnp.where(value_row, 0.0, pltpu.roll(acc_t, HEAD_DIM, axis=0)).astype(BF16)
        off += LANE
    cmp_ref[0] = _dot(y, w_ref[:, off:off + 2 * LANE])
    off += 2 * LANE
    for kv_ref, vt_ref in ((kvs_ref, vst_ref), (kvw_ref, vwt_ref)):
        for g in range(NSA_KV_HEADS):
            acc = _dot(y, w_ref[:, off:off + LANE])
            off += LANE
            kv_ref[0, :, g * LANE:(g + 1) * LANE] = acc.astype(BF16)
            v_t = acc.T[HEAD_DIM:, :].astype(BF16)
            for jt in range(tm // TK):
                vt_ref[0, g, jt] = v_t[:, jt * TK:(jt + 1) * TK]
    for g in range(NSA_KV_HEADS):
        acc = _dot(y, w_ref[:, off:off + LANE])
        off += LANE
        glt_ref[0, g] = acc.T[:GATE_ROWS, :]
    qmem_ref[0] = _spread_heads(_dot(y, w_ref[:, off:off + N_MEM_HEADS * HEAD_DIM]) * HEAD_DIM ** -0.5
                                ).astype(BF16)


def _nsa_proj(h, gain, w, tm):
    bsz, t, d = h.shape
    g = NSA_KV_HEADS
    nkt = t // TK
    row = lambda n: pl.BlockSpec((1, tm, n), lambda b, i: (b, i, 0))
    vt_spec = pl.BlockSpec((1, g, tm // TK, NSA_VT_ROWS, TK), lambda b, i: (b, 0, i, 0, 0))
    return pl.pallas_call(
        _nsa_proj_kernel,
        grid=(bsz, t // tm),
        in_specs=[pl.BlockSpec((1, tm, d), lambda b, i: (b, i, 0)),
                  pl.BlockSpec((1, d), lambda b, i: (0, 0)),
                  pl.BlockSpec(w.shape, lambda b, i: (0, 0))],
        out_specs=[pl.BlockSpec((1, N_MIX_HEADS, LANE, tm), lambda b, i: (b, 0, 0, i)),
                   row(2 * LANE), row(g * LANE), vt_spec, row(g * LANE), vt_spec,
                   pl.BlockSpec((1, g, GATE_ROWS, tm), lambda b, i: (b, 0, 0, i)),
                   row(N_MEM_HEADS * LANE)],
        out_shape=[jax.ShapeDtypeStruct((bsz, N_MIX_HEADS, LANE, t), BF16),
                   jax.ShapeDtypeStruct((bsz, t, 2 * LANE), F32),
                   jax.ShapeDtypeStruct((bsz, t, g * LANE), BF16),
                   jax.ShapeDtypeStruct((bsz, g, nkt, NSA_VT_ROWS, TK), BF16),
                   jax.ShapeDtypeStruct((bsz, t, g * LANE), BF16),
                   jax.ShapeDtypeStruct((bsz, g, nkt, NSA_VT_ROWS, TK), BF16),
                   jax.ShapeDtypeStruct((bsz, g, GATE_ROWS, t), F32),
                   jax.ShapeDtypeStruct((bsz, t, N_MEM_HEADS * LANE), BF16)],
        compiler_params=pltpu.CompilerParams(dimension_semantics=("arbitrary", "arbitrary"),
                                             vmem_limit_bytes=VMEM_LIMIT),
        name="nsa_proj",
    )(h, gain.reshape(1, d), w)


def _compress_kernel(xk_ref, xv_ref, pos_ref, w1_ref, b1_ref, w2_ref, b2_ref, o_ref):
    nc = xk_ref.shape[2]
    half = xk_ref.shape[3]
    out = None
    for j, x_ref in enumerate((xk_ref, xv_ref)):
        x = x_ref[0, 0]
        top = _dot((x + pos_ref[j, 0:1, :]).astype(BF16), w1_ref[j, :half, :])
        bot = _dot((x + pos_ref[j, 1:2, :]).astype(BF16), w1_ref[j, half:, :])
        pre = top + pltpu.roll(bot, nc - 1, axis=0) + b1_ref[j]
        hid = jax.nn.gelu(pre)
        res = _dot(hid.astype(BF16), w2_ref[j]) + b2_ref[j]
        out = res if out is None else out + res
    o_ref[0, 0] = out.astype(o_ref.dtype)


def _compress(x2, pos, w1, b1, w2, b2):
    bsz, _, nc, width = x2.shape
    g = NSA_KV_HEADS
    return pl.pallas_call(
        _compress_kernel,
        grid=(bsz, g),
        in_specs=[pl.BlockSpec((1, 1, nc, width), lambda b, gg: (b, gg, 0, 0)),
                  pl.BlockSpec((1, 1, nc, width), lambda b, gg: (b, gg + NSA_KV_HEADS, 0, 0)),
                  pl.BlockSpec(pos.shape, lambda b, gg: (0, 0, 0)),
                  pl.BlockSpec(w1.shape, lambda b, gg: (0, 0, 0)),
                  pl.BlockSpec(b1.shape, lambda b, gg: (0, 0, 0)),
                  pl.BlockSpec(w2.shape, lambda b, gg: (0, 0, 0)),
                  pl.BlockSpec(b2.shape, lambda b, gg: (0, 0, 0))],
        out_specs=pl.BlockSpec((1, 1, nc, LANE), lambda b, gg: (b, gg, 0, 0)),
        out_shape=jax.ShapeDtypeStruct((bsz, g, nc, LANE), BF16),
        compiler_params=pltpu.CompilerParams(dimension_semantics=("arbitrary", "arbitrary"),
                                             vmem_limit_bytes=VMEM_LIMIT),
        name="nsa_compress",
    )(x2, x2, pos, w1, b1, w2, b2)


def _nsa_kernel(q_ref, kvc_ref, vct_ref, kvs_ref, vst_ref, kvw_ref, vwt_ref, glt_ref, gbt_ref,
                tab_ref, tabc_ref, ovl_ref, o_ref,
                psum_ref, negsel_ref, negm_ref, s_ref, tmax_ref, m_ref, l_ref, acc_ref, ocmp_ref, oslc_ref,
                *, n_sel):
    r_heads = NSA_GROUP
    i = pl.program_id(2)
    qs = i * TQ
    t_row = qs + lax.broadcasted_iota(I32, (1, TQ), 1)
    krow = lax.broadcasted_iota(I32, (TK, TQ), 0)
    q_of = lambda h: q_ref[0, h]

    def normalised(h):
        return acc_ref[h] * (1.0 / l_ref[h])

    kvc = kvc_ref[0, 0]
    vct = vct_ref[0, 0]
    n_row = lax.broadcasted_iota(I32, (CMP_PAD, TQ), 0)
    negc = jnp.where((CMP_STRIDE * n_row + (CMP_LEN - 1)) <= t_row, 0.0, -jnp.inf)
    j0 = pl.multiple_of(CMP_PAD - (TQ // CMP_STRIDE) * i, SUBLANE)
    psum_ref[...] = jnp.zeros(psum_ref.shape, F32)

    cparts = [slice(a * KS, (a + 1) * KS) for a in range(CMP_PAD // KS)]
    for h in range(r_heads):
        s = [_dot(kvc[sl], q_of(h)) + tabc_ref[0, h, pl.ds(j0 + sl.start, KS), :] + negc[sl] for sl in cparts]
        m = functools.reduce(jnp.maximum, [jnp.max(s_a, axis=0, keepdims=True) for s_a in s])
        m = jnp.maximum(m, F32_MIN)
        e = [jnp.exp2(s_a - m) for s_a in s]
        den = _colsum(functools.reduce(jnp.add, e))
        inv = 1.0 / jnp.where(den > 0.0, den, 1.0)
        acc = functools.reduce(jnp.add, [_dot(vct[:, sl], e_a.astype(BF16)) for sl, e_a in zip(cparts, e)])
        ocmp_ref[h] = acc[LANE - HEAD_DIM:] * inv
        for sl, e_a in zip(cparts, e):
            psum_ref[sl, :] += e_a * inv

    psum = psum_ref[...]
    ovl = ovl_ref[...]
    hi = psum.astype(BF16)
    rem1 = psum - hi.astype(F32)
    mid = rem1.astype(BF16)
    lo = (rem1 - mid.astype(F32)).astype(BF16)
    p_slc = _dot(ovl, hi) + _dot(ovl, mid) + _dot(ovl, lo)
    blk = lax.broadcasted_iota(I32, (LANE, TQ), 0)
    cur = jnp.right_shift(t_row, 6)
    forced = (blk == 0) | (blk == cur) | (blk == cur - 1)
    admissible = (blk * SEL_BLOCK) <= t_row
    score = jnp.where(admissible, p_slc + jnp.where(forced, FORCE_BONUS, 0.0), NEG)
    score = jnp.where(blk < n_sel, score, -jnp.inf)
    sel = _topk_cols(score, min(N_SEL, n_sel), blk) & (score > 0.5 * NEG)
    negsel_ref[...] = jnp.where(sel, 0.0, -jnp.inf)

    def bias_of(m0):
        return lambda h: _bias_block(lambda m: tab_ref[0, m, h], m0)

    def key_rows(ref, kt):
        return ref[0, pl.ds(pl.multiple_of(kt * TK, TK), TK), :]

    def sel_mask(kt):
        rows = [jnp.broadcast_to(negsel_ref[pl.ds((TK // SEL_BLOCK) * kt + a, 1), :], (SEL_BLOCK, TQ))
                for a in range(TK // SEL_BLOCK)]
        negm_ref[...] = jnp.where((kt * TK + krow) <= t_row, jnp.concatenate(rows, axis=0), -jnp.inf)

    _attend_tiles(r_heads, i + 1, q_of, lambda kt: key_rows(kvs_ref, kt), lambda kt: vst_ref[0, 0, kt],
                  lambda kt: bias_of((TQ // KS) * (i - kt)), sel_mask, negm_ref, s_ref, tmax_ref,
                  m_ref, l_ref, acc_ref)
    for h in range(r_heads):
        oslc_ref[h] = normalised(h)

    def win_mask(j):
        kt = i - j
        dist = t_row - (jnp.maximum(kt, 0) * TK + krow)
        ok = (dist >= 0) & (dist < jnp.where(kt >= 0, WINDOW, 0))
        negm_ref[...] = jnp.where(ok, 0.0, -jnp.inf)

    _attend_tiles(r_heads, (WINDOW + TQ) // TK, q_of, lambda j: key_rows(kvw_ref, jnp.maximum(i - j, 0)),
                  lambda j: vwt_ref[0, 0, jnp.maximum(i - j, 0)], lambda j: bias_of((TQ // KS) * j),
                  win_mask, negm_ref, s_ref, tmax_ref, m_ref, l_ref, acc_ref)

    gates = jax.nn.sigmoid(glt_ref[0, 0] + gbt_ref[0])
    for pair in range(r_heads // 2):
        outs = []
        for r in (2 * pair, 2 * pair + 1):
            c = N_BRANCH * r
            outs.append(gates[c:c + 1] * ocmp_ref[r] + gates[c + 1:c + 2] * oslc_ref[r]
                        + gates[c + 2:c + 3] * normalised(r))
        o_ref[0, :, pair * LANE:(pair + 1) * LANE] = jnp.concatenate(outs, axis=0).T.astype(o_ref.dtype)


def _nsa_attention(qt, kvc, vct, kvs, vst, kvw, vwt, glt, gbt, tab, tabc, ovl, n_sel):
    bsz, _, _, t = qt.shape
    g, r = NSA_KV_HEADS, NSA_GROUP
    nkt = t // TK
    once = dict(pipeline_mode=pl.Buffered(1))
    return pl.pallas_call(
        functools.partial(_nsa_kernel, n_sel=n_sel),
        grid=(bsz, g, t // TQ),
        in_specs=[pl.BlockSpec((1, r, LANE, TQ), lambda b, gg, i: (b, gg, 0, i)),
                  pl.BlockSpec((1, 1, CMP_PAD, LANE), lambda b, gg, i: (b, gg, 0, 0)),
                  pl.BlockSpec((1, 1, LANE, CMP_PAD), lambda b, gg, i: (b, gg, 0, 0)),
                  pl.BlockSpec((1, t, LANE), lambda b, gg, i: (b, 0, gg)),
                  pl.BlockSpec((1, 1, nkt, NSA_VT_ROWS, TK), lambda b, gg, i: (b, gg, 0, 0, 0)),
                  pl.BlockSpec((1, t, LANE), lambda b, gg, i: (b, 0, gg)),
                  pl.BlockSpec((1, 1, nkt, NSA_VT_ROWS, TK), lambda b, gg, i: (b, gg, 0, 0, 0)),
                  pl.BlockSpec((1, 1, GATE_ROWS, TQ), lambda b, gg, i: (b, gg, 0, i)),
                  pl.BlockSpec((1, GATE_ROWS, TQ), lambda b, gg, i: (gg, 0, 0)),
                  pl.BlockSpec((1, N_BIAS_TILES, r, KS, LANE), lambda b, gg, i: (gg, 0, 0, 0, 0), **once),
                  pl.BlockSpec((1, r, 2 * CMP_PAD, TQ), lambda b, gg, i: (gg, 0, 0, 0), **once),
                  pl.BlockSpec((LANE, CMP_PAD), lambda b, gg, i: (0, 0))],
        out_specs=pl.BlockSpec((1, TQ, r * HEAD_DIM), lambda b, gg, i: (b, i, gg)),
        out_shape=jax.ShapeDtypeStruct((bsz, t, g * r * HEAD_DIM), BF16),
        scratch_shapes=[pltpu.VMEM((CMP_PAD, TQ), F32), pltpu.VMEM((LANE, TQ), F32),
                        pltpu.VMEM((TK, TQ), F32), pltpu.VMEM((r, TK, TQ), F32),
                        pltpu.VMEM((r, 1, TQ), F32), pltpu.VMEM((r, 1, TQ), F32), pltpu.VMEM((r, 1, TQ), F32),
                        pltpu.VMEM((r, NSA_VT_ROWS, TQ), F32), pltpu.VMEM((r, HEAD_DIM, TQ), F32),
                        pltpu.VMEM((r, HEAD_DIM, TQ), F32)],
        compiler_params=pltpu.CompilerParams(
            dimension_semantics=("arbitrary", "arbitrary", "arbitrary"),
            vmem_limit_bytes=VMEM_LIMIT),
        name="nsa_attention",
    )(qt, kvc, vct, kvs, vst, kvw, vwt, glt, gbt, tab, tabc, ovl)


def _dsa_proj_kernel(x_ref, g_ref, w_ref, qn_ref, kvn_ref, kin_ref, wqu_ref, wuk_ref, wqi_ref,
                     qat_ref, qit_ref, wit_ref, ckv_ref, ckvt_ref, kidx_ref, qmem_ref):
    tm = x_ref.shape[1]
    y = _rms(x_ref[0], g_ref[...]).astype(BF16)
    c_q = _rms(_dot(y, w_ref[:, 0:Q_LORA]), qn_ref[...]).astype(BF16)
    c_kv = _rms(_dot(y, w_ref[:, Q_LORA:Q_LORA + KV_LORA]), kvn_ref[...])
    ckv_ref[0] = c_kv.astype(BF16)
    c_kv_t = c_kv.T.astype(BF16)
    for jt in range(tm // TK):
        ckvt_ref[0, jt] = c_kv_t[:, jt * TK:(jt + 1) * TK]
    off = Q_LORA + KV_LORA
    k_idx = _rms(_dot(y, w_ref[:, off:off + LANE]), kin_ref[...], n=IDX_DIM)
    kidx_ref[0] = k_idx.astype(BF16)
    off += LANE
    w_idx = _dot(y, w_ref[:, off:off + LANE]) * (IDX_HEADS ** -0.5 * IDX_DIM ** -0.5)
    wit_ref[0] = w_idx.T[:IDX_HEADS, :]
    off += LANE
    qmem_ref[0] = _spread_heads(_dot(y, w_ref[:, off:off + N_MEM_HEADS * HEAD_DIM]) * HEAD_DIM ** -0.5
                                ).astype(BF16)
    value_row = lax.broadcasted_iota(I32, (LANE, tm), 0) >= IDX_DIM
    for pair in range(IDX_HEADS // 2):
        acc_t = _dot(c_q, wqi_ref[:, pair * LANE:(pair + 1) * LANE]).T
        qit_ref[0, 2 * pair] = jnp.where(value_row, 0.0, acc_t).astype(BF16)
        qit_ref[0, 2 * pair + 1] = jnp.where(value_row, 0.0, pltpu.roll(acc_t, IDX_DIM, axis=0)).astype(BF16)
    for pair in range(N_MIX_HEADS // 2):
        q_pair = (_dot(c_q, wqu_ref[:, pair * LANE:(pair + 1) * LANE]) * HEAD_DIM ** -0.5).astype(BF16)
        for h in (2 * pair, 2 * pair + 1):
            qat_ref[0, h] = (_dot(q_pair, wuk_ref[h]) * LOG2E).T.astype(BF16)


def _dsa_proj(h, gain, w, qn, kvn, kin, wqu, wuk, wqi, tm):
    bsz, t, d = h.shape
    nkt = t // TK
    full = lambda a: pl.BlockSpec(a.shape, lambda b, i: (0,) * a.ndim)
    row = lambda n: pl.BlockSpec((1, tm, n), lambda b, i: (b, i, 0))
    return pl.pallas_call(
        _dsa_proj_kernel,
        grid=(bsz, t // tm),
        in_specs=[pl.BlockSpec((1, tm, d), lambda b, i: (b, i, 0)), full(gain), full(w), full(qn),
                  full(kvn), full(kin), full(wqu), full(wuk), full(wqi)],
        out_specs=[pl.BlockSpec((1, N_MIX_HEADS, LANE, tm), lambda b, i: (b, 0, 0, i)),
                   pl.BlockSpec((1, IDX_HEADS, LANE, tm), lambda b, i: (b, 0, 0, i)),
                   pl.BlockSpec((1, IDX_HEADS, tm), lambda b, i: (b, 0, i)),
                   row(LANE),
                   pl.BlockSpec((1, tm // TK, DSA_VT_ROWS, TK), lambda b, i: (b, i, 0, 0)),
                   row(LANE), row(N_MEM_HEADS * LANE)],
        out_shape=[jax.ShapeDtypeStruct((bsz, N_MIX_HEADS, LANE, t), BF16),
                   jax.ShapeDtypeStruct((bsz, IDX_HEADS, LANE, t), BF16),
                   jax.ShapeDtypeStruct((bsz, IDX_HEADS, t), F32),
                   jax.ShapeDtypeStruct((bsz, t, LANE), BF16),
                   jax.ShapeDtypeStruct((bsz, nkt, DSA_VT_ROWS, TK), BF16),
                   jax.ShapeDtypeStruct((bsz, t, LANE), BF16),
                   jax.ShapeDtypeStruct((bsz, t, N_MEM_HEADS * LANE), BF16)],
        compiler_params=pltpu.CompilerParams(dimension_semantics=("arbitrary", "arbitrary"),
                                             vmem_limit_bytes=VMEM_LIMIT),
        name="dsa_proj",
    )(h, gain, w, qn, kvn, kin, wqu, wuk, wqi)


def _dsa_kernel(qi_ref, wi_ref, qa_ref, kidx_ref, ckv_ref, ckvt_ref, tab_ref, o_ref,
                sc_ref, sc_hi_ref, negm_ref, s_ref, tmax_ref, m_ref, l_ref, acc_ref, *, topk):
    i = pl.program_id(1)
    qs = i * TQ
    n_tiles = i + 1
    sub = TK // KS
    t_row = qs + lax.broadcasted_iota(I32, (1, TQ), 1)
    krow = lax.broadcasted_iota(I32, (KS, TQ), 0)
    k_f = float(topk)

    grouped = lambda x: x.reshape(KS // SUBLANE, SUBLANE, TQ)

    def score_body(kt2, carry):
        for a in range(2 * sub):
            k0 = pl.multiple_of(kt2 * (2 * TK) + a * KS, KS)
            kk = kidx_ref[0, pl.ds(k0, KS), :]
            sc = jnp.maximum(_dot(kk, qi_ref[0, 0]), 0.0) * wi_ref[0, 0:1, :]
            for h in range(1, IDX_HEADS):
                sc = sc + jnp.maximum(_dot(kk, qi_ref[0, h]), 0.0) * wi_ref[0, h:h + 1, :]
            sc = jnp.where((k0 + krow) <= t_row, sc, NEG)
            sc_ref[2 * sub * kt2 + a] = sc
            sc_hi_ref[2 * sub * kt2 + a] = sc.astype(BF16)
        return carry

    lax.fori_loop(0, (n_tiles + 1) // 2, score_body, 0)

    def count(pred):
        def body(kt, acc):
            for a in range(sub):
                hit = pred(sc_ref[sub * kt + a], kt * TK + a * KS).astype(F32)
                acc = acc + jnp.sum(grouped(hit), axis=0)
            return acc
        acc = lax.fori_loop(0, n_tiles, body, jnp.zeros((SUBLANE, TQ), F32))
        return jnp.sum(acc, axis=0, keepdims=True)

    def count_rounded(cand_f):
        cand = jnp.broadcast_to(cand_f, (BF16_ROWS, TQ)).astype(BF16)
        one, zero = jnp.ones((), BF16), jnp.zeros((), BF16)

        def body(kt, acc):
            for a in range(sub):
                tile = sc_hi_ref[sub * kt + a].reshape(KS // BF16_ROWS, BF16_ROWS, TQ)
                hit = jnp.where(tile >= cand[None], one, zero)
                part = functools.reduce(jnp.add, [hit[r] for r in range(KS // BF16_ROWS)])
                acc = acc + part.astype(F32)
            return acc
        acc = lax.fori_loop(0, n_tiles, body, jnp.zeros((BF16_ROWS, TQ), F32))
        return jnp.sum(acc, axis=0, keepdims=True)

    short = t_row < topk
    half_step = 1 << 15

    def bf16_key_to_float(v16):
        raw16 = jnp.where(v16 >= 0, v16, v16 ^ jnp.int32(0x7FFF))
        return pltpu.bitcast(jnp.left_shift(raw16, 16), F32)

    def high_body(it, v16):
        cand = v16 + jnp.left_shift(jnp.int32(1), 15 - it)
        return jnp.where(count_rounded(bf16_key_to_float(cand)) >= k_f, cand, v16)

    v16 = lax.fori_loop(0, 16, high_body, jnp.full((1, TQ), -(1 << 15), I32))
    key_g = jnp.where(v16 >= 0, jnp.left_shift(v16, 16), jnp.left_shift(v16, 16) | jnp.int32(0xFFFF))

    def low_pass(bit, v, cnt_v):
        cand = v + jnp.left_shift(jnp.int32(1), bit)
        cand_f = _key_to_float(cand)
        cnt = count(lambda sc, k0: sc >= cand_f)
        take = cnt >= k_f
        return jnp.where(take, cand, v), jnp.where(take, cnt, cnt_v)

    def open_count(cnt_v):
        return jnp.sum(((cnt_v != k_f) & ~short).astype(I32))

    passes_per_check = 4
    v, cnt_v = low_pass(16, key_g - half_step, jnp.full((1, TQ), 1e9, F32))

    def v_cond(c):
        return (c[0] >= 0) & (c[3] > 0)

    def v_body(c):
        bit, v, cnt_v, _ = c
        for step in range(passes_per_check):
            v, cnt_v = low_pass(bit - step, v, cnt_v)
        return bit - passes_per_check, v, cnt_v, open_count(cnt_v)

    _, v, cnt_v, open_cols = lax.while_loop(v_cond, v_body, (jnp.int32(15), v, cnt_v, open_count(cnt_v)))
    thr = _key_to_float(v)

    def tie_cut(_):
        need = k_f - count(lambda sc, k0: sc > thr)

        def c_body(it, c):
            cand = c + jnp.left_shift(jnp.int32(1), 13 - it)
            cnt = count(lambda sc, k0: (sc == thr) & ((k0 + krow) < cand))
            return jnp.where(cnt < need, cand, c)

        return lax.fori_loop(0, 14, c_body, jnp.zeros((1, TQ), I32))

    cut = lax.cond(open_cols > 0, tie_cut, lambda _: jnp.full((1, TQ), 2 ** 30, I32), 0)

    def att_mask(kt):
        for a in range(sub):
            sc = sc_ref[sub * kt + a]
            kpos = kt * TK + a * KS + krow
            chosen = short | (sc > thr) | ((sc == thr) & (kpos <= cut))
            negm_ref[a * KS:(a + 1) * KS, :] = jnp.where(chosen & (kpos <= t_row), 0.0, -jnp.inf)

    _attend_tiles(N_MIX_HEADS, n_tiles, lambda h: qa_ref[0, h],
                  lambda kt: ckv_ref[0, pl.ds(pl.multiple_of(kt * TK, TK), TK), :],
                  lambda kt: ckvt_ref[0, kt],
                  lambda kt: (lambda h: _bias_block(lambda m: tab_ref[m, h], (TQ // KS) * (i - kt))),
                  att_mask, negm_ref, s_ref, tmax_ref, m_ref, l_ref, acc_ref)
    for h in range(N_MIX_HEADS):
        out = acc_ref[h] * (1.0 / l_ref[h])
        o_ref[0, :, h * LANE:(h + 1) * LANE] = out.T.astype(o_ref.dtype)


def _dsa_attention(qit, wit, qat, kidx, ckv, ckvt, tab, topk):
    bsz, _, _, t = qat.shape
    nkt = t // TK
    return pl.pallas_call(
        functools.partial(_dsa_kernel, topk=topk),
        grid=(bsz, t // TQ),
        in_specs=[pl.BlockSpec((1, IDX_HEADS, LANE, TQ), lambda b, i: (b, 0, 0, i)),
                  pl.BlockSpec((1, IDX_HEADS, TQ), lambda b, i: (b, 0, i)),
                  pl.BlockSpec((1, N_MIX_HEADS, LANE, TQ), lambda b, i: (b, 0, 0, i)),
                  pl.BlockSpec((1, t, LANE), lambda b, i: (b, 0, 0)),
                  pl.BlockSpec((1, t, LANE), lambda b, i: (b, 0, 0)),
                  pl.BlockSpec((1, nkt, DSA_VT_ROWS, TK), lambda b, i: (b, 0, 0, 0)),
                  pl.BlockSpec(tab.shape, lambda b, i: (0, 0, 0, 0), pipeline_mode=pl.Buffered(1))],
        out_specs=pl.BlockSpec((1, TQ, N_MIX_HEADS * LANE), lambda b, i: (b, i, 0)),
        out_shape=jax.ShapeDtypeStruct((bsz, t, N_MIX_HEADS * LANE), BF16),
        scratch_shapes=[pltpu.VMEM((t // KS, KS, TQ), F32), pltpu.VMEM((t // KS, KS, TQ), BF16),
                        pltpu.VMEM((TK, TQ), F32),
                        pltpu.VMEM((N_MIX_HEADS, TK, TQ), F32), pltpu.VMEM((N_MIX_HEADS, 1, TQ), F32),
                        pltpu.VMEM((N_MIX_HEADS, 1, TQ), F32), pltpu.VMEM((N_MIX_HEADS, 1, TQ), F32),
                        pltpu.VMEM((N_MIX_HEADS, DSA_VT_ROWS, TQ), F32)],
        compiler_params=pltpu.CompilerParams(dimension_semantics=("arbitrary", "arbitrary"),
                                             vmem_limit_bytes=VMEM_LIMIT),
        name="dsa_attention",
    )(qit, wit, qat, kidx, ckv, ckvt, tab)


def _tail_kernel(h_ref, mix_ref, qmem_ref, kvm_ref, *rest, has_uv, final_norm):
    if has_uv:
        wuv_ref, *rest = rest
    wmix_ref, wmem_ref, g_ref, wg_ref, wu_ref, wd_ref, gf_ref, o_ref = rest
    mix = mix_ref[0]
    if has_uv:
        mix = _dot(mix, wuv_ref[...]).astype(BF16)
    upd = _dot(mix, wmix_ref[...])
    qm = qmem_ref[0]
    for hm in range(N_MEM_HEADS):
        sl = slice(hm * LANE, (hm + 1) * LANE)
        kv = kvm_ref[0, :, sl]
        s = _dot_nt(qm[:, sl], kv)
        e = jnp.exp(s - jnp.max(s, axis=-1, keepdims=True))
        p = e / jnp.sum(e, axis=-1, keepdims=True)
        o_h = _dot(p.astype(BF16), kv).astype(BF16)
        upd = upd + _dot(o_h, wmem_ref[sl, :])
    h = h_ref[0] + upd
    hn = _rms(h, g_ref[...]).astype(BF16)
    act = (jax.nn.silu(_dot(hn, wg_ref[...])) * _dot(hn, wu_ref[...])).astype(BF16)
    out = h + _dot(act, wd_ref[...])
    if final_norm:
        out = _rms(out, gf_ref[...])
    o_ref[0] = out


def _layer_tail(h, mix, qmem, kvm, w_uv, w_mix, w_mem, gain, wg, wu, wd, gain_final, final_norm, tm):
    bsz, t, d = h.shape
    has_uv = w_uv is not None
    const = lambda a: pl.BlockSpec(a.shape, lambda b, i: (0,) * a.ndim, pipeline_mode=pl.Buffered(1))
    weights = ([w_uv] if has_uv else []) + [w_mix, w_mem, gain, wg, wu, wd, gain_final]
    return pl.pallas_call(
        functools.partial(_tail_kernel, has_uv=has_uv, final_norm=final_norm),
        grid=(bsz, t // tm),
        in_specs=[pl.BlockSpec((1, tm, d), lambda b, i: (b, i, 0)),
                  pl.BlockSpec((1, tm, mix.shape[2]), lambda b, i: (b, i, 0)),
                  pl.BlockSpec((1, tm, qmem.shape[2]), lambda b, i: (b, i, 0)),
                  pl.BlockSpec((1,) + kvm.shape[1:], lambda b, i: (b, 0, 0))]
                 + [const(w) for w in weights],
        out_specs=pl.BlockSpec((1, tm, d), lambda b, i: (b, i, 0)),
        out_shape=jax.ShapeDtypeStruct((bsz, t, d), F32),
        compiler_params=pltpu.CompilerParams(dimension_semantics=("arbitrary", "arbitrary"),
                                             vmem_limit_bytes=VMEM_LIMIT),
        name="layer_tail",
    )(h, mix, qmem, kvm, *weights)


def _pad_cols(w, n):
    return jnp.pad(w, ((0, 0), (0, n - w.shape[1])))


def _value_rows(w_rows, n_heads):
    d_out = w_rows.shape[1]
    w = w_rows.reshape(n_heads, HEAD_DIM, d_out)
    return jnp.pad(w, ((0, 0), (LANE - HEAD_DIM, 0), (0, 0))).reshape(n_heads * LANE, d_out)


def _interleave_kv(k, v, n_heads):
    d_in = k.shape[0]
    kv = jnp.concatenate([k.reshape(d_in, n_heads, HEAD_DIM), v.reshape(d_in, n_heads, HEAD_DIM)], axis=2)
    return kv.reshape(d_in, n_heads * LANE)


def _bucket_bias(rel_bias, bucket_np):
    onehot = jax.nn.one_hot(jnp.asarray(bucket_np.reshape(-1)), N_REL_BUCKETS, dtype=F32)
    out = jnp.dot(onehot, rel_bias, precision=lax.Precision.HIGHEST) * LOG2E
    return out.reshape(bucket_np.shape + (rel_bias.shape[1],))


def _bias_tiles(rel_bias):
    m = np.arange(N_BIAS_TILES)[:, None, None]
    dist = LANE * m + np.arange(LANE)[None, None, :] - np.arange(KS)[None, :, None]
    return jnp.transpose(_bucket_bias(rel_bias, _rel_bucket_np(dist)), (0, 3, 1, 2))


def _bias_cmp_table(rel_bias):
    rel = np.arange(2 * CMP_PAD) - CMP_PAD
    dist = np.arange(TQ)[None, :] - CMP_STRIDE * rel[:, None] - (CMP_LEN - 1)
    return jnp.transpose(_bucket_bias(rel_bias, _rel_bucket_np(dist)), (2, 0, 1))


def _overlap_matrix(t):
    n_cmp = (t - CMP_LEN) // CMP_STRIDE + 1
    n_sel = t // SEL_BLOCK
    cs = np.arange(CMP_PAD) * CMP_STRIDE
    ss = np.arange(LANE) * SEL_BLOCK
    ov = (cs[None, :] <= ss[:, None] + SEL_BLOCK - 1) & (cs[None, :] + CMP_LEN - 1 >= ss[:, None])
    ov &= (np.arange(CMP_PAD) < n_cmp)[None, :] & (np.arange(LANE) < n_sel)[:, None]
    return jnp.asarray(ov, BF16)


def kernel(x, mem, rel_bias, norm_mix, norm_ffn, norm_mem, w_mem_kv, w_out, ffn_gate, ffn_up, ffn_down,
           nsa_w_in, nsa_gate_b, nsa_cmp_pos_k, nsa_cmp_pos_v,
           nsa_cmp_k_w1, nsa_cmp_k_b1, nsa_cmp_k_w2, nsa_cmp_k_b2,
           nsa_cmp_v_w1, nsa_cmp_v_b1, nsa_cmp_v_w2, nsa_cmp_v_b2,
           dsa_w_in, dsa_q_norm, dsa_kv_norm, dsa_w_q_up, dsa_w_uk, dsa_w_uv, dsa_w_q_idx, dsa_kidx_norm,
           norm_final):
    bsz, t, d = x.shape
    m_len = mem.shape[1]
    depth = norm_mix.shape[0]
    g, r, hh = NSA_KV_HEADS, NSA_GROUP, N_MIX_HEADS
    d_mix = hh * HEAD_DIM
    kvw_ = g * HEAD_DIM
    assert t % 1024 == 0 and t // SEL_BLOCK <= LANE and t // CMP_STRIDE <= CMP_PAD
    tm = 512
    h = x.astype(F32)

    tab = _bias_tiles(rel_bias)
    tab_nsa = jnp.transpose(tab.reshape(N_BIAS_TILES, g, r, KS, LANE), (1, 0, 2, 3, 4))
    tabc = _bias_cmp_table(rel_bias).reshape(g, r, 2 * CMP_PAD, TQ)
    ovl = _overlap_matrix(t)

    for layer in range(depth):
        j = layer // 2
        wkv = w_mem_kv[layer]
        wkv = _interleave_kv(wkv[:, :N_MEM_HEADS * HEAD_DIM], wkv[:, N_MEM_HEADS * HEAD_DIM:], N_MEM_HEADS)
        kvm = _rms_proj(mem.reshape(bsz * m_len, d), norm_mem[layer], wkv.astype(BF16), BF16, m_len)
        kvm = kvm.reshape(bsz, m_len, N_MEM_HEADS * LANE)
        w_o = w_out[layer]
        w_mem_o = _value_rows(w_o[d_mix:], N_MEM_HEADS).astype(BF16)
        w_mix = w_o[:d_mix].astype(BF16)

        if layer % 2 == 0:
            w = nsa_w_in[j]
            c = np.cumsum([0, d_mix, kvw_, kvw_, kvw_, kvw_, kvw_, kvw_, hh * N_BRANCH,
                           N_MEM_HEADS * HEAD_DIM])
            wq, wkc, wvc, wks, wvs, wkw, wvw, wgl, wqm = [w[:, c[k]:c[k + 1]] for k in range(9)]
            n_g = r * N_BRANCH
            wgl = jnp.concatenate([_pad_cols(wgl[:, gg * n_g:(gg + 1) * n_g], LANE) for gg in range(g)], 1)
            w_all = jnp.concatenate([wq, wkc, wvc, _interleave_kv(wks, wvs, g),
                                     _interleave_kv(wkw, wvw, g), wgl, wqm], axis=1)
            qt, cmp_raw, kvs, vst, kvw, vwt, glt, qmem = _nsa_proj(h, norm_mix[layer], w_all.astype(BF16), tm)
            nc = t // CMP_STRIDE
            x2 = jnp.transpose(cmp_raw.reshape(bsz, t, 2 * g, HEAD_DIM), (0, 2, 1, 3))
            x2 = x2.reshape(bsz, 2 * g, nc, CMP_STRIDE * HEAD_DIM)
            pos = jnp.stack([nsa_cmp_pos_k[j], nsa_cmp_pos_v[j]]).reshape(2, 2, CMP_STRIDE * HEAD_DIM)
            w1 = jnp.stack([nsa_cmp_k_w1[j], nsa_cmp_v_w1[j]]).astype(BF16)
            b1 = jnp.stack([nsa_cmp_k_b1[j], nsa_cmp_v_b1[j]])[:, None, :]
            w2 = jnp.stack([jnp.pad(nsa_cmp_k_w2[j], ((0, 0), (0, HEAD_DIM))),
                            jnp.pad(nsa_cmp_v_w2[j], ((0, 0), (HEAD_DIM, 0)))]).astype(BF16)
            b2 = jnp.stack([jnp.pad(nsa_cmp_k_b2[j], (0, HEAD_DIM)),
                            jnp.pad(nsa_cmp_v_b2[j], (HEAD_DIM, 0))])[:, None, :]
            kvc = _compress(x2, pos, w1, b1, w2, b2)
            kvc = jnp.pad(kvc, ((0, 0), (0, 0), (0, CMP_PAD - nc), (0, 0)))
            vct = jnp.transpose(kvc, (0, 1, 3, 2))
            gb = jnp.pad(nsa_gate_b[j].reshape(g, n_g), ((0, 0), (0, GATE_ROWS - n_g)))
            gbt = jnp.broadcast_to(gb[:, :, None], (g, GATE_ROWS, TQ))
            mix = _nsa_attention(qt, kvc, vct, kvs, vst, kvw, vwt, glt, gbt, tab_nsa, tabc, ovl,
                                 t // SEL_BLOCK)
            w_uv = None
        else:
            w = dsa_w_in[j]
            c = np.cumsum([0, Q_LORA, KV_LORA, IDX_DIM, IDX_HEADS, N_MEM_HEADS * HEAD_DIM])
            wcq, wckv, wki, wwi, wqm = [w[:, c[k]:c[k + 1]] for k in range(5)]
            w_all = jnp.concatenate([wcq, wckv, _pad_cols(wki, LANE), _pad_cols(wwi, LANE),
                                     wqm], axis=1).astype(BF16)
            wuk = jnp.transpose(dsa_w_uk[j], (1, 2, 0))
            wuk = jnp.stack([jnp.pad(wuk[hd], ((HEAD_DIM * (hd % 2), HEAD_DIM * (1 - hd % 2)), (0, 0)))
                             for hd in range(hh)]).astype(BF16)
            qat, qit, wit, ckv, ckvt, kidx, qmem = _dsa_proj(
                h, norm_mix[layer][None], w_all, dsa_q_norm[j][None], dsa_kv_norm[j][None],
                _pad_cols(dsa_kidx_norm[j][None], LANE), dsa_w_q_up[j].astype(BF16), wuk,
                dsa_w_q_idx[j].astype(BF16), tm)
            mix = _dsa_attention(qit, wit, qat, kidx, ckv, ckvt, tab, min(DSA_TOPK, t // 4))
            wv = jnp.transpose(dsa_w_uv[j], (1, 0, 2))
            w_uv = (jnp.eye(hh, dtype=F32)[:, None, :, None] * wv[:, :, None, :]
                    ).reshape(hh * KV_LORA, d_mix).astype(BF16)

        h = _layer_tail(h, mix, qmem, kvm, w_uv, w_mix, w_mem_o, norm_ffn[layer][None],
                        ffn_gate[layer].astype(BF16), ffn_up[layer].astype(BF16),
                        ffn_down[layer].astype(BF16), norm_final[None], layer == depth - 1, tm)
    return h.astype(x.dtype)
```

```python
import functools
import math

import numpy as np
import jax
import jax.numpy as jnp
from jax import lax
from jax.experimental import pallas as pl
from jax.experimental.pallas import tpu as pltpu

F32 = jnp.float32
BF16 = jnp.bfloat16
I32 = jnp.int32

NEG = -1e30
EPS = 1e-6
LOG2E = math.log2(math.e)
LANE = 128
SUBLANE = 8
BF16_ROWS = 16
HEAD_DIM = 64
N_MIX_HEADS = 12
N_MEM_HEADS = 4
N_REL_BUCKETS = 32
REL_MAX_EXACT = 16
REL_MAX_DIST = 2048
NSA_KV_HEADS = 2
NSA_GROUP = N_MIX_HEADS // NSA_KV_HEADS
N_BRANCH = 3
GATE_ROWS = 24
CMP_LEN = 32
CMP_STRIDE = 16
SEL_BLOCK = 64
N_SEL = 16
WINDOW = 512
FORCE_BONUS = 1e4
Q_LORA = 256
KV_LORA = 128
IDX_HEADS = 8
IDX_DIM = 64
DSA_TOPK = 256
TQ = 256
TK = 256
SCORE_TK = 512
KS = LANE
N_BIAS_TILES = REL_MAX_DIST // KS + 2
CMP_PAD = 512
NSA_VT_ROWS = HEAD_DIM
DSA_VT_ROWS = KV_LORA
VMEM_LIMIT = 56 * 1024 * 1024
INT_MIN = -2 ** 31
F32_MIN = float(np.finfo(np.float32).min)


def _dot(a, b):
    return jnp.dot(a, b, preferred_element_type=F32)


def _dot_nt(a, b):
    return lax.dot_general(a, b, (((1,), (1,)), ((), ())), preferred_element_type=F32)


def _rms(x, gain, n=None):
    n = x.shape[-1] if n is None else n
    ms = jnp.sum(x * x, axis=-1, keepdims=True) * (1.0 / n)
    return x * lax.rsqrt(ms + EPS) * gain


def _rel_bucket_np(dist):
    n = np.maximum(dist, 0)
    nf = np.maximum(n, REL_MAX_EXACT).astype(np.float32)
    large = REL_MAX_EXACT + (np.log(nf / np.float32(REL_MAX_EXACT))
                             / np.float32(math.log(REL_MAX_DIST / REL_MAX_EXACT))
                             * np.float32(N_REL_BUCKETS - REL_MAX_EXACT)).astype(np.int32)
    large = np.minimum(large, N_REL_BUCKETS - 1)
    return np.where(n < REL_MAX_EXACT, n, large).astype(np.int32)


def _spread_heads(x):
    rows, n = x.shape
    low = lax.broadcasted_iota(I32, (rows, LANE), 1) < HEAD_DIM
    out = []
    for j in range(n // LANE):
        pair = x[:, j * LANE:(j + 1) * LANE]
        out += [jnp.where(low, pair, 0.0), jnp.where(low, pltpu.roll(pair, HEAD_DIM, axis=1), 0.0)]
    return jnp.concatenate(out, axis=1)


def _colsum(x):
    rows, n = x.shape
    return jnp.sum(jnp.sum(x.reshape(rows // SUBLANE, SUBLANE, n), axis=0), axis=0, keepdims=True)


def _key_to_float(v):
    bits = jnp.where(v >= 0, v, v ^ jnp.int32(0x7FFFFFFF))
    return pltpu.bitcast(bits, F32)


def _topk_cols(score, k, row_idx):
    n = score.shape[1]

    def vbody(it, v):
        cand = v + jnp.left_shift(jnp.int32(1), 31 - it)
        cnt = _colsum((score >= _key_to_float(cand)).astype(F32))
        return jnp.where(cnt >= k, cand, v)

    v = lax.fori_loop(0, 32, vbody, jnp.full((1, n), INT_MIN, I32))
    thr = _key_to_float(v)
    gt = score > thr
    eq = score == thr
    need = k - _colsum(gt.astype(F32))

    def cbody(it, c):
        cand = c + jnp.left_shift(jnp.int32(1), 6 - it)
        cnt = _colsum((eq & (row_idx < cand)).astype(F32))
        return jnp.where(cnt < need, cand, c)

    c = lax.fori_loop(0, 7, cbody, jnp.zeros((1, n), I32))
    return gt | (eq & (row_idx <= c))


def _bias_block(tile_of, m0):
    idx = lambda m: jnp.clip(m, 0, N_BIAS_TILES - 1)
    return [jnp.concatenate([tile_of(idx(m0 + b - a)) for b in range(TQ // LANE)], axis=1)
            for a in range(TK // KS)]


def _attend_tiles(n_heads, n_tiles, q_of, kv_of, vt_of, bias_of, fill_mask, negm_ref, s_ref, tmax_ref,
                  m_ref, l_ref, acc_ref):
    parts = [slice(a * KS, (a + 1) * KS) for a in range(TK // KS)]
    m_ref[...] = jnp.full(m_ref.shape, F32_MIN, F32)
    l_ref[...] = jnp.zeros(l_ref.shape, F32)
    acc_ref[...] = jnp.zeros(acc_ref.shape, F32)

    def logits_of(j):
        fill_mask(j)
        kv = kv_of(j)
        bias_h = bias_of(j)

        def run(h):
            bias = bias_h(h)
            tmax = None
            for a, sl in enumerate(parts):
                s_a = _dot(kv[sl], q_of(h)) + bias[a] + negm_ref[sl, :]
                s_ref[h, sl, :] = s_a
                mx = jnp.max(s_a, axis=0, keepdims=True)
                tmax = mx if tmax is None else jnp.maximum(tmax, mx)
            tmax_ref[h] = tmax
        return run

    def accumulate(j):
        vt = vt_of(j)

        def run(h):
            m_old = m_ref[h]
            m_new = jnp.maximum(m_old, tmax_ref[h])
            alpha = jnp.exp2(m_old - m_new)
            m_ref[h] = m_new
            p = [jnp.exp2(s_ref[h, sl, :] - m_new) for sl in parts]
            l_ref[h] = alpha * l_ref[h] + _colsum(sum(p[1:], p[0]))
            acc = alpha * acc_ref[h]
            for sl, p_a in zip(parts, p):
                acc = acc + _dot(vt[:, sl], p_a.astype(BF16))
            acc_ref[h] = acc
        return run

    first = logits_of(0)
    for h in range(n_heads):
        first(h)

    def body(j, carry):
        consume = accumulate(j)
        produce = logits_of(jnp.minimum(j + 1, n_tiles - 1))
        for h in range(n_heads):
            consume(h)
            produce(h)
        return carry

    lax.fori_loop(0, n_tiles, body, 0)


def _rms_proj_kernel(x_ref, g_ref, w_ref, o_ref):
    y = _rms(x_ref[...], g_ref[...]).astype(BF16)
    o_ref[...] = _dot(y, w_ref[...]).astype(o_ref.dtype)


def _rms_proj(x2d, gain, w_bf16, out_dtype, tm):
    m, d = x2d.shape
    n = w_bf16.shape[1]
    return pl.pallas_call(
        _rms_proj_kernel,
        grid=(m // tm,),
        in_specs=[pl.BlockSpec((tm, d), lambda i: (i, 0)),
                  pl.BlockSpec((1, d), lambda i: (0, 0)),
                  pl.BlockSpec((d, n), lambda i: (0, 0))],
        out_specs=pl.BlockSpec((tm, n), lambda i: (i, 0)),
        out_shape=jax.ShapeDtypeStruct((m, n), out_dtype),
        compiler_params=pltpu.CompilerParams(dimension_semantics=("arbitrary",),
                                             vmem_limit_bytes=VMEM_LIMIT),
        name="rms_proj",
    )(x2d, gain.reshape(1, d), w_bf16)


def _nsa_proj_kernel(x_ref, g_ref, w_ref, qt_ref, cmp_ref, kvs_ref, vst_ref, kvw_ref, vwt_ref,
                     glt_ref, qmem_ref):
    tm = x_ref.shape[1]
    y = _rms(x_ref[0], g_ref[...]).astype(BF16)
    value_row = lax.broadcasted_iota(I32, (LANE, tm), 0) >= HEAD_DIM
    off = 0
    for pair in range(N_MIX_HEADS // 2):
        acc_t = (_dot(y, w_ref[:, off:off + LANE]) * (HEAD_DIM ** -0.5 * LOG2E)).T
        qt_ref[0, 2 * pair] = jnp.where(value_row, 0.0, acc_t).astype(BF16)
        qt_ref[0, 2 * pair + 1] = jnp.where(value_row, 0.0, pltpu.roll(acc_t, HEAD_DIM, axis=0)).astype(BF16)
        off += LANE
    raw = _dot(y, w_ref[:, off:off + 2 * LANE])
    for piece in range(2 * NSA_KV_HEADS):
        cmp_ref[0, piece] = raw[:, piece * HEAD_DIM:(piece + 1) * HEAD_DIM]
    off += 2 * LANE
    for kv_ref, vt_ref in ((kvs_ref, vst_ref), (kvw_ref, vwt_ref)):
        for g in range(NSA_KV_HEADS):
            acc = _dot(y, w_ref[:, off:off + LANE])
            off += LANE
            kv_ref[0, :, g * LANE:(g + 1) * LANE] = acc.astype(BF16)
            v_t = acc.T[HEAD_DIM:, :].astype(BF16)
            for jt in range(tm // TK):
                vt_ref[0, g, jt] = v_t[:, jt * TK:(jt + 1) * TK]
    for g in range(NSA_KV_HEADS):
        acc = _dot(y, w_ref[:, off:off + LANE])
        off += LANE
        glt_ref[0, g] = acc.T[:GATE_ROWS, :]
    qmem_ref[0] = _spread_heads(_dot(y, w_ref[:, off:off + N_MEM_HEADS * HEAD_DIM]) * HEAD_DIM ** -0.5
                                ).astype(BF16)


def _nsa_proj(h, gain, w, tm):
    bsz, t, d = h.shape
    g = NSA_KV_HEADS
    nkt = t // TK
    row = lambda n: pl.BlockSpec((1, tm, n), lambda b, i: (b, i, 0))
    vt_spec = pl.BlockSpec((1, g, tm // TK, NSA_VT_ROWS, TK), lambda b, i: (b, 0, i, 0, 0))
    return pl.pallas_call(
        _nsa_proj_kernel,
        grid=(bsz, t // tm),
        in_specs=[pl.BlockSpec((1, tm, d), lambda b, i: (b, i, 0)),
                  pl.BlockSpec((1, d), lambda b, i: (0, 0)),
                  pl.BlockSpec(w.shape, lambda b, i: (0, 0))],
        out_specs=[pl.BlockSpec((1, N_MIX_HEADS, LANE, tm), lambda b, i: (b, 0, 0, i)),
                   pl.BlockSpec((1, 2 * g, tm, HEAD_DIM), lambda b, i: (b, 0, i, 0)),
                   row(g * LANE), vt_spec, row(g * LANE), vt_spec,
                   pl.BlockSpec((1, g, GATE_ROWS, tm), lambda b, i: (b, 0, 0, i)),
                   row(N_MEM_HEADS * LANE)],
        out_shape=[jax.ShapeDtypeStruct((bsz, N_MIX_HEADS, LANE, t), BF16),
                   jax.ShapeDtypeStruct((bsz, 2 * g, t, HEAD_DIM), F32),
                   jax.ShapeDtypeStruct((bsz, t, g * LANE), BF16),
                   jax.ShapeDtypeStruct((bsz, g, nkt, NSA_VT_ROWS, TK), BF16),
                   jax.ShapeDtypeStruct((bsz, t, g * LANE), BF16),
                   jax.ShapeDtypeStruct((bsz, g, nkt, NSA_VT_ROWS, TK), BF16),
                   jax.ShapeDtypeStruct((bsz, g, GATE_ROWS, t), F32),
                   jax.ShapeDtypeStruct((bsz, t, N_MEM_HEADS * LANE), BF16)],
        compiler_params=pltpu.CompilerParams(dimension_semantics=("arbitrary", "arbitrary"),
                                             vmem_limit_bytes=VMEM_LIMIT),
        name="nsa_proj",
    )(h, gain.reshape(1, d), w)


def _compress_kernel(xk_ref, xv_ref, pos_ref, w1_ref, b1_ref, w2_ref, b2_ref, o_ref):
    nc = xk_ref.shape[2]
    half = xk_ref.shape[3]
    out = None
    for j, x_ref in enumerate((xk_ref, xv_ref)):
        x = x_ref[0, 0]
        top = _dot((x + pos_ref[j, 0:1, :]).astype(BF16), w1_ref[j, :half, :])
        bot = _dot((x + pos_ref[j, 1:2, :]).astype(BF16), w1_ref[j, half:, :])
        pre = top + pltpu.roll(bot, nc - 1, axis=0) + b1_ref[j]
        hid = jax.nn.gelu(pre)
        res = _dot(hid.astype(BF16), w2_ref[j]) + b2_ref[j]
        out = res if out is None else out + res
    o_ref[0, 0] = out.astype(o_ref.dtype)


def _compress(x2, pos, w1, b1, w2, b2):
    bsz, _, nc, width = x2.shape
    g = NSA_KV_HEADS
    return pl.pallas_call(
        _compress_kernel,
        grid=(bsz, g),
        in_specs=[pl.BlockSpec((1, 1, nc, width), lambda b, gg: (b, gg, 0, 0)),
                  pl.BlockSpec((1, 1, nc, width), lambda b, gg: (b, gg + NSA_KV_HEADS, 0, 0)),
                  pl.BlockSpec(pos.shape, lambda b, gg: (0, 0, 0)),
                  pl.BlockSpec(w1.shape, lambda b, gg: (0, 0, 0)),
                  pl.BlockSpec(b1.shape, lambda b, gg: (0, 0, 0)),
                  pl.BlockSpec(w2.shape, lambda b, gg: (0, 0, 0)),
                  pl.BlockSpec(b2.shape, lambda b, gg: (0, 0, 0))],
        out_specs=pl.BlockSpec((1, 1, nc, LANE), lambda b, gg: (b, gg, 0, 0)),
        out_shape=jax.ShapeDtypeStruct((bsz, g, nc, LANE), BF16),
        compiler_params=pltpu.CompilerParams(dimension_semantics=("arbitrary", "arbitrary"),
                                             vmem_limit_bytes=VMEM_LIMIT),
        name="nsa_compress",
    )(x2, x2, pos, w1, b1, w2, b2)


def _nsa_kernel(q_ref, kvc_ref, vct_ref, kvs_ref, vst_ref, kvw_ref, vwt_ref, glt_ref, gbt_ref,
                tab_ref, tabc_ref, ovl_ref, o_ref,
                psum_ref, negsel_ref, negm_ref, s_ref, tmax_ref, m_ref, l_ref, acc_ref, ocmp_ref, oslc_ref,
                *, n_sel):
    r_heads = NSA_GROUP
    i = pl.program_id(2)
    qs = i * TQ
    t_row = qs + lax.broadcasted_iota(I32, (1, TQ), 1)
    krow = lax.broadcasted_iota(I32, (TK, TQ), 0)
    q_of = lambda h: q_ref[0, h]

    def normalised(h):
        return acc_ref[h] * (1.0 / l_ref[h])

    kvc = kvc_ref[0, 0]
    vct = vct_ref[0, 0]
    n_row = lax.broadcasted_iota(I32, (CMP_PAD, TQ), 0)
    negc = jnp.where((CMP_STRIDE * n_row + (CMP_LEN - 1)) <= t_row, 0.0, -jnp.inf)
    j0 = pl.multiple_of(CMP_PAD - (TQ // CMP_STRIDE) * i, SUBLANE)
    psum_ref[...] = jnp.zeros(psum_ref.shape, F32)

    cparts = [slice(a * KS, (a + 1) * KS) for a in range(CMP_PAD // KS)]
    for h in range(r_heads):
        s = [_dot(kvc[sl], q_of(h)) + tabc_ref[0, h, pl.ds(j0 + sl.start, KS), :] + negc[sl] for sl in cparts]
        m = functools.reduce(jnp.maximum, [jnp.max(s_a, axis=0, keepdims=True) for s_a in s])
        m = jnp.maximum(m, F32_MIN)
        e = [jnp.exp2(s_a - m) for s_a in s]
        den = _colsum(functools.reduce(jnp.add, e))
        inv = 1.0 / jnp.where(den > 0.0, den, 1.0)
        acc = functools.reduce(jnp.add, [_dot(vct[:, sl], e_a.astype(BF16)) for sl, e_a in zip(cparts, e)])
        ocmp_ref[h] = acc[LANE - HEAD_DIM:] * inv
        for sl, e_a in zip(cparts, e):
            psum_ref[sl, :] += e_a * inv

    psum = psum_ref[...]
    ovl = ovl_ref[...]
    hi = psum.astype(BF16)
    rem1 = psum - hi.astype(F32)
    mid = rem1.astype(BF16)
    lo = (rem1 - mid.astype(F32)).astype(BF16)
    p_slc = _dot(ovl, hi) + _dot(ovl, mid) + _dot(ovl, lo)
    blk = lax.broadcasted_iota(I32, (LANE, TQ), 0)
    cur = jnp.right_shift(t_row, 6)
    forced = (blk == 0) | (blk == cur) | (blk == cur - 1)
    admissible = (blk * SEL_BLOCK) <= t_row
    score = jnp.where(admissible, p_slc + jnp.where(forced, FORCE_BONUS, 0.0), NEG)
    score = jnp.where(blk < n_sel, score, -jnp.inf)
    sel = _topk_cols(score, min(N_SEL, n_sel), blk) & (score > 0.5 * NEG)
    negsel_ref[...] = jnp.where(sel, 0.0, -jnp.inf)

    def bias_of(kt):
        return lambda h: _bias_block(lambda m: tab_ref[0, m, h], (TQ // KS) * i - (TK // KS) * kt)

    def key_rows(ref, kt):
        return ref[0, pl.ds(pl.multiple_of(kt * TK, TK), TK), :]

    last_tile = (qs + TQ - 1) // TK

    def sel_mask(kt):
        rows = [jnp.broadcast_to(negsel_ref[pl.ds((TK // SEL_BLOCK) * kt + a, 1), :], (SEL_BLOCK, TQ))
                for a in range(TK // SEL_BLOCK)]
        negm_ref[...] = jnp.where((kt * TK + krow) <= t_row, jnp.concatenate(rows, axis=0), -jnp.inf)

    _attend_tiles(r_heads, last_tile + 1, q_of, lambda kt: key_rows(kvs_ref, kt), lambda kt: vst_ref[0, 0, kt],
                  bias_of, sel_mask, negm_ref, s_ref, tmax_ref, m_ref, l_ref, acc_ref)
    for h in range(r_heads):
        oslc_ref[h] = normalised(h)

    n_win = max((TQ * e + TQ - 1) // TK - (TQ * e - WINDOW + 1) // TK + 1 for e in range(max(TK // TQ, 1)))

    def win_mask(j):
        kt = last_tile - j
        dist = t_row - (jnp.maximum(kt, 0) * TK + krow)
        ok = (dist >= 0) & (dist < jnp.where(kt >= 0, WINDOW, 0))
        negm_ref[...] = jnp.where(ok, 0.0, -jnp.inf)

    _attend_tiles(r_heads, n_win, q_of, lambda j: key_rows(kvw_ref, jnp.maximum(last_tile - j, 0)),
                  lambda j: vwt_ref[0, 0, jnp.maximum(last_tile - j, 0)], lambda j: bias_of(last_tile - j),
                  win_mask, negm_ref, s_ref, tmax_ref, m_ref, l_ref, acc_ref)

    gates = jax.nn.sigmoid(glt_ref[0, 0] + gbt_ref[0])
    for pair in range(r_heads // 2):
        outs = []
        for r in (2 * pair, 2 * pair + 1):
            c = N_BRANCH * r
            outs.append(gates[c:c + 1] * ocmp_ref[r] + gates[c + 1:c + 2] * oslc_ref[r]
                        + gates[c + 2:c + 3] * normalised(r))
        o_ref[0, :, pair * LANE:(pair + 1) * LANE] = jnp.concatenate(outs, axis=0).T.astype(o_ref.dtype)


def _nsa_attention(qt, kvc, vct, kvs, vst, kvw, vwt, glt, gbt, tab, tabc, ovl, n_sel):
    bsz, _, _, t = qt.shape
    g, r = NSA_KV_HEADS, NSA_GROUP
    nkt = t // TK
    once = dict(pipeline_mode=pl.Buffered(1))
    return pl.pallas_call(
        functools.partial(_nsa_kernel, n_sel=n_sel),
        grid=(bsz, g, t // TQ),
        in_specs=[pl.BlockSpec((1, r, LANE, TQ), lambda b, gg, i: (b, gg, 0, i)),
                  pl.BlockSpec((1, 1, CMP_PAD, LANE), lambda b, gg, i: (b, gg, 0, 0)),
                  pl.BlockSpec((1, 1, LANE, CMP_PAD), lambda b, gg, i: (b, gg, 0, 0)),
                  pl.BlockSpec((1, t, LANE), lambda b, gg, i: (b, 0, gg)),
                  pl.BlockSpec((1, 1, nkt, NSA_VT_ROWS, TK), lambda b, gg, i: (b, gg, 0, 0, 0)),
                  pl.BlockSpec((1, t, LANE), lambda b, gg, i: (b, 0, gg)),
                  pl.BlockSpec((1, 1, nkt, NSA_VT_ROWS, TK), lambda b, gg, i: (b, gg, 0, 0, 0)),
                  pl.BlockSpec((1, 1, GATE_ROWS, TQ), lambda b, gg, i: (b, gg, 0, i)),
                  pl.BlockSpec((1, GATE_ROWS, TQ), lambda b, gg, i: (gg, 0, 0)),
                  pl.BlockSpec((1, N_BIAS_TILES, r, KS, LANE), lambda b, gg, i: (gg, 0, 0, 0, 0), **once),
                  pl.BlockSpec((1, r, 2 * CMP_PAD, TQ), lambda b, gg, i: (gg, 0, 0, 0), **once),
                  pl.BlockSpec((LANE, CMP_PAD), lambda b, gg, i: (0, 0))],
        out_specs=pl.BlockSpec((1, TQ, r * HEAD_DIM), lambda b, gg, i: (b, i, gg)),
        out_shape=jax.ShapeDtypeStruct((bsz, t, g * r * HEAD_DIM), BF16),
        scratch_shapes=[pltpu.VMEM((CMP_PAD, TQ), F32), pltpu.VMEM((LANE, TQ), F32),
                        pltpu.VMEM((TK, TQ), F32), pltpu.VMEM((r, TK, TQ), F32),
                        pltpu.VMEM((r, 1, TQ), F32), pltpu.VMEM((r, 1, TQ), F32), pltpu.VMEM((r, 1, TQ), F32),
                        pltpu.VMEM((r, NSA_VT_ROWS, TQ), F32), pltpu.VMEM((r, HEAD_DIM, TQ), F32),
                        pltpu.VMEM((r, HEAD_DIM, TQ), F32)],
        compiler_params=pltpu.CompilerParams(
            dimension_semantics=("arbitrary", "arbitrary", "arbitrary"),
            vmem_limit_bytes=VMEM_LIMIT),
        name="nsa_attention",
    )(qt, kvc, vct, kvs, vst, kvw, vwt, glt, gbt, tab, tabc, ovl)


def _dsa_proj_kernel(x_ref, g_ref, w_ref, qn_ref, kvn_ref, kin_ref, wqu_ref, wuk_ref, wqi_ref,
                     qat_ref, qit_ref, wit_ref, ckv_ref, ckvt_ref, kidx_ref, qmem_ref):
    tm = x_ref.shape[1]
    y = _rms(x_ref[0], g_ref[...]).astype(BF16)
    c_q = _rms(_dot(y, w_ref[:, 0:Q_LORA]), qn_ref[...]).astype(BF16)
    c_kv = _rms(_dot(y, w_ref[:, Q_LORA:Q_LORA + KV_LORA]), kvn_ref[...])
    ckv_ref[0] = c_kv.astype(BF16)
    c_kv_t = c_kv.T.astype(BF16)
    for jt in range(tm // TK):
        ckvt_ref[0, jt] = c_kv_t[:, jt * TK:(jt + 1) * TK]
    off = Q_LORA + KV_LORA
    k_idx = _rms(_dot(y, w_ref[:, off:off + LANE]), kin_ref[...], n=IDX_DIM)
    kidx_ref[0] = k_idx.astype(BF16)
    off += LANE
    w_idx = _dot(y, w_ref[:, off:off + LANE]) * (IDX_HEADS ** -0.5 * IDX_DIM ** -0.5)
    wit_ref[0] = w_idx.T[:IDX_HEADS, :]
    off += LANE
    qmem_ref[0] = _spread_heads(_dot(y, w_ref[:, off:off + N_MEM_HEADS * HEAD_DIM]) * HEAD_DIM ** -0.5
                                ).astype(BF16)
    value_row = lax.broadcasted_iota(I32, (LANE, tm), 0) >= IDX_DIM
    for pair in range(IDX_HEADS // 2):
        acc_t = _dot(c_q, wqi_ref[:, pair * LANE:(pair + 1) * LANE]).T
        qit_ref[0, 2 * pair] = jnp.where(value_row, 0.0, acc_t).astype(BF16)
        qit_ref[0, 2 * pair + 1] = jnp.where(value_row, 0.0, pltpu.roll(acc_t, IDX_DIM, axis=0)).astype(BF16)
    for pair in range(N_MIX_HEADS // 2):
        q_pair = (_dot(c_q, wqu_ref[:, pair * LANE:(pair + 1) * LANE]) * HEAD_DIM ** -0.5).astype(BF16)
        for h in (2 * pair, 2 * pair + 1):
            qat_ref[0, h] = (_dot(q_pair, wuk_ref[h]) * LOG2E).T.astype(BF16)


def _dsa_proj(h, gain, w, qn, kvn, kin, wqu, wuk, wqi, tm):
    bsz, t, d = h.shape
    nkt = t // TK
    full = lambda a: pl.BlockSpec(a.shape, lambda b, i: (0,) * a.ndim)
    row = lambda n: pl.BlockSpec((1, tm, n), lambda b, i: (b, i, 0))
    return pl.pallas_call(
        _dsa_proj_kernel,
        grid=(bsz, t // tm),
        in_specs=[pl.BlockSpec((1, tm, d), lambda b, i: (b, i, 0)), full(gain), full(w), full(qn),
                  full(kvn), full(kin), full(wqu), full(wuk), full(wqi)],
        out_specs=[pl.BlockSpec((1, N_MIX_HEADS, LANE, tm), lambda b, i: (b, 0, 0, i)),
                   pl.BlockSpec((1, IDX_HEADS, LANE, tm), lambda b, i: (b, 0, 0, i)),
                   pl.BlockSpec((1, IDX_HEADS, tm), lambda b, i: (b, 0, i)),
                   row(LANE),
                   pl.BlockSpec((1, tm // TK, DSA_VT_ROWS, TK), lambda b, i: (b, i, 0, 0)),
                   row(LANE), row(N_MEM_HEADS * LANE)],
        out_shape=[jax.ShapeDtypeStruct((bsz, N_MIX_HEADS, LANE, t), BF16),
                   jax.ShapeDtypeStruct((bsz, IDX_HEADS, LANE, t), BF16),
                   jax.ShapeDtypeStruct((bsz, IDX_HEADS, t), F32),
                   jax.ShapeDtypeStruct((bsz, t, LANE), BF16),
                   jax.ShapeDtypeStruct((bsz, nkt, DSA_VT_ROWS, TK), BF16),
                   jax.ShapeDtypeStruct((bsz, t, LANE), BF16),
                   jax.ShapeDtypeStruct((bsz, t, N_MEM_HEADS * LANE), BF16)],
        compiler_params=pltpu.CompilerParams(dimension_semantics=("arbitrary", "arbitrary"),
                                             vmem_limit_bytes=VMEM_LIMIT),
        name="dsa_proj",
    )(h, gain, w, qn, kvn, kin, wqu, wuk, wqi)


def _dsa_kernel(qi_ref, wi_ref, qa_ref, kidx_ref, ckv_ref, ckvt_ref, tab_ref, o_ref,
                sc_ref, sc_hi_ref, negm_ref, s_ref, tmax_ref, m_ref, l_ref, acc_ref, *, topk):
    i = pl.program_id(1)
    qs = i * TQ
    n_tiles = (qs + TQ - 1) // TK + 1
    sub = TK // KS
    n_steps = (qs + TQ - 1) // SCORE_TK + 1
    per_step = SCORE_TK // KS
    t_row = qs + lax.broadcasted_iota(I32, (1, TQ), 1)
    krow = lax.broadcasted_iota(I32, (KS, TQ), 0)
    k_f = float(topk)

    grouped = lambda x: x.reshape(KS // SUBLANE, SUBLANE, TQ)

    def score_body(ks, carry):
        for a in range(per_step):
            k0 = pl.multiple_of(ks * SCORE_TK + a * KS, KS)
            kk = kidx_ref[0, pl.ds(k0, KS), :]
            sc = jnp.maximum(_dot(kk, qi_ref[0, 0]), 0.0) * wi_ref[0, 0:1, :]
            for h in range(1, IDX_HEADS):
                sc = sc + jnp.maximum(_dot(kk, qi_ref[0, h]), 0.0) * wi_ref[0, h:h + 1, :]
            sc = jnp.where((k0 + krow) <= t_row, sc, NEG)
            sc_ref[per_step * ks + a] = sc
            sc_hi_ref[per_step * ks + a] = sc.astype(BF16)
        return carry

    lax.fori_loop(0, n_steps, score_body, 0)

    def count(pred):
        def body(ks, acc):
            for a in range(per_step):
                hit = pred(sc_ref[per_step * ks + a], ks * SCORE_TK + a * KS).astype(F32)
                acc = acc + jnp.sum(grouped(hit), axis=0)
            return acc
        acc = lax.fori_loop(0, n_steps, body, jnp.zeros((SUBLANE, TQ), F32))
        return jnp.sum(acc, axis=0, keepdims=True)

    def count_rounded(cand_f):
        cand = jnp.broadcast_to(cand_f, (BF16_ROWS, TQ)).astype(BF16)
        one, zero = jnp.ones((), BF16), jnp.zeros((), BF16)

        def body(ks, acc):
            for a in range(per_step):
                tile = sc_hi_ref[per_step * ks + a].reshape(KS // BF16_ROWS, BF16_ROWS, TQ)
                hit = jnp.where(tile >= cand[None], one, zero)
                part = functools.reduce(jnp.add, [hit[r] for r in range(KS // BF16_ROWS)])
                acc = acc + part.astype(F32)
            return acc
        acc = lax.fori_loop(0, n_steps, body, jnp.zeros((BF16_ROWS, TQ), F32))
        return jnp.sum(acc, axis=0, keepdims=True)

    short = t_row < topk
    half_step = 1 << 15

    def bf16_key_to_float(v16):
        raw16 = jnp.where(v16 >= 0, v16, v16 ^ jnp.int32(0x7FFF))
        return pltpu.bitcast(jnp.left_shift(raw16, 16), F32)

    def high_body(it, v16):
        cand = v16 + jnp.left_shift(jnp.int32(1), 15 - it)
        return jnp.where(count_rounded(bf16_key_to_float(cand)) >= k_f, cand, v16)

    v16 = lax.fori_loop(0, 16, high_body, jnp.full((1, TQ), -(1 << 15), I32))
    key_g = jnp.where(v16 >= 0, jnp.left_shift(v16, 16), jnp.left_shift(v16, 16) | jnp.int32(0xFFFF))

    def low_pass(bit, v, cnt_v):
        cand = v + jnp.left_shift(jnp.int32(1), bit)
        cand_f = _key_to_float(cand)
        cnt = count(lambda sc, k0: sc >= cand_f)
        take = cnt >= k_f
        return jnp.where(take, cand, v), jnp.where(take, cnt, cnt_v)

    def open_count(cnt_v):
        return jnp.sum(((cnt_v != k_f) & ~short).astype(I32))

    passes_per_check = 4
    v, cnt_v = low_pass(16, key_g - half_step, jnp.full((1, TQ), 1e9, F32))

    def v_cond(c):
        return (c[0] >= 0) & (c[3] > 0)

    def v_body(c):
        bit, v, cnt_v, _ = c
        for step in range(passes_per_check):
            v, cnt_v = low_pass(bit - step, v, cnt_v)
        return bit - passes_per_check, v, cnt_v, open_count(cnt_v)

    _, v, cnt_v, open_cols = lax.while_loop(v_cond, v_body, (jnp.int32(15), v, cnt_v, open_count(cnt_v)))
    thr = _key_to_float(v)

    def tie_cut(_):
        need = k_f - count(lambda sc, k0: sc > thr)

        def c_body(it, c):
            cand = c + jnp.left_shift(jnp.int32(1), 13 - it)
            cnt = count(lambda sc, k0: (sc == thr) & ((k0 + krow) < cand))
            return jnp.where(cnt < need, cand, c)

        return lax.fori_loop(0, 14, c_body, jnp.zeros((1, TQ), I32))

    cut = lax.cond(open_cols > 0, tie_cut, lambda _: jnp.full((1, TQ), 2 ** 30, I32), 0)

    def att_mask(kt):
        for a in range(sub):
            sc = sc_ref[sub * kt + a]
            kpos = kt * TK + a * KS + krow
            chosen = short | (sc > thr) | ((sc == thr) & (kpos <= cut))
            negm_ref[a * KS:(a + 1) * KS, :] = jnp.where(chosen & (kpos <= t_row), 0.0, -jnp.inf)

    _attend_tiles(N_MIX_HEADS, n_tiles, lambda h: qa_ref[0, h],
                  lambda kt: ckv_ref[0, pl.ds(pl.multiple_of(kt * TK, TK), TK), :],
                  lambda kt: ckvt_ref[0, kt],
                  lambda kt: (lambda h: _bias_block(lambda m: tab_ref[m, h],
                                                    (TQ // KS) * i - (TK // KS) * kt)),
                  att_mask, negm_ref, s_ref, tmax_ref, m_ref, l_ref, acc_ref)
    for h in range(N_MIX_HEADS):
        out = acc_ref[h] * (1.0 / l_ref[h])
        o_ref[0, :, h * LANE:(h + 1) * LANE] = out.T.astype(o_ref.dtype)


def _dsa_attention(qit, wit, qat, kidx, ckv, ckvt, tab, topk):
    bsz, _, _, t = qat.shape
    nkt = t // TK
    return pl.pallas_call(
        functools.partial(_dsa_kernel, topk=topk),
        grid=(bsz, t // TQ),
        in_specs=[pl.BlockSpec((1, IDX_HEADS, LANE, TQ), lambda b, i: (b, 0, 0, i)),
                  pl.BlockSpec((1, IDX_HEADS, TQ), lambda b, i: (b, 0, i)),
                  pl.BlockSpec((1, N_MIX_HEADS, LANE, TQ), lambda b, i: (b, 0, 0, i)),
                  pl.BlockSpec((1, t, LANE), lambda b, i: (b, 0, 0), pipeline_mode=pl.Buffered(1)),
                  pl.BlockSpec((1, t, LANE), lambda b, i: (b, 0, 0), pipeline_mode=pl.Buffered(1)),
                  pl.BlockSpec((1, nkt, DSA_VT_ROWS, TK), lambda b, i: (b, 0, 0, 0),
                               pipeline_mode=pl.Buffered(1)),
                  pl.BlockSpec(tab.shape, lambda b, i: (0, 0, 0, 0), pipeline_mode=pl.Buffered(1))],
        out_specs=pl.BlockSpec((1, TQ, N_MIX_HEADS * LANE), lambda b, i: (b, i, 0)),
        out_shape=jax.ShapeDtypeStruct((bsz, t, N_MIX_HEADS * LANE), BF16),
        scratch_shapes=[pltpu.VMEM((t // KS, KS, TQ), F32), pltpu.VMEM((t // KS, KS, TQ), BF16),
                        pltpu.VMEM((TK, TQ), F32),
                        pltpu.VMEM((N_MIX_HEADS, TK, TQ), F32), pltpu.VMEM((N_MIX_HEADS, 1, TQ), F32),
                        pltpu.VMEM((N_MIX_HEADS, 1, TQ), F32), pltpu.VMEM((N_MIX_HEADS, 1, TQ), F32),
                        pltpu.VMEM((N_MIX_HEADS, DSA_VT_ROWS, TQ), F32)],
        compiler_params=pltpu.CompilerParams(dimension_semantics=("arbitrary", "arbitrary"),
                                             vmem_limit_bytes=VMEM_LIMIT),
        name="dsa_attention",
    )(qit, wit, qat, kidx, ckv, ckvt, tab)


def _tail_kernel(h_ref, mix_ref, qmem_ref, kvm_ref, *rest, has_uv, final_norm):
    if has_uv:
        wuv_ref, *rest = rest
    wmix_ref, wmem_ref, g_ref, wg_ref, wu_ref, wd_ref, gf_ref, o_ref = rest
    mix = mix_ref[0]
    if has_uv:
        mix = _dot(mix, wuv_ref[...]).astype(BF16)
    upd = _dot(mix, wmix_ref[...])
    qm = qmem_ref[0]
    for hm in range(N_MEM_HEADS):
        sl = slice(hm * LANE, (hm + 1) * LANE)
        kv = kvm_ref[0, :, sl]
        s = _dot_nt(qm[:, sl], kv)
        e = jnp.exp(s - jnp.max(s, axis=-1, keepdims=True))
        p = e / jnp.sum(e, axis=-1, keepdims=True)
        o_h = _dot(p.astype(BF16), kv).astype(BF16)
        upd = upd + _dot(o_h, wmem_ref[sl, :])
    h = h_ref[0] + upd
    hn = _rms(h, g_ref[...]).astype(BF16)
    act = (jax.nn.silu(_dot(hn, wg_ref[...])) * _dot(hn, wu_ref[...])).astype(BF16)
    out = h + _dot(act, wd_ref[...])
    if final_norm:
        out = _rms(out, gf_ref[...])
    o_ref[0] = out


def _layer_tail(h, mix, qmem, kvm, w_uv, w_mix, w_mem, gain, wg, wu, wd, gain_final, final_norm, tm):
    bsz, t, d = h.shape
    has_uv = w_uv is not None
    const = lambda a: pl.BlockSpec(a.shape, lambda b, i: (0,) * a.ndim, pipeline_mode=pl.Buffered(1))
    weights = ([w_uv] if has_uv else []) + [w_mix, w_mem, gain, wg, wu, wd, gain_final]
    return pl.pallas_call(
        functools.partial(_tail_kernel, has_uv=has_uv, final_norm=final_norm),
        grid=(bsz, t // tm),
        in_specs=[pl.BlockSpec((1, tm, d), lambda b, i: (b, i, 0)),
                  pl.BlockSpec((1, tm, mix.shape[2]), lambda b, i: (b, i, 0)),
                  pl.BlockSpec((1, tm, qmem.shape[2]), lambda b, i: (b, i, 0)),
                  pl.BlockSpec((1,) + kvm.shape[1:], lambda b, i: (b, 0, 0))]
                 + [const(w) for w in weights],
        out_specs=pl.BlockSpec((1, tm, d), lambda b, i: (b, i, 0)),
        out_shape=jax.ShapeDtypeStruct((bsz, t, d), F32),
        compiler_params=pltpu.CompilerParams(dimension_semantics=("arbitrary", "arbitrary"),
                                             vmem_limit_bytes=VMEM_LIMIT),
        name="layer_tail",
    )(h, mix, qmem, kvm, *weights)


def _pad_cols(w, n):
    return jnp.pad(w, ((0, 0), (0, n - w.shape[1])))


def _value_rows(w_rows, n_heads):
    d_out = w_rows.shape[1]
    w = w_rows.reshape(n_heads, HEAD_DIM, d_out)
    return jnp.pad(w, ((0, 0), (LANE - HEAD_DIM, 0), (0, 0))).reshape(n_heads * LANE, d_out)


def _interleave_kv(k, v, n_heads):
    d_in = k.shape[0]
    kv = jnp.concatenate([k.reshape(d_in, n_heads, HEAD_DIM), v.reshape(d_in, n_heads, HEAD_DIM)], axis=2)
    return kv.reshape(d_in, n_heads * LANE)


def _bucket_bias(rel_bias, bucket_np):
    onehot = jax.nn.one_hot(jnp.asarray(bucket_np.reshape(-1)), N_REL_BUCKETS, dtype=F32)
    out = jnp.dot(onehot, rel_bias, precision=lax.Precision.HIGHEST) * LOG2E
    return out.reshape(bucket_np.shape + (rel_bias.shape[1],))


def _bias_tiles(rel_bias):
    m = np.arange(N_BIAS_TILES)[:, None, None]
    dist = LANE * m + np.arange(LANE)[None, None, :] - np.arange(KS)[None, :, None]
    return jnp.transpose(_bucket_bias(rel_bias, _rel_bucket_np(dist)), (0, 3, 1, 2))


def _bias_cmp_table(rel_bias):
    rel = np.arange(2 * CMP_PAD) - CMP_PAD
    dist = np.arange(TQ)[None, :] - CMP_STRIDE * rel[:, None] - (CMP_LEN - 1)
    return jnp.transpose(_bucket_bias(rel_bias, _rel_bucket_np(dist)), (2, 0, 1))


def _overlap_matrix(t):
    n_cmp = (t - CMP_LEN) // CMP_STRIDE + 1
    n_sel = t // SEL_BLOCK
    cs = np.arange(CMP_PAD) * CMP_STRIDE
    ss = np.arange(LANE) * SEL_BLOCK
    ov = (cs[None, :] <= ss[:, None] + SEL_BLOCK - 1) & (cs[None, :] + CMP_LEN - 1 >= ss[:, None])
    ov &= (np.arange(CMP_PAD) < n_cmp)[None, :] & (np.arange(LANE) < n_sel)[:, None]
    return jnp.asarray(ov, BF16)


def kernel(x, mem, rel_bias, norm_mix, norm_ffn, norm_mem, w_mem_kv, w_out, ffn_gate, ffn_up, ffn_down,
           nsa_w_in, nsa_gate_b, nsa_cmp_pos_k, nsa_cmp_pos_v,
           nsa_cmp_k_w1, nsa_cmp_k_b1, nsa_cmp_k_w2, nsa_cmp_k_b2,
           nsa_cmp_v_w1, nsa_cmp_v_b1, nsa_cmp_v_w2, nsa_cmp_v_b2,
           dsa_w_in, dsa_q_norm, dsa_kv_norm, dsa_w_q_up, dsa_w_uk, dsa_w_uv, dsa_w_q_idx, dsa_kidx_norm,
           norm_final):
    bsz, t, d = x.shape
    m_len = mem.shape[1]
    depth = norm_mix.shape[0]
    g, r, hh = NSA_KV_HEADS, NSA_GROUP, N_MIX_HEADS
    d_mix = hh * HEAD_DIM
    kvw_ = g * HEAD_DIM
    assert t % 1024 == 0 and t // SEL_BLOCK <= LANE and t // CMP_STRIDE <= CMP_PAD
    tm = 512
    h = x.astype(F32)

    tab = _bias_tiles(rel_bias)
    tab_nsa = jnp.transpose(tab.reshape(N_BIAS_TILES, g, r, KS, LANE), (1, 0, 2, 3, 4))
    tabc = _bias_cmp_table(rel_bias).reshape(g, r, 2 * CMP_PAD, TQ)
    ovl = _overlap_matrix(t)

    for layer in range(depth):
        j = layer // 2
        wkv = w_mem_kv[layer]
        wkv = _interleave_kv(wkv[:, :N_MEM_HEADS * HEAD_DIM], wkv[:, N_MEM_HEADS * HEAD_DIM:], N_MEM_HEADS)
        kvm = _rms_proj(mem.reshape(bsz * m_len, d), norm_mem[layer], wkv.astype(BF16), BF16, m_len)
        kvm = kvm.reshape(bsz, m_len, N_MEM_HEADS * LANE)
        w_o = w_out[layer]
        w_mem_o = _value_rows(w_o[d_mix:], N_MEM_HEADS).astype(BF16)
        w_mix = w_o[:d_mix].astype(BF16)

        if layer % 2 == 0:
            w = nsa_w_in[j]
            c = np.cumsum([0, d_mix, kvw_, kvw_, kvw_, kvw_, kvw_, kvw_, hh * N_BRANCH,
                           N_MEM_HEADS * HEAD_DIM])
            wq, wkc, wvc, wks, wvs, wkw, wvw, wgl, wqm = [w[:, c[k]:c[k + 1]] for k in range(9)]
            n_g = r * N_BRANCH
            wgl = jnp.concatenate([_pad_cols(wgl[:, gg * n_g:(gg + 1) * n_g], LANE) for gg in range(g)], 1)
            w_all = jnp.concatenate([wq, wkc, wvc, _interleave_kv(wks, wvs, g),
                                     _interleave_kv(wkw, wvw, g), wgl, wqm], axis=1)
            qt, cmp_raw, kvs, vst, kvw, vwt, glt, qmem = _nsa_proj(h, norm_mix[layer], w_all.astype(BF16), tm)
            nc = t // CMP_STRIDE
            x2 = cmp_raw.reshape(bsz, 2 * g, nc, CMP_STRIDE * HEAD_DIM)
            pos = jnp.stack([nsa_cmp_pos_k[j], nsa_cmp_pos_v[j]]).reshape(2, 2, CMP_STRIDE * HEAD_DIM)
            w1 = jnp.stack([nsa_cmp_k_w1[j], nsa_cmp_v_w1[j]]).astype(BF16)
            b1 = jnp.stack([nsa_cmp_k_b1[j], nsa_cmp_v_b1[j]])[:, None, :]
            w2 = jnp.stack([jnp.pad(nsa_cmp_k_w2[j], ((0, 0), (0, HEAD_DIM))),
                            jnp.pad(nsa_cmp_v_w2[j], ((0, 0), (HEAD_DIM, 0)))]).astype(BF16)
            b2 = jnp.stack([jnp.pad(nsa_cmp_k_b2[j], (0, HEAD_DIM)),
                            jnp.pad(nsa_cmp_v_b2[j], (HEAD_DIM, 0))])[:, None, :]
            kvc = _compress(x2, pos, w1, b1, w2, b2)
            kvc = jnp.pad(kvc, ((0, 0), (0, 0), (0, CMP_PAD - nc), (0, 0)))
            vct = jnp.transpose(kvc, (0, 1, 3, 2))
            gb = jnp.pad(nsa_gate_b[j].reshape(g, n_g), ((0, 0), (0, GATE_ROWS - n_g)))
            gbt = jnp.broadcast_to(gb[:, :, None], (g, GATE_ROWS, TQ))
            mix = _nsa_attention(qt, kvc, vct, kvs, vst, kvw, vwt, glt, gbt, tab_nsa, tabc, ovl,
                                 t // SEL_BLOCK)
            w_uv = None
        else:
            w = dsa_w_in[j]
            c = np.cumsum([0, Q_LORA, KV_LORA, IDX_DIM, IDX_HEADS, N_MEM_HEADS * HEAD_DIM])
            wcq, wckv, wki, wwi, wqm = [w[:, c[k]:c[k + 1]] for k in range(5)]
            w_all = jnp.concatenate([wcq, wckv, _pad_cols(wki, LANE), _pad_cols(wwi, LANE),
                                     wqm], axis=1).astype(BF16)
            wuk = jnp.transpose(dsa_w_uk[j], (1, 2, 0))
            wuk = jnp.stack([jnp.pad(wuk[hd], ((HEAD_DIM * (hd % 2), HEAD_DIM * (1 - hd % 2)), (0, 0)))
                             for hd in range(hh)]).astype(BF16)
            qat, qit, wit, ckv, ckvt, kidx, qmem = _dsa_proj(
                h, norm_mix[layer][None], w_all, dsa_q_norm[j][None], dsa_kv_norm[j][None],
                _pad_cols(dsa_kidx_norm[j][None], LANE), dsa_w_q_up[j].astype(BF16), wuk,
                dsa_w_q_idx[j].astype(BF16), tm)
            mix = _dsa_attention(qit, wit, qat, kidx, ckv, ckvt, tab, min(DSA_TOPK, t // 4))
            wv = jnp.transpose(dsa_w_uv[j], (1, 0, 2))
            w_uv = (jnp.eye(hh, dtype=F32)[:, None, :, None] * wv[:, :, None, :]
                    ).reshape(hh * KV_LORA, d_mix).astype(BF16)

        h = _layer_tail(h, mix, qmem, kvm, w_uv, w_mix, w_mem_o, norm_ffn[layer][None],
                        ffn_gate[layer].astype(BF16), ffn_up[layer].astype(BF16),
                        ffn_down[layer].astype(BF16), norm_final[None], layer == depth - 1, tm)
    return h.astype(x.dtype)
```

```python
import functools
import math

import numpy as np
import jax
import jax.numpy as jnp
from jax import lax
from jax.experimental import pallas as pl
from jax.experimental.pallas import tpu as pltpu

F32 = jnp.float32
BF16 = jnp.bfloat16
I32 = jnp.int32

NEG = -1e30
EPS = 1e-6
LOG2E = math.log2(math.e)
LANE = 128
SUBLANE = 8
BF16_ROWS = 16
HEAD_DIM = 64
N_MIX_HEADS = 12
N_MEM_HEADS = 4
N_REL_BUCKETS = 32
REL_MAX_EXACT = 16
REL_MAX_DIST = 2048
NSA_KV_HEADS = 2
NSA_GROUP = N_MIX_HEADS // NSA_KV_HEADS
N_BRANCH = 3
GATE_ROWS = 24
CMP_LEN = 32
CMP_STRIDE = 16
SEL_BLOCK = 64
N_SEL = 16
WINDOW = 512
FORCE_BONUS = 1e4
Q_LORA = 256
KV_LORA = 128
IDX_HEADS = 8
IDX_DIM = 64
DSA_TOPK = 256
TQ = 256
TK = 256
SCORE_TK = 512
KS = LANE
N_BIAS_TILES = REL_MAX_DIST // KS + 2
CMP_PAD = 512
NSA_VT_ROWS = 2 * HEAD_DIM
DSA_VT_ROWS = KV_LORA
VMEM_LIMIT = 56 * 1024 * 1024
INT_MIN = -2 ** 31
F32_MIN = float(np.finfo(np.float32).min)


def _dot(a, b):
    return jnp.dot(a, b, preferred_element_type=F32)


def _dot_nt(a, b):
    return lax.dot_general(a, b, (((1,), (1,)), ((), ())), preferred_element_type=F32)


def _rms(x, gain, n=None):
    n = x.shape[-1] if n is None else n
    ms = jnp.sum(x * x, axis=-1, keepdims=True) * (1.0 / n)
    return x * lax.rsqrt(ms + EPS) * gain


def _rel_bucket_np(dist):
    n = np.maximum(dist, 0)
    nf = np.maximum(n, REL_MAX_EXACT).astype(np.float32)
    large = REL_MAX_EXACT + (np.log(nf / np.float32(REL_MAX_EXACT))
                             / np.float32(math.log(REL_MAX_DIST / REL_MAX_EXACT))
                             * np.float32(N_REL_BUCKETS - REL_MAX_EXACT)).astype(np.int32)
    large = np.minimum(large, N_REL_BUCKETS - 1)
    return np.where(n < REL_MAX_EXACT, n, large).astype(np.int32)


def _spread_heads(x):
    rows, n = x.shape
    low = lax.broadcasted_iota(I32, (rows, LANE), 1) < HEAD_DIM
    out = []
    for j in range(n // LANE):
        pair = x[:, j * LANE:(j + 1) * LANE]
        out += [jnp.where(low, pair, 0.0), jnp.where(low, pltpu.roll(pair, HEAD_DIM, axis=1), 0.0)]
    return jnp.concatenate(out, axis=1)


def _colsum(x):
    rows, n = x.shape
    return jnp.sum(jnp.sum(x.reshape(rows // SUBLANE, SUBLANE, n), axis=0), axis=0, keepdims=True)


def _key_to_float(v):
    bits = jnp.where(v >= 0, v, v ^ jnp.int32(0x7FFFFFFF))
    return pltpu.bitcast(bits, F32)


def _topk_cols(score, k, row_idx):
    n = score.shape[1]

    def vbody(it, v):
        cand = v + jnp.left_shift(jnp.int32(1), 31 - it)
        cnt = _colsum((score >= _key_to_float(cand)).astype(F32))
        return jnp.where(cnt >= k, cand, v)

    v = lax.fori_loop(0, 32, vbody, jnp.full((1, n), INT_MIN, I32))
    thr = _key_to_float(v)
    gt = score > thr
    eq = score == thr
    need = k - _colsum(gt.astype(F32))

    def cbody(it, c):
        cand = c + jnp.left_shift(jnp.int32(1), 6 - it)
        cnt = _colsum((eq & (row_idx < cand)).astype(F32))
        return jnp.where(cnt < need, cand, c)

    c = lax.fori_loop(0, 7, cbody, jnp.zeros((1, n), I32))
    return gt | (eq & (row_idx <= c))


def _bias_block(tile_of, m0):
    idx = lambda m: jnp.clip(m, 0, N_BIAS_TILES - 1)
    return [jnp.concatenate([tile_of(idx(m0 + b - a)) for b in range(TQ // LANE)], axis=1)
            for a in range(TK // KS)]


def _attend_tiles(n_heads, n_tiles, q_of, kv_of, vt_of, bias_of, fill_mask, negm_ref, s_ref, tmax_ref,
                  m_ref, l_ref, acc_ref, ones_row=None):
    parts = [slice(a * KS, (a + 1) * KS) for a in range(TK // KS)]
    m_ref[...] = jnp.full(m_ref.shape, F32_MIN, F32)
    l_ref[...] = jnp.zeros(l_ref.shape, F32)
    acc_ref[...] = jnp.zeros(acc_ref.shape, F32)

    def logits_of(j):
        fill_mask(j)
        kv = kv_of(j)
        bias_h = bias_of(j)

        def run(h):
            bias = bias_h(h)
            tmax = None
            for a, sl in enumerate(parts):
                s_a = _dot(kv[sl], q_of(h)) + bias[a] + negm_ref[sl, :]
                s_ref[h, sl, :] = s_a
                mx = jnp.max(s_a, axis=0, keepdims=True)
                tmax = mx if tmax is None else jnp.maximum(tmax, mx)
            tmax_ref[h] = tmax
        return run

    def accumulate(j):
        vt = vt_of(j)

        def run(h):
            m_old = m_ref[h]
            m_new = jnp.maximum(m_old, tmax_ref[h])
            alpha = jnp.exp2(m_old - m_new)
            m_ref[h] = m_new
            p = [jnp.exp2(s_ref[h, sl, :] - m_new) for sl in parts]
            if ones_row is None:
                l_ref[h] = alpha * l_ref[h] + _colsum(sum(p[1:], p[0]))
            acc = alpha * acc_ref[h]
            for sl, p_a in zip(parts, p):
                acc = acc + _dot(vt[:, sl], p_a.astype(BF16))
            acc_ref[h] = acc
        return run

    first = logits_of(0)
    for h in range(n_heads):
        first(h)

    def body(j, carry):
        consume = accumulate(j)
        produce = logits_of(jnp.minimum(j + 1, n_tiles - 1))
        for h in range(n_heads):
            consume(h)
            produce(h)
        return carry

    lax.fori_loop(0, n_tiles, body, 0)
    if ones_row is not None:
        for h in range(n_heads):
            l_ref[h] = acc_ref[h, ones_row:ones_row + 1, :]


def _rms_proj_kernel(x_ref, g_ref, w_ref, o_ref):
    y = _rms(x_ref[...], g_ref[...]).astype(BF16)
    o_ref[...] = _dot(y, w_ref[...]).astype(o_ref.dtype)


def _rms_proj(x2d, gain, w_bf16, out_dtype, tm):
    m, d = x2d.shape
    n = w_bf16.shape[1]
    return pl.pallas_call(
        _rms_proj_kernel,
        grid=(m // tm,),
        in_specs=[pl.BlockSpec((tm, d), lambda i: (i, 0)),
                  pl.BlockSpec((1, d), lambda i: (0, 0)),
                  pl.BlockSpec((d, n), lambda i: (0, 0))],
        out_specs=pl.BlockSpec((tm, n), lambda i: (i, 0)),
        out_shape=jax.ShapeDtypeStruct((m, n), out_dtype),
        compiler_params=pltpu.CompilerParams(dimension_semantics=("arbitrary",),
                                             vmem_limit_bytes=VMEM_LIMIT),
        name="rms_proj",
    )(x2d, gain.reshape(1, d), w_bf16)


def _nsa_proj_kernel(x_ref, g_ref, w_ref, qt_ref, cmp_ref, kvs_ref, vst_ref, kvw_ref, vwt_ref,
                     glt_ref, qmem_ref):
    tm = x_ref.shape[1]
    y = _rms(x_ref[0], g_ref[...]).astype(BF16)
    value_row = lax.broadcasted_iota(I32, (LANE, tm), 0) >= HEAD_DIM
    off = 0
    for pair in range(N_MIX_HEADS // 2):
        acc_t = (_dot(y, w_ref[:, off:off + LANE]) * (HEAD_DIM ** -0.5 * LOG2E)).T
        qt_ref[0, 2 * pair] = jnp.where(value_row, 0.0, acc_t).astype(BF16)
        qt_ref[0, 2 * pair + 1] = jnp.where(value_row, 0.0, pltpu.roll(acc_t, HEAD_DIM, axis=0)).astype(BF16)
        off += LANE
    raw = _dot(y, w_ref[:, off:off + 2 * LANE])
    for piece in range(2 * NSA_KV_HEADS):
        cmp_ref[0, piece] = raw[:, piece * HEAD_DIM:(piece + 1) * HEAD_DIM]
    off += 2 * LANE
    for kv_ref, vt_ref in ((kvs_ref, vst_ref), (kvw_ref, vwt_ref)):
        for g in range(NSA_KV_HEADS):
            acc = _dot(y, w_ref[:, off:off + LANE])
            off += LANE
            kv_ref[0, :, g * LANE:(g + 1) * LANE] = acc.astype(BF16)
            v_t = jnp.where(value_row, acc.T, 1.0).astype(BF16)
            for jt in range(tm // TK):
                vt_ref[0, g, jt] = v_t[:, jt * TK:(jt + 1) * TK]
    for g in range(NSA_KV_HEADS):
        acc = _dot(y, w_ref[:, off:off + LANE])
        off += LANE
        glt_ref[0, g] = acc.T[:GATE_ROWS, :]
    qmem_ref[0] = _spread_heads(_dot(y, w_ref[:, off:off + N_MEM_HEADS * HEAD_DIM]) * HEAD_DIM ** -0.5
                                ).astype(BF16)


def _nsa_proj(h, gain, w, tm):
    bsz, t, d = h.shape
    g = NSA_KV_HEADS
    nkt = t // TK
    row = lambda n: pl.BlockSpec((1, tm, n), lambda b, i: (b, i, 0))
    vt_spec = pl.BlockSpec((1, g, tm // TK, NSA_VT_ROWS, TK), lambda b, i: (b, 0, i, 0, 0))
    return pl.pallas_call(
        _nsa_proj_kernel,
        grid=(bsz, t // tm),
        in_specs=[pl.BlockSpec((1, tm, d), lambda b, i: (b, i, 0)),
                  pl.BlockSpec((1, d), lambda b, i: (0, 0)),
                  pl.BlockSpec(w.shape, lambda b, i: (0, 0))],
        out_specs=[pl.BlockSpec((1, N_MIX_HEADS, LANE, tm), lambda b, i: (b, 0, 0, i)),
                   pl.BlockSpec((1, 2 * g, tm, HEAD_DIM), lambda b, i: (b, 0, i, 0)),
                   row(g * LANE), vt_spec, row(g * LANE), vt_spec,
                   pl.BlockSpec((1, g, GATE_ROWS, tm), lambda b, i: (b, 0, 0, i)),
                   row(N_MEM_HEADS * LANE)],
        out_shape=[jax.ShapeDtypeStruct((bsz, N_MIX_HEADS, LANE, t), BF16),
                   jax.ShapeDtypeStruct((bsz, 2 * g, t, HEAD_DIM), F32),
                   jax.ShapeDtypeStruct((bsz, t, g * LANE), BF16),
                   jax.ShapeDtypeStruct((bsz, g, nkt, NSA_VT_ROWS, TK), BF16),
                   jax.ShapeDtypeStruct((bsz, t, g * LANE), BF16),
                   jax.ShapeDtypeStruct((bsz, g, nkt, NSA_VT_ROWS, TK), BF16),
                   jax.ShapeDtypeStruct((bsz, g, GATE_ROWS, t), F32),
                   jax.ShapeDtypeStruct((bsz, t, N_MEM_HEADS * LANE), BF16)],
        compiler_params=pltpu.CompilerParams(dimension_semantics=("arbitrary", "arbitrary"),
                                             vmem_limit_bytes=VMEM_LIMIT),
        name="nsa_proj",
    )(h, gain.reshape(1, d), w)


def _compress_kernel(xk_ref, xv_ref, pos_ref, w1_ref, b1_ref, w2_ref, b2_ref, o_ref):
    nc = xk_ref.shape[2]
    half = xk_ref.shape[3]
    out = None
    for j, x_ref in enumerate((xk_ref, xv_ref)):
        x = x_ref[0, 0]
        top = _dot((x + pos_ref[j, 0:1, :]).astype(BF16), w1_ref[j, :half, :])
        bot = _dot((x + pos_ref[j, 1:2, :]).astype(BF16), w1_ref[j, half:, :])
        pre = top + pltpu.roll(bot, nc - 1, axis=0) + b1_ref[j]
        hid = jax.nn.gelu(pre)
        res = _dot(hid.astype(BF16), w2_ref[j]) + b2_ref[j]
        out = res if out is None else out + res
    o_ref[0, 0] = out.astype(o_ref.dtype)


def _compress(x2, pos, w1, b1, w2, b2):
    bsz, _, nc, width = x2.shape
    g = NSA_KV_HEADS
    return pl.pallas_call(
        _compress_kernel,
        grid=(bsz, g),
        in_specs=[pl.BlockSpec((1, 1, nc, width), lambda b, gg: (b, gg, 0, 0)),
                  pl.BlockSpec((1, 1, nc, width), lambda b, gg: (b, gg + NSA_KV_HEADS, 0, 0)),
                  pl.BlockSpec(pos.shape, lambda b, gg: (0, 0, 0)),
                  pl.BlockSpec(w1.shape, lambda b, gg: (0, 0, 0)),
                  pl.BlockSpec(b1.shape, lambda b, gg: (0, 0, 0)),
                  pl.BlockSpec(w2.shape, lambda b, gg: (0, 0, 0)),
                  pl.BlockSpec(b2.shape, lambda b, gg: (0, 0, 0))],
        out_specs=pl.BlockSpec((1, 1, nc, LANE), lambda b, gg: (b, gg, 0, 0)),
        out_shape=jax.ShapeDtypeStruct((bsz, g, nc, LANE), BF16),
        compiler_params=pltpu.CompilerParams(dimension_semantics=("arbitrary", "arbitrary"),
                                             vmem_limit_bytes=VMEM_LIMIT),
        name="nsa_compress",
    )(x2, x2, pos, w1, b1, w2, b2)


def _nsa_kernel(q_ref, kvc_ref, vct_ref, kvs_ref, vst_ref, kvw_ref, vwt_ref, glt_ref, gbt_ref,
                tab_ref, tabc_ref, ovl_ref, o_ref,
                psum_ref, negsel_ref, negm_ref, s_ref, tmax_ref, m_ref, l_ref, acc_ref, ocmp_ref, oslc_ref,
                *, n_sel):
    r_heads = NSA_GROUP
    i = pl.program_id(2)
    qs = i * TQ
    t_row = qs + lax.broadcasted_iota(I32, (1, TQ), 1)
    krow = lax.broadcasted_iota(I32, (TK, TQ), 0)
    q_of = lambda h: q_ref[0, h]

    def normalised(h):
        return acc_ref[h, NSA_VT_ROWS - HEAD_DIM:, :] * (1.0 / l_ref[h])

    kvc = kvc_ref[0, 0]
    vct = vct_ref[0, 0]
    n_row = lax.broadcasted_iota(I32, (CMP_PAD, TQ), 0)
    negc = jnp.where((CMP_STRIDE * n_row + (CMP_LEN - 1)) <= t_row, 0.0, -jnp.inf)
    j0 = pl.multiple_of(CMP_PAD - (TQ // CMP_STRIDE) * i, SUBLANE)
    psum_ref[...] = jnp.zeros(psum_ref.shape, F32)

    cparts = [slice(a * KS, (a + 1) * KS) for a in range(CMP_PAD // KS)]
    for h in range(r_heads):
        s = [_dot(kvc[sl], q_of(h)) + tabc_ref[0, h, pl.ds(j0 + sl.start, KS), :] + negc[sl] for sl in cparts]
        m = functools.reduce(jnp.maximum, [jnp.max(s_a, axis=0, keepdims=True) for s_a in s])
        m = jnp.maximum(m, F32_MIN)
        e = [jnp.exp2(s_a - m) for s_a in s]
        den = _colsum(functools.reduce(jnp.add, e))
        inv = 1.0 / jnp.where(den > 0.0, den, 1.0)
        acc = functools.reduce(jnp.add, [_dot(vct[:, sl], e_a.astype(BF16)) for sl, e_a in zip(cparts, e)])
        ocmp_ref[h] = acc[LANE - HEAD_DIM:] * inv
        for sl, e_a in zip(cparts, e):
            psum_ref[sl, :] += e_a * inv

    psum = psum_ref[...]
    ovl = ovl_ref[...]
    hi = psum.astype(BF16)
    rem1 = psum - hi.astype(F32)
    mid = rem1.astype(BF16)
    lo = (rem1 - mid.astype(F32)).astype(BF16)
    p_slc = _dot(ovl, hi) + _dot(ovl, mid) + _dot(ovl, lo)
    blk = lax.broadcasted_iota(I32, (LANE, TQ), 0)
    cur = jnp.right_shift(t_row, 6)
    forced = (blk == 0) | (blk == cur) | (blk == cur - 1)
    admissible = (blk * SEL_BLOCK) <= t_row
    score = jnp.where(admissible, p_slc + jnp.where(forced, FORCE_BONUS, 0.0), NEG)
    score = jnp.where(blk < n_sel, score, -jnp.inf)
    sel = _topk_cols(score, min(N_SEL, n_sel), blk) & (score > 0.5 * NEG)
    negsel_ref[...] = jnp.where(sel, 0.0, -jnp.inf)

    def bias_of(kt):
        return lambda h: _bias_block(lambda m: tab_ref[0, m, h], (TQ // KS) * i - (TK // KS) * kt)

    def key_rows(ref, kt):
        return ref[0, pl.ds(pl.multiple_of(kt * TK, TK), TK), :]

    last_tile = (qs + TQ - 1) // TK

    def sel_mask(kt):
        rows = [jnp.broadcast_to(negsel_ref[pl.ds((TK // SEL_BLOCK) * kt + a, 1), :], (SEL_BLOCK, TQ))
                for a in range(TK // SEL_BLOCK)]
        negm_ref[...] = jnp.where((kt * TK + krow) <= t_row, jnp.concatenate(rows, axis=0), -jnp.inf)

    _attend_tiles(r_heads, last_tile + 1, q_of, lambda kt: key_rows(kvs_ref, kt), lambda kt: vst_ref[0, 0, kt],
                  bias_of, sel_mask, negm_ref, s_ref, tmax_ref, m_ref, l_ref, acc_ref, ones_row=0)
    for h in range(r_heads):
        oslc_ref[h] = normalised(h)

    n_win = max((TQ * e + TQ - 1) // TK - (TQ * e - WINDOW + 1) // TK + 1 for e in range(max(TK // TQ, 1)))

    def win_mask(j):
        kt = last_tile - j
        dist = t_row - (jnp.maximum(kt, 0) * TK + krow)
        ok = (dist >= 0) & (dist < jnp.where(kt >= 0, WINDOW, 0))
        negm_ref[...] = jnp.where(ok, 0.0, -jnp.inf)

    _attend_tiles(r_heads, n_win, q_of, lambda j: key_rows(kvw_ref, jnp.maximum(last_tile - j, 0)),
                  lambda j: vwt_ref[0, 0, jnp.maximum(last_tile - j, 0)], lambda j: bias_of(last_tile - j),
                  win_mask, negm_ref, s_ref, tmax_ref, m_ref, l_ref, acc_ref, ones_row=0)

    gates = jax.nn.sigmoid(glt_ref[0, 0] + gbt_ref[0])
    for pair in range(r_heads // 2):
        outs = []
        for r in (2 * pair, 2 * pair + 1):
            c = N_BRANCH * r
            outs.append(gates[c:c + 1] * ocmp_ref[r] + gates[c + 1:c + 2] * oslc_ref[r]
                        + gates[c + 2:c + 3] * normalised(r))
        o_ref[0, :, pair * LANE:(pair + 1) * LANE] = jnp.concatenate(outs, axis=0).T.astype(o_ref.dtype)


def _nsa_attention(qt, kvc, vct, kvs, vst, kvw, vwt, glt, gbt, tab, tabc, ovl, n_sel):
    bsz, _, _, t = qt.shape
    g, r = NSA_KV_HEADS, NSA_GROUP
    nkt = t // TK
    once = dict(pipeline_mode=pl.Buffered(1))
    return pl.pallas_call(
        functools.partial(_nsa_kernel, n_sel=n_sel),
        grid=(bsz, g, t // TQ),
        in_specs=[pl.BlockSpec((1, r, LANE, TQ), lambda b, gg, i: (b, gg, 0, i)),
                  pl.BlockSpec((1, 1, CMP_PAD, LANE), lambda b, gg, i: (b, gg, 0, 0)),
                  pl.BlockSpec((1, 1, LANE, CMP_PAD), lambda b, gg, i: (b, gg, 0, 0)),
                  pl.BlockSpec((1, t, LANE), lambda b, gg, i: (b, 0, gg)),
                  pl.BlockSpec((1, 1, nkt, NSA_VT_ROWS, TK), lambda b, gg, i: (b, gg, 0, 0, 0)),
                  pl.BlockSpec((1, t, LANE), lambda b, gg, i: (b, 0, gg)),
                  pl.BlockSpec((1, 1, nkt, NSA_VT_ROWS, TK), lambda b, gg, i: (b, gg, 0, 0, 0)),
                  pl.BlockSpec((1, 1, GATE_ROWS, TQ), lambda b, gg, i: (b, gg, 0, i)),
                  pl.BlockSpec((1, GATE_ROWS, TQ), lambda b, gg, i: (gg, 0, 0)),
                  pl.BlockSpec((1, N_BIAS_TILES, r, KS, LANE), lambda b, gg, i: (gg, 0, 0, 0, 0), **once),
                  pl.BlockSpec((1, r, 2 * CMP_PAD, TQ), lambda b, gg, i: (gg, 0, 0, 0), **once),
                  pl.BlockSpec((LANE, CMP_PAD), lambda b, gg, i: (0, 0))],
        out_specs=pl.BlockSpec((1, TQ, r * HEAD_DIM), lambda b, gg, i: (b, i, gg)),
        out_shape=jax.ShapeDtypeStruct((bsz, t, g * r * HEAD_DIM), BF16),
        scratch_shapes=[pltpu.VMEM((CMP_PAD, TQ), F32), pltpu.VMEM((LANE, TQ), F32),
                        pltpu.VMEM((TK, TQ), F32), pltpu.VMEM((r, TK, TQ), F32),
                        pltpu.VMEM((r, 1, TQ), F32), pltpu.VMEM((r, 1, TQ), F32), pltpu.VMEM((r, 1, TQ), F32),
                        pltpu.VMEM((r, NSA_VT_ROWS, TQ), F32), pltpu.VMEM((r, HEAD_DIM, TQ), F32),
                        pltpu.VMEM((r, HEAD_DIM, TQ), F32)],
        compiler_params=pltpu.CompilerParams(
            dimension_semantics=("arbitrary", "arbitrary", "arbitrary"),
            vmem_limit_bytes=VMEM_LIMIT),
        name="nsa_attention",
    )(qt, kvc, vct, kvs, vst, kvw, vwt, glt, gbt, tab, tabc, ovl)


def _dsa_proj_kernel(x_ref, g_ref, w_ref, qn_ref, kvn_ref, kin_ref, wqu_ref, wuk_ref, wqi_ref,
                     qat_ref, qit_ref, wit_ref, ckv_ref, ckvt_ref, kidx_ref, qmem_ref):
    tm = x_ref.shape[1]
    y = _rms(x_ref[0], g_ref[...]).astype(BF16)
    c_q = _rms(_dot(y, w_ref[:, 0:Q_LORA]), qn_ref[...]).astype(BF16)
    c_kv = _rms(_dot(y, w_ref[:, Q_LORA:Q_LORA + KV_LORA]), kvn_ref[...])
    ckv_ref[0] = c_kv.astype(BF16)
    c_kv_t = c_kv.T.astype(BF16)
    for jt in range(tm // TK):
        ckvt_ref[0, jt] = c_kv_t[:, jt * TK:(jt + 1) * TK]
    off = Q_LORA + KV_LORA
    k_idx = _rms(_dot(y, w_ref[:, off:off + LANE]), kin_ref[...], n=IDX_DIM)
    kidx_ref[0] = k_idx.astype(BF16)
    off += LANE
    w_idx = _dot(y, w_ref[:, off:off + LANE]) * (IDX_HEADS ** -0.5 * IDX_DIM ** -0.5)
    wit_ref[0] = w_idx.T[:IDX_HEADS, :]
    off += LANE
    qmem_ref[0] = _spread_heads(_dot(y, w_ref[:, off:off + N_MEM_HEADS * HEAD_DIM]) * HEAD_DIM ** -0.5
                                ).astype(BF16)
    value_row = lax.broadcasted_iota(I32, (LANE, tm), 0) >= IDX_DIM
    for pair in range(IDX_HEADS // 2):
        acc_t = _dot(c_q, wqi_ref[:, pair * LANE:(pair + 1) * LANE]).T
        qit_ref[0, 2 * pair] = jnp.where(value_row, 0.0, acc_t).astype(BF16)
        qit_ref[0, 2 * pair + 1] = jnp.where(value_row, 0.0, pltpu.roll(acc_t, IDX_DIM, axis=0)).astype(BF16)
    for pair in range(N_MIX_HEADS // 2):
        q_pair = (_dot(c_q, wqu_ref[:, pair * LANE:(pair + 1) * LANE]) * HEAD_DIM ** -0.5).astype(BF16)
        for h in (2 * pair, 2 * pair + 1):
            qat_ref[0, h] = (_dot(q_pair, wuk_ref[h]) * LOG2E).T.astype(BF16)


def _dsa_proj(h, gain, w, qn, kvn, kin, wqu, wuk, wqi, tm):
    bsz, t, d = h.shape
    nkt = t // TK
    full = lambda a: pl.BlockSpec(a.shape, lambda b, i: (0,) * a.ndim)
    row = lambda n: pl.BlockSpec((1, tm, n), lambda b, i: (b, i, 0))
    return pl.pallas_call(
        _dsa_proj_kernel,
        grid=(bsz, t // tm),
        in_specs=[pl.BlockSpec((1, tm, d), lambda b, i: (b, i, 0)), full(gain), full(w), full(qn),
                  full(kvn), full(kin), full(wqu), full(wuk), full(wqi)],
        out_specs=[pl.BlockSpec((1, N_MIX_HEADS, LANE, tm), lambda b, i: (b, 0, 0, i)),
                   pl.BlockSpec((1, IDX_HEADS, LANE, tm), lambda b, i: (b, 0, 0, i)),
                   pl.BlockSpec((1, IDX_HEADS, tm), lambda b, i: (b, 0, i)),
                   row(LANE),
                   pl.BlockSpec((1, tm // TK, DSA_VT_ROWS, TK), lambda b, i: (b, i, 0, 0)),
                   row(LANE), row(N_MEM_HEADS * LANE)],
        out_shape=[jax.ShapeDtypeStruct((bsz, N_MIX_HEADS, LANE, t), BF16),
                   jax.ShapeDtypeStruct((bsz, IDX_HEADS, LANE, t), BF16),
                   jax.ShapeDtypeStruct((bsz, IDX_HEADS, t), F32),
                   jax.ShapeDtypeStruct((bsz, t, LANE), BF16),
                   jax.ShapeDtypeStruct((bsz, nkt, DSA_VT_ROWS, TK), BF16),
                   jax.ShapeDtypeStruct((bsz, t, LANE), BF16),
                   jax.ShapeDtypeStruct((bsz, t, N_MEM_HEADS * LANE), BF16)],
        compiler_params=pltpu.CompilerParams(dimension_semantics=("arbitrary", "arbitrary"),
                                             vmem_limit_bytes=VMEM_LIMIT),
        name="dsa_proj",
    )(h, gain, w, qn, kvn, kin, wqu, wuk, wqi)


def _dsa_kernel(qi_ref, wi_ref, qa_ref, kidx_ref, ckv_ref, ckvt_ref, tab_ref, o_ref,
                sc_ref, sc_hi_ref, negm_ref, s_ref, tmax_ref, m_ref, l_ref, acc_ref, *, topk):
    i = pl.program_id(1)
    qs = i * TQ
    n_tiles = (qs + TQ - 1) // TK + 1
    sub = TK // KS
    n_steps = (qs + TQ - 1) // SCORE_TK + 1
    per_step = SCORE_TK // KS
    t_row = qs + lax.broadcasted_iota(I32, (1, TQ), 1)
    krow = lax.broadcasted_iota(I32, (KS, TQ), 0)
    k_f = float(topk)

    grouped = lambda x: x.reshape(KS // SUBLANE, SUBLANE, TQ)

    def score_body(ks, carry):
        for a in range(per_step):
            k0 = pl.multiple_of(ks * SCORE_TK + a * KS, KS)
            kk = kidx_ref[0, pl.ds(k0, KS), :]
            sc = jnp.maximum(_dot(kk, qi_ref[0, 0]), 0.0) * wi_ref[0, 0:1, :]
            for h in range(1, IDX_HEADS):
                sc = sc + jnp.maximum(_dot(kk, qi_ref[0, h]), 0.0) * wi_ref[0, h:h + 1, :]
            sc = jnp.where((k0 + krow) <= t_row, sc, NEG)
            sc_ref[per_step * ks + a] = sc
            sc_hi_ref[per_step * ks + a] = sc.astype(BF16)
        return carry

    lax.fori_loop(0, n_steps, score_body, 0)

    def count(pred):
        def body(ks, acc):
            for a in range(per_step):
                hit = pred(sc_ref[per_step * ks + a], ks * SCORE_TK + a * KS).astype(F32)
                acc = acc + jnp.sum(grouped(hit), axis=0)
            return acc
        acc = lax.fori_loop(0, n_steps, body, jnp.zeros((SUBLANE, TQ), F32))
        return jnp.sum(acc, axis=0, keepdims=True)

    def count_rounded(cand_f):
        cand = jnp.broadcast_to(cand_f, (BF16_ROWS, TQ)).astype(BF16)
        one, zero = jnp.ones((), BF16), jnp.zeros((), BF16)

        def body(ks, acc):
            for a in range(per_step):
                tile = sc_hi_ref[per_step * ks + a].reshape(KS // BF16_ROWS, BF16_ROWS, TQ)
                hit = jnp.where(tile >= cand[None], one, zero)
                part = functools.reduce(jnp.add, [hit[r] for r in range(KS // BF16_ROWS)])
                acc = acc + part.astype(F32)
            return acc
        acc = lax.fori_loop(0, n_steps, body, jnp.zeros((BF16_ROWS, TQ), F32))
        return jnp.sum(acc, axis=0, keepdims=True)

    short = t_row < topk
    half_step = 1 << 15

    def bf16_key_to_float(v16):
        raw16 = jnp.where(v16 >= 0, v16, v16 ^ jnp.int32(0x7FFF))
        return pltpu.bitcast(jnp.left_shift(raw16, 16), F32)

    def high_body(it, v16):
        cand = v16 + jnp.left_shift(jnp.int32(1), 15 - it)
        return jnp.where(count_rounded(bf16_key_to_float(cand)) >= k_f, cand, v16)

    v16 = lax.fori_loop(0, 16, high_body, jnp.full((1, TQ), -(1 << 15), I32))
    key_g = jnp.where(v16 >= 0, jnp.left_shift(v16, 16), jnp.left_shift(v16, 16) | jnp.int32(0xFFFF))

    def low_pass(bit, v, cnt_v):
        cand = v + jnp.left_shift(jnp.int32(1), bit)
        cand_f = _key_to_float(cand)
        cnt = count(lambda sc, k0: sc >= cand_f)
        take = cnt >= k_f
        return jnp.where(take, cand, v), jnp.where(take, cnt, cnt_v)

    def open_count(cnt_v):
        return jnp.sum(((cnt_v != k_f) & ~short).astype(I32))

    passes_per_check = 4
    v, cnt_v = low_pass(16, key_g - half_step, jnp.full((1, TQ), 1e9, F32))

    def v_cond(c):
        return (c[0] >= 0) & (c[3] > 0)

    def v_body(c):
        bit, v, cnt_v, _ = c
        for step in range(passes_per_check):
            v, cnt_v = low_pass(bit - step, v, cnt_v)
        return bit - passes_per_check, v, cnt_v, open_count(cnt_v)

    _, v, cnt_v, open_cols = lax.while_loop(v_cond, v_body, (jnp.int32(15), v, cnt_v, open_count(cnt_v)))
    thr = _key_to_float(v)

    def tie_cut(_):
        need = k_f - count(lambda sc, k0: sc > thr)

        def c_body(it, c):
            cand = c + jnp.left_shift(jnp.int32(1), 13 - it)
            cnt = count(lambda sc, k0: (sc == thr) & ((k0 + krow) < cand))
            return jnp.where(cnt < need, cand, c)

        return lax.fori_loop(0, 14, c_body, jnp.zeros((1, TQ), I32))

    cut = lax.cond(open_cols > 0, tie_cut, lambda _: jnp.full((1, TQ), 2 ** 30, I32), 0)

    def att_mask(kt):
        for a in range(sub):
            sc = sc_ref[sub * kt + a]
            kpos = kt * TK + a * KS + krow
            chosen = short | (sc > thr) | ((sc == thr) & (kpos <= cut))
            negm_ref[a * KS:(a + 1) * KS, :] = jnp.where(chosen & (kpos <= t_row), 0.0, -jnp.inf)

    _attend_tiles(N_MIX_HEADS, n_tiles, lambda h: qa_ref[0, h],
                  lambda kt: ckv_ref[0, pl.ds(pl.multiple_of(kt * TK, TK), TK), :],
                  lambda kt: ckvt_ref[0, kt],
                  lambda kt: (lambda h: _bias_block(lambda m: tab_ref[m, h],
                                                    (TQ // KS) * i - (TK // KS) * kt)),
                  att_mask, negm_ref, s_ref, tmax_ref, m_ref, l_ref, acc_ref)
    for h in range(N_MIX_HEADS):
        out = acc_ref[h] * (1.0 / l_ref[h])
        o_ref[0, :, h * LANE:(h + 1) * LANE] = out.T.astype(o_ref.dtype)


def _dsa_attention(qit, wit, qat, kidx, ckv, ckvt, tab, topk):
    bsz, _, _, t = qat.shape
    nkt = t // TK
    return pl.pallas_call(
        functools.partial(_dsa_kernel, topk=topk),
        grid=(bsz, t // TQ),
        in_specs=[pl.BlockSpec((1, IDX_HEADS, LANE, TQ), lambda b, i: (b, 0, 0, i)),
                  pl.BlockSpec((1, IDX_HEADS, TQ), lambda b, i: (b, 0, i)),
                  pl.BlockSpec((1, N_MIX_HEADS, LANE, TQ), lambda b, i: (b, 0, 0, i)),
                  pl.BlockSpec((1, t, LANE), lambda b, i: (b, 0, 0), pipeline_mode=pl.Buffered(1)),
                  pl.BlockSpec((1, t, LANE), lambda b, i: (b, 0, 0), pipeline_mode=pl.Buffered(1)),
                  pl.BlockSpec((1, nkt, DSA_VT_ROWS, TK), lambda b, i: (b, 0, 0, 0),
                               pipeline_mode=pl.Buffered(1)),
                  pl.BlockSpec(tab.shape, lambda b, i: (0, 0, 0, 0), pipeline_mode=pl.Buffered(1))],
        out_specs=pl.BlockSpec((1, TQ, N_MIX_HEADS * LANE), lambda b, i: (b, i, 0)),
        out_shape=jax.ShapeDtypeStruct((bsz, t, N_MIX_HEADS * LANE), BF16),
        scratch_shapes=[pltpu.VMEM((t // KS, KS, TQ), F32), pltpu.VMEM((t // KS, KS, TQ), BF16),
                        pltpu.VMEM((TK, TQ), F32),
                        pltpu.VMEM((N_MIX_HEADS, TK, TQ), F32), pltpu.VMEM((N_MIX_HEADS, 1, TQ), F32),
                        pltpu.VMEM((N_MIX_HEADS, 1, TQ), F32), pltpu.VMEM((N_MIX_HEADS, 1, TQ), F32),
                        pltpu.VMEM((N_MIX_HEADS, DSA_VT_ROWS, TQ), F32)],
        compiler_params=pltpu.CompilerParams(dimension_semantics=("arbitrary", "arbitrary"),
                                             vmem_limit_bytes=VMEM_LIMIT),
        name="dsa_attention",
    )(qit, wit, qat, kidx, ckv, ckvt, tab)


def _tail_kernel(h_ref, mix_ref, qmem_ref, kvm_ref, *rest, has_uv, final_norm):
    if has_uv:
        wuv_ref, *rest = rest
    wmix_ref, wmem_ref, g_ref, wg_ref, wu_ref, wd_ref, gf_ref, o_ref = rest
    mix = mix_ref[0]
    if has_uv:
        mix = _dot(mix, wuv_ref[...]).astype(BF16)
    upd = _dot(mix, wmix_ref[...])
    qm = qmem_ref[0]
    for hm in range(N_MEM_HEADS):
        sl = slice(hm * LANE, (hm + 1) * LANE)
        kv = kvm_ref[0, :, sl]
        s = _dot_nt(qm[:, sl], kv)
        e = jnp.exp(s - jnp.max(s, axis=-1, keepdims=True))
        p = e / jnp.sum(e, axis=-1, keepdims=True)
        o_h = _dot(p.astype(BF16), kv).astype(BF16)
        upd = upd + _dot(o_h, wmem_ref[sl, :])
    h = h_ref[0] + upd
    hn = _rms(h, g_ref[...]).astype(BF16)
    act = (jax.nn.silu(_dot(hn, wg_ref[...])) * _dot(hn, wu_ref[...])).astype(BF16)
    out = h + _dot(act, wd_ref[...])
    if final_norm:
        out = _rms(out, gf_ref[...])
    o_ref[0] = out


def _layer_tail(h, mix, qmem, kvm, w_uv, w_mix, w_mem, gain, wg, wu, wd, gain_final, final_norm, tm):
    bsz, t, d = h.shape
    has_uv = w_uv is not None
    const = lambda a: pl.BlockSpec(a.shape, lambda b, i: (0,) * a.ndim, pipeline_mode=pl.Buffered(1))
    weights = ([w_uv] if has_uv else []) + [w_mix, w_mem, gain, wg, wu, wd, gain_final]
    return pl.pallas_call(
        functools.partial(_tail_kernel, has_uv=has_uv, final_norm=final_norm),
        grid=(bsz, t // tm),
        in_specs=[pl.BlockSpec((1, tm, d), lambda b, i: (b, i, 0)),
                  pl.BlockSpec((1, tm, mix.shape[2]), lambda b, i: (b, i, 0)),
                  pl.BlockSpec((1, tm, qmem.shape[2]), lambda b, i: (b, i, 0)),
                  pl.BlockSpec((1,) + kvm.shape[1:], lambda b, i: (b, 0, 0))]
                 + [const(w) for w in weights],
        out_specs=pl.BlockSpec((1, tm, d), lambda b, i: (b, i, 0)),
        out_shape=jax.ShapeDtypeStruct((bsz, t, d), F32),
        compiler_params=pltpu.CompilerParams(dimension_semantics=("arbitrary", "arbitrary"),
                                             vmem_limit_bytes=VMEM_LIMIT),
        name="layer_tail",
    )(h, mix, qmem, kvm, *weights)


def _pad_cols(w, n):
    return jnp.pad(w, ((0, 0), (0, n - w.shape[1])))


def _value_rows(w_rows, n_heads):
    d_out = w_rows.shape[1]
    w = w_rows.reshape(n_heads, HEAD_DIM, d_out)
    return jnp.pad(w, ((0, 0), (LANE - HEAD_DIM, 0), (0, 0))).reshape(n_heads * LANE, d_out)


def _interleave_kv(k, v, n_heads):
    d_in = k.shape[0]
    kv = jnp.concatenate([k.reshape(d_in, n_heads, HEAD_DIM), v.reshape(d_in, n_heads, HEAD_DIM)], axis=2)
    return kv.reshape(d_in, n_heads * LANE)


def _bucket_bias(rel_bias, bucket_np):
    onehot = jax.nn.one_hot(jnp.asarray(bucket_np.reshape(-1)), N_REL_BUCKETS, dtype=F32)
    out = jnp.dot(onehot, rel_bias, precision=lax.Precision.HIGHEST) * LOG2E
    return out.reshape(bucket_np.shape + (rel_bias.shape[1],))


def _bias_tiles(rel_bias):
    m = np.arange(N_BIAS_TILES)[:, None, None]
    dist = LANE * m + np.arange(LANE)[None, None, :] - np.arange(KS)[None, :, None]
    return jnp.transpose(_bucket_bias(rel_bias, _rel_bucket_np(dist)), (0, 3, 1, 2))


def _bias_cmp_table(rel_bias):
    rel = np.arange(2 * CMP_PAD) - CMP_PAD
    dist = np.arange(TQ)[None, :] - CMP_STRIDE * rel[:, None] - (CMP_LEN - 1)
    return jnp.transpose(_bucket_bias(rel_bias, _rel_bucket_np(dist)), (2, 0, 1))


def _overlap_matrix(t):
    n_cmp = (t - CMP_LEN) // CMP_STRIDE + 1
    n_sel = t // SEL_BLOCK
    cs = np.arange(CMP_PAD) * CMP_STRIDE
    ss = np.arange(LANE) * SEL_BLOCK
    ov = (cs[None, :] <= ss[:, None] + SEL_BLOCK - 1) & (cs[None, :] + CMP_LEN - 1 >= ss[:, None])
    ov &= (np.arange(CMP_PAD) < n_cmp)[None, :] & (np.arange(LANE) < n_sel)[:, None]
    return jnp.asarray(ov, BF16)


def kernel(x, mem, rel_bias, norm_mix, norm_ffn, norm_mem, w_mem_kv, w_out, ffn_gate, ffn_up, ffn_down,
           nsa_w_in, nsa_gate_b, nsa_cmp_pos_k, nsa_cmp_pos_v,
           nsa_cmp_k_w1, nsa_cmp_k_b1, nsa_cmp_k_w2, nsa_cmp_k_b2,
           nsa_cmp_v_w1, nsa_cmp_v_b1, nsa_cmp_v_w2, nsa_cmp_v_b2,
           dsa_w_in, dsa_q_norm, dsa_kv_norm, dsa_w_q_up, dsa_w_uk, dsa_w_uv, dsa_w_q_idx, dsa_kidx_norm,
           norm_final):
    bsz, t, d = x.shape
    m_len = mem.shape[1]
    depth = norm_mix.shape[0]
    g, r, hh = NSA_KV_HEADS, NSA_GROUP, N_MIX_HEADS
    d_mix = hh * HEAD_DIM
    kvw_ = g * HEAD_DIM
    assert t % 1024 == 0 and t // SEL_BLOCK <= LANE and t // CMP_STRIDE <= CMP_PAD
    tm = 512
    h = x.astype(F32)

    tab = _bias_tiles(rel_bias)
    tab_nsa = jnp.transpose(tab.reshape(N_BIAS_TILES, g, r, KS, LANE), (1, 0, 2, 3, 4))
    tabc = _bias_cmp_table(rel_bias).reshape(g, r, 2 * CMP_PAD, TQ)
    ovl = _overlap_matrix(t)

    for layer in range(depth):
        j = layer // 2
        wkv = w_mem_kv[layer]
        wkv = _interleave_kv(wkv[:, :N_MEM_HEADS * HEAD_DIM], wkv[:, N_MEM_HEADS * HEAD_DIM:], N_MEM_HEADS)
        kvm = _rms_proj(mem.reshape(bsz * m_len, d), norm_mem[layer], wkv.astype(BF16), BF16, m_len)
        kvm = kvm.reshape(bsz, m_len, N_MEM_HEADS * LANE)
        w_o = w_out[layer]
        w_mem_o = _value_rows(w_o[d_mix:], N_MEM_HEADS).astype(BF16)
        w_mix = w_o[:d_mix].astype(BF16)

        if layer % 2 == 0:
            w = nsa_w_in[j]
            c = np.cumsum([0, d_mix, kvw_, kvw_, kvw_, kvw_, kvw_, kvw_, hh * N_BRANCH,
                           N_MEM_HEADS * HEAD_DIM])
            wq, wkc, wvc, wks, wvs, wkw, wvw, wgl, wqm = [w[:, c[k]:c[k + 1]] for k in range(9)]
            n_g = r * N_BRANCH
            wgl = jnp.concatenate([_pad_cols(wgl[:, gg * n_g:(gg + 1) * n_g], LANE) for gg in range(g)], 1)
            w_all = jnp.concatenate([wq, wkc, wvc, _interleave_kv(wks, wvs, g),
                                     _interleave_kv(wkw, wvw, g), wgl, wqm], axis=1)
            qt, cmp_raw, kvs, vst, kvw, vwt, glt, qmem = _nsa_proj(h, norm_mix[layer], w_all.astype(BF16), tm)
            nc = t // CMP_STRIDE
            x2 = cmp_raw.reshape(bsz, 2 * g, nc, CMP_STRIDE * HEAD_DIM)
            pos = jnp.stack([nsa_cmp_pos_k[j], nsa_cmp_pos_v[j]]).reshape(2, 2, CMP_STRIDE * HEAD_DIM)
            w1 = jnp.stack([nsa_cmp_k_w1[j], nsa_cmp_v_w1[j]]).astype(BF16)
            b1 = jnp.stack([nsa_cmp_k_b1[j], nsa_cmp_v_b1[j]])[:, None, :]
            w2 = jnp.stack([jnp.pad(nsa_cmp_k_w2[j], ((0, 0), (0, HEAD_DIM))),
                            jnp.pad(nsa_cmp_v_w2[j], ((0, 0), (HEAD_DIM, 0)))]).astype(BF16)
            b2 = jnp.stack([jnp.pad(nsa_cmp_k_b2[j], (0, HEAD_DIM)),
                            jnp.pad(nsa_cmp_v_b2[j], (HEAD_DIM, 0))])[:, None, :]
            kvc = _compress(x2, pos, w1, b1, w2, b2)
            kvc = jnp.pad(kvc, ((0, 0), (0, 0), (0, CMP_PAD - nc), (0, 0)))
            vct = jnp.transpose(kvc, (0, 1, 3, 2))
            gb = jnp.pad(nsa_gate_b[j].reshape(g, n_g), ((0, 0), (0, GATE_ROWS - n_g)))
            gbt = jnp.broadcast_to(gb[:, :, None], (g, GATE_ROWS, TQ))
            mix = _nsa_attention(qt, kvc, vct, kvs, vst, kvw, vwt, glt, gbt, tab_nsa, tabc, ovl,
                                 t // SEL_BLOCK)
            w_uv = None
        else:
            w = dsa_w_in[j]
            c = np.cumsum([0, Q_LORA, KV_LORA, IDX_DIM, IDX_HEADS, N_MEM_HEADS * HEAD_DIM])
            wcq, wckv, wki, wwi, wqm = [w[:, c[k]:c[k + 1]] for k in range(5)]
            w_all = jnp.concatenate([wcq, wckv, _pad_cols(wki, LANE), _pad_cols(wwi, LANE),
                                     wqm], axis=1).astype(BF16)
            wuk = jnp.transpose(dsa_w_uk[j], (1, 2, 0))
            wuk = jnp.stack([jnp.pad(wuk[hd], ((HEAD_DIM * (hd % 2), HEAD_DIM * (1 - hd % 2)), (0, 0)))
                             for hd in range(hh)]).astype(BF16)
            qat, qit, wit, ckv, ckvt, kidx, qmem = _dsa_proj(
                h, norm_mix[layer][None], w_all, dsa_q_norm[j][None], dsa_kv_norm[j][None],
                _pad_cols(dsa_kidx_norm[j][None], LANE), dsa_w_q_up[j].astype(BF16), wuk,
                dsa_w_q_idx[j].astype(BF16), tm)
            mix = _dsa_attention(qit, wit, qat, kidx, ckv, ckvt, tab, min(DSA_TOPK, t // 4))
            wv = jnp.transpose(dsa_w_uv[j], (1, 0, 2))
            w_uv = (jnp.eye(hh, dtype=F32)[:, None, :, None] * wv[:, :, None, :]
                    ).reshape(hh * KV_LORA, d_mix).astype(BF16)

        h = _layer_tail(h, mix, qmem, kvm, w_uv, w_mix, w_mem_o, norm_ffn[layer][None],
                        ffn_gate[layer].astype(BF16), ffn_up[layer].astype(BF16),
                        ffn_down[layer].astype(BF16), norm_final[None], layer == depth - 1, tm)
    return h.astype(x.dtype)
```

```python
import functools
import math

import numpy as np
import jax
import jax.numpy as jnp
from jax import lax
from jax.experimental import pallas as pl
from jax.experimental.pallas import tpu as pltpu

F32 = jnp.float32
BF16 = jnp.bfloat16
I32 = jnp.int32

NEG = -1e30
EPS = 1e-6
LOG2E = math.log2(math.e)
LANE = 128
SUBLANE = 8
BF16_ROWS = 16
HEAD_DIM = 64
N_MIX_HEADS = 12
N_MEM_HEADS = 4
N_REL_BUCKETS = 32
REL_MAX_EXACT = 16
REL_MAX_DIST = 2048
NSA_KV_HEADS = 2
NSA_GROUP = N_MIX_HEADS // NSA_KV_HEADS
N_BRANCH = 3
GATE_ROWS = 24
CMP_LEN = 32
CMP_STRIDE = 16
SEL_BLOCK = 64
N_SEL = 16
WINDOW = 512
FORCE_BONUS = 1e4
Q_LORA = 256
KV_LORA = 128
IDX_HEADS = 8
IDX_DIM = 64
DSA_TOPK = 256
TQ = 256
TK = 256
SCORE_TK = 512
KS = LANE
N_BIAS_TILES = REL_MAX_DIST // KS + 2
CMP_PAD = 512
NSA_VT_ROWS = HEAD_DIM
DSA_VT_ROWS = KV_LORA
VMEM_LIMIT = 56 * 1024 * 1024
INT_MIN = -2 ** 31
F32_MIN = float(np.finfo(np.float32).min)


def _dot(a, b):
    return jnp.dot(a, b, preferred_element_type=F32)


def _dot_nt(a, b):
    return lax.dot_general(a, b, (((1,), (1,)), ((), ())), preferred_element_type=F32)


def _rms(x, gain, n=None):
    n = x.shape[-1] if n is None else n
    ms = jnp.sum(x * x, axis=-1, keepdims=True) * (1.0 / n)
    return x * lax.rsqrt(ms + EPS) * gain


def _rel_bucket_np(dist):
    n = np.maximum(dist, 0)
    nf = np.maximum(n, REL_MAX_EXACT).astype(np.float32)
    large = REL_MAX_EXACT + (np.log(nf / np.float32(REL_MAX_EXACT))
                             / np.float32(math.log(REL_MAX_DIST / REL_MAX_EXACT))
                             * np.float32(N_REL_BUCKETS - REL_MAX_EXACT)).astype(np.int32)
    large = np.minimum(large, N_REL_BUCKETS - 1)
    return np.where(n < REL_MAX_EXACT, n, large).astype(np.int32)


def _spread_heads(x):
    rows, n = x.shape
    low = lax.broadcasted_iota(I32, (rows, LANE), 1) < HEAD_DIM
    out = []
    for j in range(n // LANE):
        pair = x[:, j * LANE:(j + 1) * LANE]
        out += [jnp.where(low, pair, 0.0), jnp.where(low, pltpu.roll(pair, HEAD_DIM, axis=1), 0.0)]
    return jnp.concatenate(out, axis=1)


def _colsum(x):
    rows, n = x.shape
    return jnp.sum(jnp.sum(x.reshape(rows // SUBLANE, SUBLANE, n), axis=0), axis=0, keepdims=True)


def _key_to_float(v):
    bits = jnp.where(v >= 0, v, v ^ jnp.int32(0x7FFFFFFF))
    return pltpu.bitcast(bits, F32)


def _topk_cols(score, k, row_idx):
    n = score.shape[1]

    def vbody(it, v):
        cand = v + jnp.left_shift(jnp.int32(1), 31 - it)
        cnt = _colsum((score >= _key_to_float(cand)).astype(F32))
        return jnp.where(cnt >= k, cand, v)

    v = lax.fori_loop(0, 32, vbody, jnp.full((1, n), INT_MIN, I32))
    thr = _key_to_float(v)
    gt = score > thr
    eq = score == thr
    need = k - _colsum(gt.astype(F32))

    def cbody(it, c):
        cand = c + jnp.left_shift(jnp.int32(1), 6 - it)
        cnt = _colsum((eq & (row_idx < cand)).astype(F32))
        return jnp.where(cnt < need, cand, c)

    c = lax.fori_loop(0, 7, cbody, jnp.zeros((1, n), I32))
    return gt | (eq & (row_idx <= c))


def _bias_block(tile_of, m0):
    idx = lambda m: jnp.clip(m, 0, N_BIAS_TILES - 1)
    return [jnp.concatenate([tile_of(idx(m0 + b - a)) for b in range(TQ // LANE)], axis=1)
            for a in range(TK // KS)]


def _attend_tiles(n_heads, n_tiles, q_of, kv_of, vt_of, bias_of, fill_mask, negm_ref, s_ref, tmax_ref,
                  m_ref, l_ref, acc_ref):
    parts = [slice(a * KS, (a + 1) * KS) for a in range(TK // KS)]
    m_ref[...] = jnp.full(m_ref.shape, F32_MIN, F32)
    l_ref[...] = jnp.zeros(l_ref.shape, F32)
    acc_ref[...] = jnp.zeros(acc_ref.shape, F32)

    def logits_of(j):
        fill_mask(j)
        kv = kv_of(j)
        bias_h = bias_of(j)

        def run(h):
            bias = bias_h(h)
            tmax = None
            for a, sl in enumerate(parts):
                s_a = _dot(kv[sl], q_of(h)) + bias[a] + negm_ref[sl, :]
                s_ref[h, sl, :] = s_a
                mx = jnp.max(s_a, axis=0, keepdims=True)
                tmax = mx if tmax is None else jnp.maximum(tmax, mx)
            tmax_ref[h] = tmax
        return run

    def accumulate(j):
        vt = vt_of(j)

        def run(h):
            m_old = m_ref[h]
            m_new = jnp.maximum(m_old, tmax_ref[h])
            alpha = jnp.exp2(m_old - m_new)
            m_ref[h] = m_new
            p = [jnp.exp2(s_ref[h, sl, :] - m_new) for sl in parts]
            l_ref[h] = alpha * l_ref[h] + _colsum(sum(p[1:], p[0]))
            acc = alpha * acc_ref[h]
            for sl, p_a in zip(parts, p):
                acc = acc + _dot(vt[:, sl], p_a.astype(BF16))
            acc_ref[h] = acc
        return run

    first = logits_of(0)
    for h in range(n_heads):
        first(h)

    def body(j, carry):
        consume = accumulate(j)
        produce = logits_of(jnp.minimum(j + 1, n_tiles - 1))
        for h in range(n_heads):
            consume(h)
            produce(h)
        return carry

    lax.fori_loop(0, n_tiles, body, 0)


def _rms_proj_kernel(x_ref, g_ref, w_ref, o_ref):
    y = _rms(x_ref[...], g_ref[...]).astype(BF16)
    o_ref[...] = _dot(y, w_ref[...]).astype(o_ref.dtype)


def _rms_proj(x2d, gain, w_bf16, out_dtype, tm):
    m, d = x2d.shape
    n = w_bf16.shape[1]
    return pl.pallas_call(
        _rms_proj_kernel,
        grid=(m // tm,),
        in_specs=[pl.BlockSpec((tm, d), lambda i: (i, 0)),
                  pl.BlockSpec((1, d), lambda i: (0, 0)),
                  pl.BlockSpec((d, n), lambda i: (0, 0))],
        out_specs=pl.BlockSpec((tm, n), lambda i: (i, 0)),
        out_shape=jax.ShapeDtypeStruct((m, n), out_dtype),
        compiler_params=pltpu.CompilerParams(dimension_semantics=("arbitrary",),
                                             vmem_limit_bytes=VMEM_LIMIT),
        name="rms_proj",
    )(x2d, gain.reshape(1, d), w_bf16)


def _nsa_proj_kernel(x_ref, g_ref, w_ref, qt_ref, cmp_ref, kvs_ref, vst_ref, kvw_ref, vwt_ref,
                     glt_ref, qmem_ref):
    tm = x_ref.shape[1]
    y = _rms(x_ref[0], g_ref[...]).astype(BF16)
    value_row = lax.broadcasted_iota(I32, (LANE, tm), 0) >= HEAD_DIM
    off = 0
    for pair in range(N_MIX_HEADS // 2):
        acc_t = (_dot(y, w_ref[:, off:off + LANE]) * (HEAD_DIM ** -0.5 * LOG2E)).T
        qt_ref[0, 2 * pair] = jnp.where(value_row, 0.0, acc_t).astype(BF16)
        qt_ref[0, 2 * pair + 1] = jnp.where(value_row, 0.0, pltpu.roll(acc_t, HEAD_DIM, axis=0)).astype(BF16)
        off += LANE
    raw = _dot(y, w_ref[:, off:off + 2 * LANE])
    for piece in range(2 * NSA_KV_HEADS):
        cmp_ref[0, piece] = raw[:, piece * HEAD_DIM:(piece + 1) * HEAD_DIM]
    off += 2 * LANE
    for kv_ref, vt_ref in ((kvs_ref, vst_ref), (kvw_ref, vwt_ref)):
        for g in range(NSA_KV_HEADS):
            acc = _dot(y, w_ref[:, off:off + LANE])
            off += LANE
            kv_ref[0, :, g * LANE:(g + 1) * LANE] = acc.astype(BF16)
            v_t = acc.T[HEAD_DIM:, :].astype(BF16)
            for jt in range(tm // TK):
                vt_ref[0, g, jt] = v_t[:, jt * TK:(jt + 1) * TK]
    for g in range(NSA_KV_HEADS):
        acc = _dot(y, w_ref[:, off:off + LANE])
        off += LANE
        glt_ref[0, g] = acc.T[:GATE_ROWS, :]
    qmem_ref[0] = _spread_heads(_dot(y, w_ref[:, off:off + N_MEM_HEADS * HEAD_DIM]) * HEAD_DIM ** -0.5
                                ).astype(BF16)


def _nsa_proj(h, gain, w, tm):
    bsz, t, d = h.shape
    g = NSA_KV_HEADS
    nkt = t // TK
    row = lambda n: pl.BlockSpec((1, tm, n), lambda b, i: (b, i, 0))
    vt_spec = pl.BlockSpec((1, g, tm // TK, NSA_VT_ROWS, TK), lambda b, i: (b, 0, i, 0, 0))
    return pl.pallas_call(
        _nsa_proj_kernel,
        grid=(bsz, t // tm),
        in_specs=[pl.BlockSpec((1, tm, d), lambda b, i: (b, i, 0)),
                  pl.BlockSpec((1, d), lambda b, i: (0, 0)),
                  pl.BlockSpec(w.shape, lambda b, i: (0, 0))],
        out_specs=[pl.BlockSpec((1, N_MIX_HEADS, LANE, tm), lambda b, i: (b, 0, 0, i)),
                   pl.BlockSpec((1, 2 * g, tm, HEAD_DIM), lambda b, i: (b, 0, i, 0)),
                   row(g * LANE), vt_spec, row(g * LANE), vt_spec,
                   pl.BlockSpec((1, g, GATE_ROWS, tm), lambda b, i: (b, 0, 0, i)),
                   row(N_MEM_HEADS * LANE)],
        out_shape=[jax.ShapeDtypeStruct((bsz, N_MIX_HEADS, LANE, t), BF16),
                   jax.ShapeDtypeStruct((bsz, 2 * g, t, HEAD_DIM), F32),
                   jax.ShapeDtypeStruct((bsz, t, g * LANE), BF16),
                   jax.ShapeDtypeStruct((bsz, g, nkt, NSA_VT_ROWS, TK), BF16),
                   jax.ShapeDtypeStruct((bsz, t, g * LANE), BF16),
                   jax.ShapeDtypeStruct((bsz, g, nkt, NSA_VT_ROWS, TK), BF16),
                   jax.ShapeDtypeStruct((bsz, g, GATE_ROWS, t), F32),
                   jax.ShapeDtypeStruct((bsz, t, N_MEM_HEADS * LANE), BF16)],
        compiler_params=pltpu.CompilerParams(dimension_semantics=("arbitrary", "arbitrary"),
                                             vmem_limit_bytes=VMEM_LIMIT),
        name="nsa_proj",
    )(h, gain.reshape(1, d), w)


def _compress_kernel(xk_ref, xv_ref, pos_ref, w1_ref, b1_ref, w2_ref, b2_ref, o_ref):
    nc = xk_ref.shape[2]
    half = xk_ref.shape[3]
    out = None
    for j, x_ref in enumerate((xk_ref, xv_ref)):
        x = x_ref[0, 0]
        top = _dot((x + pos_ref[j, 0:1, :]).astype(BF16), w1_ref[j, :half, :])
        bot = _dot((x + pos_ref[j, 1:2, :]).astype(BF16), w1_ref[j, half:, :])
        pre = top + pltpu.roll(bot, nc - 1, axis=0) + b1_ref[j]
        hid = jax.nn.gelu(pre)
        res = _dot(hid.astype(BF16), w2_ref[j]) + b2_ref[j]
        out = res if out is None else out + res
    o_ref[0, 0] = out.astype(o_ref.dtype)


def _compress(x2, pos, w1, b1, w2, b2):
    bsz, _, nc, width = x2.shape
    g = NSA_KV_HEADS
    return pl.pallas_call(
        _compress_kernel,
        grid=(bsz, g),
        in_specs=[pl.BlockSpec((1, 1, nc, width), lambda b, gg: (b, gg, 0, 0)),
                  pl.BlockSpec((1, 1, nc, width), lambda b, gg: (b, gg + NSA_KV_HEADS, 0, 0)),
                  pl.BlockSpec(pos.shape, lambda b, gg: (0, 0, 0)),
                  pl.BlockSpec(w1.shape, lambda b, gg: (0, 0, 0)),
                  pl.BlockSpec(b1.shape, lambda b, gg: (0, 0, 0)),
                  pl.BlockSpec(w2.shape, lambda b, gg: (0, 0, 0)),
                  pl.BlockSpec(b2.shape, lambda b, gg: (0, 0, 0))],
        out_specs=pl.BlockSpec((1, 1, nc, LANE), lambda b, gg: (b, gg, 0, 0)),
        out_shape=jax.ShapeDtypeStruct((bsz, g, nc, LANE), BF16),
        compiler_params=pltpu.CompilerParams(dimension_semantics=("arbitrary", "arbitrary"),
                                             vmem_limit_bytes=VMEM_LIMIT),
        name="nsa_compress",
    )(x2, x2, pos, w1, b1, w2, b2)


def _nsa_kernel(q_ref, kvc_ref, vct_ref, kvs_ref, vst_ref, kvw_ref, vwt_ref, glt_ref, gbt_ref,
                tab_ref, tabc_ref, ovl_ref, o_ref,
                psum_ref, negsel_ref, negm_ref, s_ref, tmax_ref, m_ref, l_ref, acc_ref, ocmp_ref, oslc_ref,
                *, n_sel):
    r_heads = NSA_GROUP
    i = pl.program_id(2)
    qs = i * TQ
    t_row = qs + lax.broadcasted_iota(I32, (1, TQ), 1)
    krow = lax.broadcasted_iota(I32, (TK, TQ), 0)
    q_of = lambda h: q_ref[0, h]

    def normalised(h):
        return acc_ref[h] * (1.0 / l_ref[h])

    kvc = kvc_ref[0, 0]
    vct = vct_ref[0, 0]
    n_row = lax.broadcasted_iota(I32, (CMP_PAD, TQ), 0)
    negc = jnp.where((CMP_STRIDE * n_row + (CMP_LEN - 1)) <= t_row, 0.0, -jnp.inf)
    j0 = pl.multiple_of(CMP_PAD - (TQ // CMP_STRIDE) * i, SUBLANE)
    psum_ref[...] = jnp.zeros(psum_ref.shape, F32)

    cparts = [slice(a * KS, (a + 1) * KS) for a in range(CMP_PAD // KS)]
    for h in range(r_heads):
        s = [_dot(kvc[sl], q_of(h)) + tabc_ref[0, h, pl.ds(j0 + sl.start, KS), :] + negc[sl] for sl in cparts]
        m = functools.reduce(jnp.maximum, [jnp.max(s_a, axis=0, keepdims=True) for s_a in s])
        m = jnp.maximum(m, F32_MIN)
        e = [jnp.exp2(s_a - m) for s_a in s]
        den = _colsum(functools.reduce(jnp.add, e))
        inv = 1.0 / jnp.where(den > 0.0, den, 1.0)
        acc = functools.reduce(jnp.add, [_dot(vct[:, sl], e_a.astype(BF16)) for sl, e_a in zip(cparts, e)])
        ocmp_ref[h] = acc[LANE - HEAD_DIM:] * inv
        for sl, e_a in zip(cparts, e):
            psum_ref[sl, :] += e_a * inv

    psum = psum_ref[...]
    ovl = ovl_ref[...]
    hi = psum.astype(BF16)
    rem1 = psum - hi.astype(F32)
    mid = rem1.astype(BF16)
    lo = (rem1 - mid.astype(F32)).astype(BF16)
    p_slc = _dot(ovl, hi) + _dot(ovl, mid) + _dot(ovl, lo)
    blk = lax.broadcasted_iota(I32, (LANE, TQ), 0)
    cur = jnp.right_shift(t_row, 6)
    forced = (blk == 0) | (blk == cur) | (blk == cur - 1)
    admissible = (blk * SEL_BLOCK) <= t_row
    score = jnp.where(admissible, p_slc + jnp.where(forced, FORCE_BONUS, 0.0), NEG)
    score = jnp.where(blk < n_sel, score, -jnp.inf)
    sel = _topk_cols(score, min(N_SEL, n_sel), blk) & (score > 0.5 * NEG)
    negsel_ref[...] = jnp.where(sel, 0.0, -jnp.inf)

    def bias_of(kt):
        return lambda h: _bias_block(lambda m: tab_ref[0, m, h], (TQ // KS) * i - (TK // KS) * kt)

    def key_rows(ref, kt):
        return ref[0, pl.ds(pl.multiple_of(kt * TK, TK), TK), :]

    last_tile = (qs + TQ - 1) // TK

    def sel_mask(kt):
        rows = [jnp.broadcast_to(negsel_ref[pl.ds((TK // SEL_BLOCK) * kt + a, 1), :], (SEL_BLOCK, TQ))
                for a in range(TK // SEL_BLOCK)]
        negm_ref[...] = jnp.where((kt * TK + krow) <= t_row, jnp.concatenate(rows, axis=0), -jnp.inf)

    _attend_tiles(r_heads, last_tile + 1, q_of, lambda kt: key_rows(kvs_ref, kt), lambda kt: vst_ref[0, 0, kt],
                  bias_of, sel_mask, negm_ref, s_ref, tmax_ref, m_ref, l_ref, acc_ref)
    for h in range(r_heads):
        oslc_ref[h] = normalised(h)

    n_win = max((TQ * e + TQ - 1) // TK - (TQ * e - WINDOW + 1) // TK + 1 for e in range(max(TK // TQ, 1)))

    def win_mask(j):
        kt = last_tile - j
        dist = t_row - (jnp.maximum(kt, 0) * TK + krow)
        ok = (dist >= 0) & (dist < jnp.where(kt >= 0, WINDOW, 0))
        negm_ref[...] = jnp.where(ok, 0.0, -jnp.inf)

    _attend_tiles(r_heads, n_win, q_of, lambda j: key_rows(kvw_ref, jnp.maximum(last_tile - j, 0)),
                  lambda j: vwt_ref[0, 0, jnp.maximum(last_tile - j, 0)], lambda j: bias_of(last_tile - j),
                  win_mask, negm_ref, s_ref, tmax_ref, m_ref, l_ref, acc_ref)

    gates = jax.nn.sigmoid(glt_ref[0, 0] + gbt_ref[0])
    for pair in range(r_heads // 2):
        outs = []
        for r in (2 * pair, 2 * pair + 1):
            c = N_BRANCH * r
            outs.append(gates[c:c + 1] * ocmp_ref[r] + gates[c + 1:c + 2] * oslc_ref[r]
                        + gates[c + 2:c + 3] * normalised(r))
        o_ref[0, :, pair * LANE:(pair + 1) * LANE] = jnp.concatenate(outs, axis=0).T.astype(o_ref.dtype)


def _nsa_attention(qt, kvc, vct, kvs, vst, kvw, vwt, glt, gbt, tab, tabc, ovl, n_sel):
    bsz, _, _, t = qt.shape
    g, r = NSA_KV_HEADS, NSA_GROUP
    nkt = t // TK
    once = dict(pipeline_mode=pl.Buffered(1))
    return pl.pallas_call(
        functools.partial(_nsa_kernel, n_sel=n_sel),
        grid=(bsz, g, t // TQ),
        in_specs=[pl.BlockSpec((1, r, LANE, TQ), lambda b, gg, i: (b, gg, 0, i)),
                  pl.BlockSpec((1, 1, CMP_PAD, LANE), lambda b, gg, i: (b, gg, 0, 0)),
                  pl.BlockSpec((1, 1, LANE, CMP_PAD), lambda b, gg, i: (b, gg, 0, 0)),
                  pl.BlockSpec((1, t, LANE), lambda b, gg, i: (b, 0, gg)),
                  pl.BlockSpec((1, 1, nkt, NSA_VT_ROWS, TK), lambda b, gg, i: (b, gg, 0, 0, 0)),
                  pl.BlockSpec((1, t, LANE), lambda b, gg, i: (b, 0, gg)),
                  pl.BlockSpec((1, 1, nkt, NSA_VT_ROWS, TK), lambda b, gg, i: (b, gg, 0, 0, 0)),
                  pl.BlockSpec((1, 1, GATE_ROWS, TQ), lambda b, gg, i: (b, gg, 0, i)),
                  pl.BlockSpec((1, GATE_ROWS, TQ), lambda b, gg, i: (gg, 0, 0)),
                  pl.BlockSpec((1, N_BIAS_TILES, r, KS, LANE), lambda b, gg, i: (gg, 0, 0, 0, 0), **once),
                  pl.BlockSpec((1, r, 2 * CMP_PAD, TQ), lambda b, gg, i: (gg, 0, 0, 0), **once),
                  pl.BlockSpec((LANE, CMP_PAD), lambda b, gg, i: (0, 0))],
        out_specs=pl.BlockSpec((1, TQ, r * HEAD_DIM), lambda b, gg, i: (b, i, gg)),
        out_shape=jax.ShapeDtypeStruct((bsz, t, g * r * HEAD_DIM), BF16),
        scratch_shapes=[pltpu.VMEM((CMP_PAD, TQ), F32), pltpu.VMEM((LANE, TQ), F32),
                        pltpu.VMEM((TK, TQ), F32), pltpu.VMEM((r, TK, TQ), F32),
                        pltpu.VMEM((r, 1, TQ), F32), pltpu.VMEM((r, 1, TQ), F32), pltpu.VMEM((r, 1, TQ), F32),
                        pltpu.VMEM((r, NSA_VT_ROWS, TQ), F32), pltpu.VMEM((r, HEAD_DIM, TQ), F32),
                        pltpu.VMEM((r, HEAD_DIM, TQ), F32)],
        compiler_params=pltpu.CompilerParams(
            dimension_semantics=("arbitrary", "arbitrary", "arbitrary"),
            vmem_limit_bytes=VMEM_LIMIT),
        name="nsa_attention",
    )(qt, kvc, vct, kvs, vst, kvw, vwt, glt, gbt, tab, tabc, ovl)


def _dsa_proj_kernel(x_ref, g_ref, w_ref, qn_ref, kvn_ref, kin_ref, wqu_ref, wuk_ref, wqi_ref,
                     qat_ref, qit_ref, wit_ref, ckv_ref, ckvt_ref, kidx_ref, qmem_ref):
    tm = x_ref.shape[1]
    y = _rms(x_ref[0], g_ref[...]).astype(BF16)
    c_q = _rms(_dot(y, w_ref[:, 0:Q_LORA]), qn_ref[...]).astype(BF16)
    c_kv = _rms(_dot(y, w_ref[:, Q_LORA:Q_LORA + KV_LORA]), kvn_ref[...])
    ckv_ref[0] = c_kv.astype(BF16)
    c_kv_t = c_kv.T.astype(BF16)
    for jt in range(tm // TK):
        ckvt_ref[0, jt] = c_kv_t[:, jt * TK:(jt + 1) * TK]
    off = Q_LORA + KV_LORA
    k_idx = _rms(_dot(y, w_ref[:, off:off + LANE]), kin_ref[...], n=IDX_DIM)
    kidx_ref[0] = k_idx.astype(BF16)
    off += LANE
    w_idx = _dot(y, w_ref[:, off:off + LANE]) * (IDX_HEADS ** -0.5 * IDX_DIM ** -0.5)
    wit_ref[0] = w_idx.T[:IDX_HEADS, :]
    off += LANE
    qmem_ref[0] = _spread_heads(_dot(y, w_ref[:, off:off + N_MEM_HEADS * HEAD_DIM]) * HEAD_DIM ** -0.5
                                ).astype(BF16)
    value_row = lax.broadcasted_iota(I32, (LANE, tm), 0) >= IDX_DIM
    for pair in range(IDX_HEADS // 2):
        acc_t = _dot(c_q, wqi_ref[:, pair * LANE:(pair + 1) * LANE]).T
        qit_ref[0, 2 * pair] = jnp.where(value_row, 0.0, acc_t).astype(BF16)
        qit_ref[0, 2 * pair + 1] = jnp.where(value_row, 0.0, pltpu.roll(acc_t, IDX_DIM, axis=0)).astype(BF16)
    for pair in range(N_MIX_HEADS // 2):
        q_pair = (_dot(c_q, wqu_ref[:, pair * LANE:(pair + 1) * LANE]) * HEAD_DIM ** -0.5).astype(BF16)
        for h in (2 * pair, 2 * pair + 1):
            qat_ref[0, h] = (_dot(q_pair, wuk_ref[h]) * LOG2E).T.astype(BF16)


def _dsa_proj(h, gain, w, qn, kvn, kin, wqu, wuk, wqi, tm):
    bsz, t, d = h.shape
    nkt = t // TK
    full = lambda a: pl.BlockSpec(a.shape, lambda b, i: (0,) * a.ndim)
    row = lambda n: pl.BlockSpec((1, tm, n), lambda b, i: (b, i, 0))
    return pl.pallas_call(
        _dsa_proj_kernel,
        grid=(bsz, t // tm),
        in_specs=[pl.BlockSpec((1, tm, d), lambda b, i: (b, i, 0)), full(gain), full(w), full(qn),
                  full(kvn), full(kin), full(wqu), full(wuk), full(wqi)],
        out_specs=[pl.BlockSpec((1, N_MIX_HEADS, LANE, tm), lambda b, i: (b, 0, 0, i)),
                   pl.BlockSpec((1, IDX_HEADS, LANE, tm), lambda b, i: (b, 0, 0, i)),
                   pl.BlockSpec((1, IDX_HEADS, tm), lambda b, i: (b, 0, i)),
                   row(LANE),
                   pl.BlockSpec((1, tm // TK, DSA_VT_ROWS, TK), lambda b, i: (b, i, 0, 0)),
                   row(LANE), row(N_MEM_HEADS * LANE)],
        out_shape=[jax.ShapeDtypeStruct((bsz, N_MIX_HEADS, LANE, t), BF16),
                   jax.ShapeDtypeStruct((bsz, IDX_HEADS, LANE, t), BF16),
                   jax.ShapeDtypeStruct((bsz, IDX_HEADS, t), F32),
                   jax.ShapeDtypeStruct((bsz, t, LANE), BF16),
                   jax.ShapeDtypeStruct((bsz, nkt, DSA_VT_ROWS, TK), BF16),
                   jax.ShapeDtypeStruct((bsz, t, LANE), BF16),
                   jax.ShapeDtypeStruct((bsz, t, N_MEM_HEADS * LANE), BF16)],
        compiler_params=pltpu.CompilerParams(dimension_semantics=("arbitrary", "arbitrary"),
                                             vmem_limit_bytes=VMEM_LIMIT),
        name="dsa_proj",
    )(h, gain, w, qn, kvn, kin, wqu, wuk, wqi)


def _dsa_kernel(qi_ref, wi_ref, qa_ref, kidx_ref, ckv_ref, ckvt_ref, tab_ref, o_ref,
                sc_ref, sc_hi_ref, negm_ref, s_ref, tmax_ref, m_ref, l_ref, acc_ref, *, topk):
    i = pl.program_id(1)
    qs = i * TQ
    n_tiles = (qs + TQ - 1) // TK + 1
    sub = TK // KS
    n_steps = (qs + TQ - 1) // SCORE_TK + 1
    per_step = SCORE_TK // KS
    t_row = qs + lax.broadcasted_iota(I32, (1, TQ), 1)
    krow = lax.broadcasted_iota(I32, (KS, TQ), 0)
    k_f = float(topk)

    grouped = lambda x: x.reshape(KS // SUBLANE, SUBLANE, TQ)

    def score_body(ks, carry):
        for a in range(per_step):
            k0 = pl.multiple_of(ks * SCORE_TK + a * KS, KS)
            kk = kidx_ref[0, pl.ds(k0, KS), :]
            sc = jnp.maximum(_dot(kk, qi_ref[0, 0]), 0.0) * wi_ref[0, 0:1, :]
            for h in range(1, IDX_HEADS):
                sc = sc + jnp.maximum(_dot(kk, qi_ref[0, h]), 0.0) * wi_ref[0, h:h + 1, :]
            sc = jnp.where((k0 + krow) <= t_row, sc, NEG)
            sc_ref[per_step * ks + a] = sc
            sc_hi_ref[per_step * ks + a] = sc.astype(BF16)
        return carry

    lax.fori_loop(0, n_steps, score_body, 0)

    def count(pred):
        def body(ks, acc):
            for a in range(per_step):
                hit = pred(sc_ref[per_step * ks + a], ks * SCORE_TK + a * KS).astype(F32)
                acc = acc + jnp.sum(grouped(hit), axis=0)
            return acc
        acc = lax.fori_loop(0, n_steps, body, jnp.zeros((SUBLANE, TQ), F32))
        return jnp.sum(acc, axis=0, keepdims=True)

    def count_rounded(cand_f):
        cand = jnp.broadcast_to(cand_f, (BF16_ROWS, TQ)).astype(BF16)
        one, zero = jnp.ones((), BF16), jnp.zeros((), BF16)

        def body(ks, acc):
            for a in range(per_step):
                tile = sc_hi_ref[per_step * ks + a].reshape(KS // BF16_ROWS, BF16_ROWS, TQ)
                hit = jnp.where(tile >= cand[None], one, zero)
                part = functools.reduce(jnp.add, [hit[r] for r in range(KS // BF16_ROWS)])
                acc = acc + part.astype(F32)
            return acc
        acc = lax.fori_loop(0, n_steps, body, jnp.zeros((BF16_ROWS, TQ), F32))
        return jnp.sum(acc, axis=0, keepdims=True)

    short = t_row < topk
    half_step = 1 << 15

    def bf16_key_to_float(v16):
        raw16 = jnp.where(v16 >= 0, v16, v16 ^ jnp.int32(0x7FFF))
        return pltpu.bitcast(jnp.left_shift(raw16, 16), F32)

    def high_body(it, v16):
        cand = v16 + jnp.left_shift(jnp.int32(1), 15 - it)
        return jnp.where(count_rounded(bf16_key_to_float(cand)) >= k_f, cand, v16)

    v16 = lax.fori_loop(0, 16, high_body, jnp.full((1, TQ), -(1 << 15), I32))
    key_g = jnp.where(v16 >= 0, jnp.left_shift(v16, 16), jnp.left_shift(v16, 16) | jnp.int32(0xFFFF))

    def low_body(it, c):
        v, cnt_v = c
        cand = v + jnp.left_shift(jnp.int32(1), 16 - it)
        cand_f = _key_to_float(cand)
        cnt = count(lambda sc, k0: sc >= cand_f)
        take = cnt >= k_f
        return jnp.where(take, cand, v), jnp.where(take, cnt, cnt_v)

    v, cnt_v = lax.fori_loop(0, 17, low_body, (key_g - half_step, jnp.full((1, TQ), 1e9, F32)))
    thr = _key_to_float(v)
    open_cols = jnp.sum(((cnt_v != k_f) & ~short).astype(I32))

    def tie_cut(_):
        need = k_f - count(lambda sc, k0: sc > thr)

        def c_body(it, c):
            cand = c + jnp.left_shift(jnp.int32(1), 13 - it)
            cnt = count(lambda sc, k0: (sc == thr) & ((k0 + krow) < cand))
            return jnp.where(cnt < need, cand, c)

        return lax.fori_loop(0, 14, c_body, jnp.zeros((1, TQ), I32))

    cut = lax.cond(open_cols > 0, tie_cut, lambda _: jnp.full((1, TQ), 2 ** 30, I32), 0)

    def att_mask(kt):
        for a in range(sub):
            sc = sc_ref[sub * kt + a]
            kpos = kt * TK + a * KS + krow
            chosen = short | (sc > thr) | ((sc == thr) & (kpos <= cut))
            negm_ref[a * KS:(a + 1) * KS, :] = jnp.where(chosen & (kpos <= t_row), 0.0, -jnp.inf)

    _attend_tiles(N_MIX_HEADS, n_tiles, lambda h: qa_ref[0, h],
                  lambda kt: ckv_ref[0, pl.ds(pl.multiple_of(kt * TK, TK), TK), :],
                  lambda kt: ckvt_ref[0, kt],
                  lambda kt: (lambda h: _bias_block(lambda m: tab_ref[m, h],
                                                    (TQ // KS) * i - (TK // KS) * kt)),
                  att_mask, negm_ref, s_ref, tmax_ref, m_ref, l_ref, acc_ref)
    for h in range(N_MIX_HEADS):
        out = acc_ref[h] * (1.0 / l_ref[h])
        o_ref[0, :, h * LANE:(h + 1) * LANE] = out.T.astype(o_ref.dtype)


def _dsa_attention(qit, wit, qat, kidx, ckv, ckvt, tab, topk):
    bsz, _, _, t = qat.shape
    nkt = t // TK
    return pl.pallas_call(
        functools.partial(_dsa_kernel, topk=topk),
        grid=(bsz, t // TQ),
        in_specs=[pl.BlockSpec((1, IDX_HEADS, LANE, TQ), lambda b, i: (b, 0, 0, i)),
                  pl.BlockSpec((1, IDX_HEADS, TQ), lambda b, i: (b, 0, i)),
                  pl.BlockSpec((1, N_MIX_HEADS, LANE, TQ), lambda b, i: (b, 0, 0, i)),
                  pl.BlockSpec((1, t, LANE), lambda b, i: (b, 0, 0), pipeline_mode=pl.Buffered(1)),
                  pl.BlockSpec((1, t, LANE), lambda b, i: (b, 0, 0), pipeline_mode=pl.Buffered(1)),
                  pl.BlockSpec((1, nkt, DSA_VT_ROWS, TK), lambda b, i: (b, 0, 0, 0),
                               pipeline_mode=pl.Buffered(1)),
                  pl.BlockSpec(tab.shape, lambda b, i: (0, 0, 0, 0), pipeline_mode=pl.Buffered(1))],
        out_specs=pl.BlockSpec((1, TQ, N_MIX_HEADS * LANE), lambda b, i: (b, i, 0)),
        out_shape=jax.ShapeDtypeStruct((bsz, t, N_MIX_HEADS * LANE), BF16),
        scratch_shapes=[pltpu.VMEM((t // KS, KS, TQ), F32), pltpu.VMEM((t // KS, KS, TQ), BF16),
                        pltpu.VMEM((TK, TQ), F32),
                        pltpu.VMEM((N_MIX_HEADS, TK, TQ), F32), pltpu.VMEM((N_MIX_HEADS, 1, TQ), F32),
                        pltpu.VMEM((N_MIX_HEADS, 1, TQ), F32), pltpu.VMEM((N_MIX_HEADS, 1, TQ), F32),
                        pltpu.VMEM((N_MIX_HEADS, DSA_VT_ROWS, TQ), F32)],
        compiler_params=pltpu.CompilerParams(dimension_semantics=("arbitrary", "arbitrary"),
                                             vmem_limit_bytes=VMEM_LIMIT),
        name="dsa_attention",
    )(qit, wit, qat, kidx, ckv, ckvt, tab)


def _tail_kernel(h_ref, mix_ref, qmem_ref, kvm_ref, *rest, has_uv, final_norm):
    if has_uv:
        wuv_ref, *rest = rest
    wmix_ref, wmem_ref, g_ref, wg_ref, wu_ref, wd_ref, gf_ref, o_ref = rest
    mix = mix_ref[0]
    if has_uv:
        mix = _dot(mix, wuv_ref[...]).astype(BF16)
    upd = _dot(mix, wmix_ref[...])
    qm = qmem_ref[0]
    for hm in range(N_MEM_HEADS):
        sl = slice(hm * LANE, (hm + 1) * LANE)
        kv = kvm_ref[0, :, sl]
        s = _dot_nt(qm[:, sl], kv)
        e = jnp.exp(s - jnp.max(s, axis=-1, keepdims=True))
        p = e / jnp.sum(e, axis=-1, keepdims=True)
        o_h = _dot(p.astype(BF16), kv).astype(BF16)
        upd = upd + _dot(o_h, wmem_ref[sl, :])
    h = h_ref[0] + upd
    hn = _rms(h, g_ref[...]).astype(BF16)
    act = (jax.nn.silu(_dot(hn, wg_ref[...])) * _dot(hn, wu_ref[...])).astype(BF16)
    out = h + _dot(act, wd_ref[...])
    if final_norm:
        out = _rms(out, gf_ref[...])
    o_ref[0] = out


def _layer_tail(h, mix, qmem, kvm, w_uv, w_mix, w_mem, gain, wg, wu, wd, gain_final, final_norm, tm):
    bsz, t, d = h.shape
    has_uv = w_uv is not None
    const = lambda a: pl.BlockSpec(a.shape, lambda b, i: (0,) * a.ndim, pipeline_mode=pl.Buffered(1))
    weights = ([w_uv] if has_uv else []) + [w_mix, w_mem, gain, wg, wu, wd, gain_final]
    return pl.pallas_call(
        functools.partial(_tail_kernel, has_uv=has_uv, final_norm=final_norm),
        grid=(bsz, t // tm),
        in_specs=[pl.BlockSpec((1, tm, d), lambda b, i: (b, i, 0)),
                  pl.BlockSpec((1, tm, mix.shape[2]), lambda b, i: (b, i, 0)),
                  pl.BlockSpec((1, tm, qmem.shape[2]), lambda b, i: (b, i, 0)),
                  pl.BlockSpec((1,) + kvm.shape[1:], lambda b, i: (b, 0, 0))]
                 + [const(w) for w in weights],
        out_specs=pl.BlockSpec((1, tm, d), lambda b, i: (b, i, 0)),
        out_shape=jax.ShapeDtypeStruct((bsz, t, d), F32),
        compiler_params=pltpu.CompilerParams(dimension_semantics=("arbitrary", "arbitrary"),
                                             vmem_limit_bytes=VMEM_LIMIT),
        name="layer_tail",
    )(h, mix, qmem, kvm, *weights)


def _pad_cols(w, n):
    return jnp.pad(w, ((0, 0), (0, n - w.shape[1])))


def _value_rows(w_rows, n_heads):
    d_out = w_rows.shape[1]
    w = w_rows.reshape(n_heads, HEAD_DIM, d_out)
    return jnp.pad(w, ((0, 0), (LANE - HEAD_DIM, 0), (0, 0))).reshape(n_heads * LANE, d_out)


def _interleave_kv(k, v, n_heads):
    d_in = k.shape[0]
    kv = jnp.concatenate([k.reshape(d_in, n_heads, HEAD_DIM), v.reshape(d_in, n_heads, HEAD_DIM)], axis=2)
    return kv.reshape(d_in, n_heads * LANE)


def _bucket_bias(rel_bias, bucket_np):
    onehot = jax.nn.one_hot(jnp.asarray(bucket_np.reshape(-1)), N_REL_BUCKETS, dtype=F32)
    out = jnp.dot(onehot, rel_bias, precision=lax.Precision.HIGHEST) * LOG2E
    return out.reshape(bucket_np.shape + (rel_bias.shape[1],))


def _bias_tiles(rel_bias):
    m = np.arange(N_BIAS_TILES)[:, None, None]
    dist = LANE * m + np.arange(LANE)[None, None, :] - np.arange(KS)[None, :, None]
    return jnp.transpose(_bucket_bias(rel_bias, _rel_bucket_np(dist)), (0, 3, 1, 2))


def _bias_cmp_table(rel_bias):
    rel = np.arange(2 * CMP_PAD) - CMP_PAD
    dist = np.arange(TQ)[None, :] - CMP_STRIDE * rel[:, None] - (CMP_LEN - 1)
    return jnp.transpose(_bucket_bias(rel_bias, _rel_bucket_np(dist)), (2, 0, 1))


def _overlap_matrix(t):
    n_cmp = (t - CMP_LEN) // CMP_STRIDE + 1
    n_sel = t // SEL_BLOCK
    cs = np.arange(CMP_PAD) * CMP_STRIDE
    ss = np.arange(LANE) * SEL_BLOCK
    ov = (cs[None, :] <= ss[:, None] + SEL_BLOCK - 1) & (cs[None, :] + CMP_LEN - 1 >= ss[:, None])
    ov &= (np.arange(CMP_PAD) < n_cmp)[None, :] & (np.arange(LANE) < n_sel)[:, None]
    return jnp.asarray(ov, BF16)


def kernel(x, mem, rel_bias, norm_mix, norm_ffn, norm_mem, w_mem_kv, w_out, ffn_gate, ffn_up, ffn_down,
           nsa_w_in, nsa_gate_b, nsa_cmp_pos_k, nsa_cmp_pos_v,
           nsa_cmp_k_w1, nsa_cmp_k_b1, nsa_cmp_k_w2, nsa_cmp_k_b2,
           nsa_cmp_v_w1, nsa_cmp_v_b1, nsa_cmp_v_w2, nsa_cmp_v_b2,
           dsa_w_in, dsa_q_norm, dsa_kv_norm, dsa_w_q_up, dsa_w_uk, dsa_w_uv, dsa_w_q_idx, dsa_kidx_norm,
           norm_final):
    bsz, t, d = x.shape
    m_len = mem.shape[1]
    depth = norm_mix.shape[0]
    g, r, hh = NSA_KV_HEADS, NSA_GROUP, N_MIX_HEADS
    d_mix = hh * HEAD_DIM
    kvw_ = g * HEAD_DIM
    assert t % 1024 == 0 and t // SEL_BLOCK <= LANE and t // CMP_STRIDE <= CMP_PAD
    tm = 512
    h = x.astype(F32)

    tab = _bias_tiles(rel_bias)
    tab_nsa = jnp.transpose(tab.reshape(N_BIAS_TILES, g, r, KS, LANE), (1, 0, 2, 3, 4))
    tabc = _bias_cmp_table(rel_bias).reshape(g, r, 2 * CMP_PAD, TQ)
    ovl = _overlap_matrix(t)

    for layer in range(depth):
        j = layer // 2
        wkv = w_mem_kv[layer]
        wkv = _interleave_kv(wkv[:, :N_MEM_HEADS * HEAD_DIM], wkv[:, N_MEM_HEADS * HEAD_DIM:], N_MEM_HEADS)
        kvm = _rms_proj(mem.reshape(bsz * m_len, d), norm_mem[layer], wkv.astype(BF16), BF16, m_len)
        kvm = kvm.reshape(bsz, m_len, N_MEM_HEADS * LANE)
        w_o = w_out[layer]
        w_mem_o = _value_rows(w_o[d_mix:], N_MEM_HEADS).astype(BF16)
        w_mix = w_o[:d_mix].astype(BF16)

        if layer % 2 == 0:
            w = nsa_w_in[j]
            c = np.cumsum([0, d_mix, kvw_, kvw_, kvw_, kvw_, kvw_, kvw_, hh * N_BRANCH,
                           N_MEM_HEADS * HEAD_DIM])
            wq, wkc, wvc, wks, wvs, wkw, wvw, wgl, wqm = [w[:, c[k]:c[k + 1]] for k in range(9)]
            n_g = r * N_BRANCH
            wgl = jnp.concatenate([_pad_cols(wgl[:, gg * n_g:(gg + 1) * n_g], LANE) for gg in range(g)], 1)
            w_all = jnp.concatenate([wq, wkc, wvc, _interleave_kv(wks, wvs, g),
                                     _interleave_kv(wkw, wvw, g), wgl, wqm], axis=1)
            qt, cmp_raw, kvs, vst, kvw, vwt, glt, qmem = _nsa_proj(h, norm_mix[layer], w_all.astype(BF16), tm)
            nc = t // CMP_STRIDE
            x2 = cmp_raw.reshape(bsz, 2 * g, nc, CMP_STRIDE * HEAD_DIM)
            pos = jnp.stack([nsa_cmp_pos_k[j], nsa_cmp_pos_v[j]]).reshape(2, 2, CMP_STRIDE * HEAD_DIM)
            w1 = jnp.stack([nsa_cmp_k_w1[j], nsa_cmp_v_w1[j]]).astype(BF16)
            b1 = jnp.stack([nsa_cmp_k_b1[j], nsa_cmp_v_b1[j]])[:, None, :]
            w2 = jnp.stack([jnp.pad(nsa_cmp_k_w2[j], ((0, 0), (0, HEAD_DIM))),
                            jnp.pad(nsa_cmp_v_w2[j], ((0, 0), (HEAD_DIM, 0)))]).astype(BF16)
            b2 = jnp.stack([jnp.pad(nsa_cmp_k_b2[j], (0, HEAD_DIM)),
                            jnp.pad(nsa_cmp_v_b2[j], (HEAD_DIM, 0))])[:, None, :]
            kvc = _compress(x2, pos, w1, b1, w2, b2)
            kvc = jnp.pad(kvc, ((0, 0), (0, 0), (0, CMP_PAD - nc), (0, 0)))
            vct = jnp.transpose(kvc, (0, 1, 3, 2))
            gb = jnp.pad(nsa_gate_b[j].reshape(g, n_g), ((0, 0), (0, GATE_ROWS - n_g)))
            gbt = jnp.broadcast_to(gb[:, :, None], (g, GATE_ROWS, TQ))
            mix = _nsa_attention(qt, kvc, vct, kvs, vst, kvw, vwt, glt, gbt, tab_nsa, tabc, ovl,
                                 t // SEL_BLOCK)
            w_uv = None
        else:
            w = dsa_w_in[j]
            c = np.cumsum([0, Q_LORA, KV_LORA, IDX_DIM, IDX_HEADS, N_MEM_HEADS * HEAD_DIM])
            wcq, wckv, wki, wwi, wqm = [w[:, c[k]:c[k + 1]] for k in range(5)]
            w_all = jnp.concatenate([wcq, wckv, _pad_cols(wki, LANE), _pad_cols(wwi, LANE),
                                     wqm], axis=1).astype(BF16)
            wuk = jnp.transpose(dsa_w_uk[j], (1, 2, 0))
            wuk = jnp.stack([jnp.pad(wuk[hd], ((HEAD_DIM * (hd % 2), HEAD_DIM * (1 - hd % 2)), (0, 0)))
                             for hd in range(hh)]).astype(BF16)
            qat, qit, wit, ckv, ckvt, kidx, qmem = _dsa_proj(
                h, norm_mix[layer][None], w_all, dsa_q_norm[j][None], dsa_kv_norm[j][None],
                _pad_cols(dsa_kidx_norm[j][None], LANE), dsa_w_q_up[j].astype(BF16), wuk,
                dsa_w_q_idx[j].astype(BF16), tm)
            mix = _dsa_attention(qit, wit, qat, kidx, ckv, ckvt, tab, min(DSA_TOPK, t // 4))
            wv = jnp.transpose(dsa_w_uv[j], (1, 0, 2))
            w_uv = (jnp.eye(hh, dtype=F32)[:, None, :, None] * wv[:, :, None, :]
                    ).reshape(hh * KV_LORA, d_mix).astype(BF16)

        h = _layer_tail(h, mix, qmem, kvm, w_uv, w_mix, w_mem_o, norm_ffn[layer][None],
                        ffn_gate[layer].astype(BF16), ffn_up[layer].astype(BF16),
                        ffn_down[layer].astype(BF16), norm_final[None], layer == depth - 1, tm)
    return h.astype(x.dtype)
```

```python
import functools
import math

import numpy as np
import jax
import jax.numpy as jnp
from jax import lax
from jax.experimental import pallas as pl
from jax.experimental.pallas import tpu as pltpu

F32 = jnp.float32
BF16 = jnp.bfloat16
I32 = jnp.int32

NEG = -1e30
EPS = 1e-6
LOG2E = math.log2(math.e)
LANE = 128
SUBLANE = 8
BF16_ROWS = 16
HEAD_DIM = 64
N_MIX_HEADS = 12
N_MEM_HEADS = 4
N_REL_BUCKETS = 32
REL_MAX_EXACT = 16
REL_MAX_DIST = 2048
NSA_KV_HEADS = 2
NSA_GROUP = N_MIX_HEADS // NSA_KV_HEADS
N_BRANCH = 3
GATE_ROWS = 24
CMP_LEN = 32
CMP_STRIDE = 16
SEL_BLOCK = 64
N_SEL = 16
WINDOW = 512
FORCE_BONUS = 1e4
Q_LORA = 256
KV_LORA = 128
IDX_HEADS = 8
IDX_DIM = 64
DSA_TOPK = 256
TQ = 256
TK = 256
SCORE_TK = 512
KS = LANE
N_BIAS_TILES = REL_MAX_DIST // KS + 2
CMP_PAD = 512
NSA_VT_ROWS = HEAD_DIM
DSA_VT_ROWS = KV_LORA
VMEM_LIMIT = 56 * 1024 * 1024
INT_MIN = -2 ** 31
F32_MIN = float(np.finfo(np.float32).min)


def _dot(a, b):
    return jnp.dot(a, b, preferred_element_type=F32)


def _dot_nt(a, b):
    return lax.dot_general(a, b, (((1,), (1,)), ((), ())), preferred_element_type=F32)


def _rms(x, gain, n=None):
    n = x.shape[-1] if n is None else n
    ms = jnp.sum(x * x, axis=-1, keepdims=True) * (1.0 / n)
    return x * lax.rsqrt(ms + EPS) * gain


def _rel_bucket_np(dist):
    n = np.maximum(dist, 0)
    nf = np.maximum(n, REL_MAX_EXACT).astype(np.float32)
    large = REL_MAX_EXACT + (np.log(nf / np.float32(REL_MAX_EXACT))
                             / np.float32(math.log(REL_MAX_DIST / REL_MAX_EXACT))
                             * np.float32(N_REL_BUCKETS - REL_MAX_EXACT)).astype(np.int32)
    large = np.minimum(large, N_REL_BUCKETS - 1)
    return np.where(n < REL_MAX_EXACT, n, large).astype(np.int32)


def _spread_heads(x):
    rows, n = x.shape
    low = lax.broadcasted_iota(I32, (rows, LANE), 1) < HEAD_DIM
    out = []
    for j in range(n // LANE):
        pair = x[:, j * LANE:(j + 1) * LANE]
        out += [jnp.where(low, pair, 0.0), jnp.where(low, pltpu.roll(pair, HEAD_DIM, axis=1), 0.0)]
    return jnp.concatenate(out, axis=1)


def _colsum(x):
    rows, n = x.shape
    return jnp.sum(jnp.sum(x.reshape(rows // SUBLANE, SUBLANE, n), axis=0), axis=0, keepdims=True)


def _key_to_float(v):
    bits = jnp.where(v >= 0, v, v ^ jnp.int32(0x7FFFFFFF))
    return pltpu.bitcast(bits, F32)


def _topk_cols(score, k, row_idx):
    n = score.shape[1]

    def vbody(it, v):
        cand = v + jnp.left_shift(jnp.int32(1), 31 - it)
        cnt = _colsum((score >= _key_to_float(cand)).astype(F32))
        return jnp.where(cnt >= k, cand, v)

    v = lax.fori_loop(0, 32, vbody, jnp.full((1, n), INT_MIN, I32))
    thr = _key_to_float(v)
    gt = score > thr
    eq = score == thr
    need = k - _colsum(gt.astype(F32))

    def cbody(it, c):
        cand = c + jnp.left_shift(jnp.int32(1), 6 - it)
        cnt = _colsum((eq & (row_idx < cand)).astype(F32))
        return jnp.where(cnt < need, cand, c)

    c = lax.fori_loop(0, 7, cbody, jnp.zeros((1, n), I32))
    return gt | (eq & (row_idx <= c))


def _bias_block(tile_of, m0):
    idx = lambda m: jnp.clip(m, 0, N_BIAS_TILES - 1)
    return [jnp.concatenate([tile_of(idx(m0 + b - a)) for b in range(TQ // LANE)], axis=1)
            for a in range(TK // KS)]


def _attend_tiles(n_heads, n_tiles, q_of, kv_of, vt_of, bias_of, fill_mask, negm_ref, s_ref, tmax_ref,
                  m_ref, l_ref, acc_ref):
    parts = [slice(a * KS, (a + 1) * KS) for a in range(TK // KS)]
    m_ref[...] = jnp.full(m_ref.shape, F32_MIN, F32)
    l_ref[...] = jnp.zeros(l_ref.shape, F32)
    acc_ref[...] = jnp.zeros(acc_ref.shape, F32)

    def logits_of(j):
        fill_mask(j)
        kv = kv_of(j)
        bias_h = bias_of(j)

        def run(h):
            bias = bias_h(h)
            tmax = None
            for a, sl in enumerate(parts):
                s_a = _dot(kv[sl], q_of(h)) + bias[a] + negm_ref[sl, :]
                s_ref[h, sl, :] = s_a
                mx = jnp.max(s_a, axis=0, keepdims=True)
                tmax = mx if tmax is None else jnp.maximum(tmax, mx)
            tmax_ref[h] = tmax
        return run

    def accumulate(j):
        vt = vt_of(j)

        def run(h):
            m_old = m_ref[h]
            m_new = jnp.maximum(m_old, tmax_ref[h])
            alpha = jnp.exp2(m_old - m_new)
            m_ref[h] = m_new
            p = [jnp.exp2(s_ref[h, sl, :] - m_new) for sl in parts]
            l_ref[h] = alpha * l_ref[h] + _colsum(sum(p[1:], p[0]))
            acc = alpha * acc_ref[h]
            for sl, p_a in zip(parts, p):
                acc = acc + _dot(vt[:, sl], p_a.astype(BF16))
            acc_ref[h] = acc
        return run

    first = logits_of(0)
    for h in range(n_heads):
        first(h)

    def body(j, carry):
        consume = accumulate(j)
        produce = logits_of(jnp.minimum(j + 1, n_tiles - 1))
        for h in range(n_heads):
            consume(h)
            produce(h)
        return carry

    lax.fori_loop(0, n_tiles, body, 0)


def _rms_proj_kernel(x_ref, g_ref, w_ref, o_ref):
    y = _rms(x_ref[...], g_ref[...]).astype(BF16)
    o_ref[...] = _dot(y, w_ref[...]).astype(o_ref.dtype)


def _rms_proj(x2d, gain, w_bf16, out_dtype, tm):
    m, d = x2d.shape
    n = w_bf16.shape[1]
    return pl.pallas_call(
        _rms_proj_kernel,
        grid=(m // tm,),
        in_specs=[pl.BlockSpec((tm, d), lambda i: (i, 0)),
                  pl.BlockSpec((1, d), lambda i: (0, 0)),
                  pl.BlockSpec((d, n), lambda i: (0, 0))],
        out_specs=pl.BlockSpec((tm, n), lambda i: (i, 0)),
        out_shape=jax.ShapeDtypeStruct((m, n), out_dtype),
        compiler_params=pltpu.CompilerParams(dimension_semantics=("arbitrary",),
                                             vmem_limit_bytes=VMEM_LIMIT),
        name="rms_proj",
    )(x2d, gain.reshape(1, d), w_bf16)


def _nsa_proj_kernel(x_ref, g_ref, w_ref, qt_ref, cmp_ref, kvs_ref, vst_ref, kvw_ref, vwt_ref,
                     glt_ref, qmem_ref):
    tm = x_ref.shape[1]
    y = _rms(x_ref[0], g_ref[...]).astype(BF16)
    value_row = lax.broadcasted_iota(I32, (LANE, tm), 0) >= HEAD_DIM
    off = 0
    for pair in range(N_MIX_HEADS // 2):
        acc_t = (_dot(y, w_ref[:, off:off + LANE]) * (HEAD_DIM ** -0.5 * LOG2E)).T
        qt_ref[0, 2 * pair] = jnp.where(value_row, 0.0, acc_t).astype(BF16)
        qt_ref[0, 2 * pair + 1] = jnp.where(value_row, 0.0, pltpu.roll(acc_t, HEAD_DIM, axis=0)).astype(BF16)
        off += LANE
    raw = _dot(y, w_ref[:, off:off + 2 * LANE])
    for piece in range(2 * NSA_KV_HEADS):
        cmp_ref[0, piece] = raw[:, piece * HEAD_DIM:(piece + 1) * HEAD_DIM]
    off += 2 * LANE
    for kv_ref, vt_ref in ((kvs_ref, vst_ref), (kvw_ref, vwt_ref)):
        for g in range(NSA_KV_HEADS):
            acc = _dot(y, w_ref[:, off:off + LANE])
            off += LANE
            kv_ref[0, :, g * LANE:(g + 1) * LANE] = acc.astype(BF16)
            v_t = acc.T[HEAD_DIM:, :].astype(BF16)
            for jt in range(tm // TK):
                vt_ref[0, g, jt] = v_t[:, jt * TK:(jt + 1) * TK]
    for g in range(NSA_KV_HEADS):
        acc = _dot(y, w_ref[:, off:off + LANE])
        off += LANE
        glt_ref[0, g] = acc.T[:GATE_ROWS, :]
    qmem_ref[0] = _spread_heads(_dot(y, w_ref[:, off:off + N_MEM_HEADS * HEAD_DIM]) * HEAD_DIM ** -0.5
                                ).astype(BF16)


def _nsa_proj(h, gain, w, tm):
    bsz, t, d = h.shape
    g = NSA_KV_HEADS
    nkt = t // TK
    row = lambda n: pl.BlockSpec((1, tm, n), lambda b, i: (b, i, 0))
    vt_spec = pl.BlockSpec((1, g, tm // TK, NSA_VT_ROWS, TK), lambda b, i: (b, 0, i, 0, 0))
    return pl.pallas_call(
        _nsa_proj_kernel,
        grid=(bsz, t // tm),
        in_specs=[pl.BlockSpec((1, tm, d), lambda b, i: (b, i, 0)),
                  pl.BlockSpec((1, d), lambda b, i: (0, 0)),
                  pl.BlockSpec(w.shape, lambda b, i: (0, 0))],
        out_specs=[pl.BlockSpec((1, N_MIX_HEADS, LANE, tm), lambda b, i: (b, 0, 0, i)),
                   pl.BlockSpec((1, 2 * g, tm, HEAD_DIM), lambda b, i: (b, 0, i, 0)),
                   row(g * LANE), vt_spec, row(g * LANE), vt_spec,
                   pl.BlockSpec((1, g, GATE_ROWS, tm), lambda b, i: (b, 0, 0, i)),
                   row(N_MEM_HEADS * LANE)],
        out_shape=[jax.ShapeDtypeStruct((bsz, N_MIX_HEADS, LANE, t), BF16),
                   jax.ShapeDtypeStruct((bsz, 2 * g, t, HEAD_DIM), F32),
                   jax.ShapeDtypeStruct((bsz, t, g * LANE), BF16),
                   jax.ShapeDtypeStruct((bsz, g, nkt, NSA_VT_ROWS, TK), BF16),
                   jax.ShapeDtypeStruct((bsz, t, g * LANE), BF16),
                   jax.ShapeDtypeStruct((bsz, g, nkt, NSA_VT_ROWS, TK), BF16),
                   jax.ShapeDtypeStruct((bsz, g, GATE_ROWS, t), F32),
                   jax.ShapeDtypeStruct((bsz, t, N_MEM_HEADS * LANE), BF16)],
        compiler_params=pltpu.CompilerParams(dimension_semantics=("arbitrary", "arbitrary"),
                                             vmem_limit_bytes=VMEM_LIMIT),
        name="nsa_proj",
    )(h, gain.reshape(1, d), w)


def _compress_kernel(xk_ref, xv_ref, pos_ref, w1_ref, b1_ref, w2_ref, b2_ref, o_ref):
    nc = xk_ref.shape[2]
    half = xk_ref.shape[3]
    out = None
    for j, x_ref in enumerate((xk_ref, xv_ref)):
        x = x_ref[0, 0]
        top = _dot((x + pos_ref[j, 0:1, :]).astype(BF16), w1_ref[j, :half, :])
        bot = _dot((x + pos_ref[j, 1:2, :]).astype(BF16), w1_ref[j, half:, :])
        pre = top + pltpu.roll(bot, nc - 1, axis=0) + b1_ref[j]
        hid = jax.nn.gelu(pre)
        res = _dot(hid.astype(BF16), w2_ref[j]) + b2_ref[j]
        out = res if out is None else out + res
    o_ref[0, 0] = out.astype(o_ref.dtype)


def _compress(x2, pos, w1, b1, w2, b2):
    bsz, _, nc, width = x2.shape
    g = NSA_KV_HEADS
    return pl.pallas_call(
        _compress_kernel,
        grid=(bsz, g),
        in_specs=[pl.BlockSpec((1, 1, nc, width), lambda b, gg: (b, gg, 0, 0)),
                  pl.BlockSpec((1, 1, nc, width), lambda b, gg: (b, gg + NSA_KV_HEADS, 0, 0)),
                  pl.BlockSpec(pos.shape, lambda b, gg: (0, 0, 0)),
                  pl.BlockSpec(w1.shape, lambda b, gg: (0, 0, 0)),
                  pl.BlockSpec(b1.shape, lambda b, gg: (0, 0, 0)),
                  pl.BlockSpec(w2.shape, lambda b, gg: (0, 0, 0)),
                  pl.BlockSpec(b2.shape, lambda b, gg: (0, 0, 0))],
        out_specs=pl.BlockSpec((1, 1, nc, LANE), lambda b, gg: (b, gg, 0, 0)),
        out_shape=jax.ShapeDtypeStruct((bsz, g, nc, LANE), BF16),
        compiler_params=pltpu.CompilerParams(dimension_semantics=("arbitrary", "arbitrary"),
                                             vmem_limit_bytes=VMEM_LIMIT),
        name="nsa_compress",
    )(x2, x2, pos, w1, b1, w2, b2)


def _nsa_kernel(q_ref, kvc_ref, vct_ref, kvs_ref, vst_ref, kvw_ref, vwt_ref, glt_ref, gbt_ref,
                tab_ref, tabc_ref, ovl_ref, o_ref,
                psum_ref, negsel_ref, negm_ref, s_ref, tmax_ref, m_ref, l_ref, acc_ref, ocmp_ref, oslc_ref,
                *, n_sel):
    r_heads = NSA_GROUP
    i = pl.program_id(2)
    qs = i * TQ
    t_row = qs + lax.broadcasted_iota(I32, (1, TQ), 1)
    krow = lax.broadcasted_iota(I32, (TK, TQ), 0)
    q_of = lambda h: q_ref[0, h]

    def normalised(h):
        return acc_ref[h] * (1.0 / l_ref[h])

    kvc = kvc_ref[0, 0]
    vct = vct_ref[0, 0]
    n_row = lax.broadcasted_iota(I32, (CMP_PAD, TQ), 0)
    negc = jnp.where((CMP_STRIDE * n_row + (CMP_LEN - 1)) <= t_row, 0.0, -jnp.inf)
    j0 = pl.multiple_of(CMP_PAD - (TQ // CMP_STRIDE) * i, SUBLANE)
    psum_ref[...] = jnp.zeros(psum_ref.shape, F32)

    cparts = [slice(a * KS, (a + 1) * KS) for a in range(CMP_PAD // KS)]
    for h in range(r_heads):
        s = [_dot(kvc[sl], q_of(h)) + tabc_ref[0, h, pl.ds(j0 + sl.start, KS), :] + negc[sl] for sl in cparts]
        m = functools.reduce(jnp.maximum, [jnp.max(s_a, axis=0, keepdims=True) for s_a in s])
        m = jnp.maximum(m, F32_MIN)
        e = [jnp.exp2(s_a - m) for s_a in s]
        den = _colsum(functools.reduce(jnp.add, e))
        inv = 1.0 / jnp.where(den > 0.0, den, 1.0)
        acc = functools.reduce(jnp.add, [_dot(vct[:, sl], e_a.astype(BF16)) for sl, e_a in zip(cparts, e)])
        ocmp_ref[h] = acc[LANE - HEAD_DIM:] * inv
        for sl, e_a in zip(cparts, e):
            psum_ref[sl, :] += e_a * inv

    psum = psum_ref[...]
    ovl = ovl_ref[...]
    hi = psum.astype(BF16)
    rem1 = psum - hi.astype(F32)
    mid = rem1.astype(BF16)
    lo = (rem1 - mid.astype(F32)).astype(BF16)
    p_slc = _dot(ovl, hi) + _dot(ovl, mid) + _dot(ovl, lo)
    blk = lax.broadcasted_iota(I32, (LANE, TQ), 0)
    cur = jnp.right_shift(t_row, 6)
    forced = (blk == 0) | (blk == cur) | (blk == cur - 1)
    admissible = (blk * SEL_BLOCK) <= t_row
    score = jnp.where(admissible, p_slc + jnp.where(forced, FORCE_BONUS, 0.0), NEG)
    score = jnp.where(blk < n_sel, score, -jnp.inf)
    sel = _topk_cols(score, min(N_SEL, n_sel), blk) & (score > 0.5 * NEG)
    negsel_ref[...] = jnp.where(sel, 0.0, -jnp.inf)

    def bias_of(kt):
        return lambda h: _bias_block(lambda m: tab_ref[0, m, h], (TQ // KS) * i - (TK // KS) * kt)

    def key_rows(ref, kt):
        return ref[0, pl.ds(pl.multiple_of(kt * TK, TK), TK), :]

    last_tile = (qs + TQ - 1) // TK

    def sel_mask(kt):
        rows = [jnp.broadcast_to(negsel_ref[pl.ds((TK // SEL_BLOCK) * kt + a, 1), :], (SEL_BLOCK, TQ))
                for a in range(TK // SEL_BLOCK)]
        negm_ref[...] = jnp.where((kt * TK + krow) <= t_row, jnp.concatenate(rows, axis=0), -jnp.inf)

    _attend_tiles(r_heads, last_tile + 1, q_of, lambda kt: key_rows(kvs_ref, kt), lambda kt: vst_ref[0, 0, kt],
                  bias_of, sel_mask, negm_ref, s_ref, tmax_ref, m_ref, l_ref, acc_ref)
    for h in range(r_heads):
        oslc_ref[h] = normalised(h)

    n_win = max((TQ * e + TQ - 1) // TK - (TQ * e - WINDOW + 1) // TK + 1 for e in range(max(TK // TQ, 1)))

    def win_mask(j):
        kt = last_tile - j
        dist = t_row - (jnp.maximum(kt, 0) * TK + krow)
        ok = (dist >= 0) & (dist < jnp.where(kt >= 0, WINDOW, 0))
        negm_ref[...] = jnp.where(ok, 0.0, -jnp.inf)

    _attend_tiles(r_heads, n_win, q_of, lambda j: key_rows(kvw_ref, jnp.maximum(last_tile - j, 0)),
                  lambda j: vwt_ref[0, 0, jnp.maximum(last_tile - j, 0)], lambda j: bias_of(last_tile - j),
                  win_mask, negm_ref, s_ref, tmax_ref, m_ref, l_ref, acc_ref)

    gates = jax.nn.sigmoid(glt_ref[0, 0] + gbt_ref[0])
    for pair in range(r_heads // 2):
        outs = []
        for r in (2 * pair, 2 * pair + 1):
            c = N_BRANCH * r
            outs.append(gates[c:c + 1] * ocmp_ref[r] + gates[c + 1:c + 2] * oslc_ref[r]
                        + gates[c + 2:c + 3] * normalised(r))
        o_ref[0, :, pair * LANE:(pair + 1) * LANE] = jnp.concatenate(outs, axis=0).T.astype(o_ref.dtype)


def _nsa_attention(qt, kvc, vct, kvs, vst, kvw, vwt, glt, gbt, tab, tabc, ovl, n_sel):
    bsz, _, _, t = qt.shape
    g, r = NSA_KV_HEADS, NSA_GROUP
    nkt = t // TK
    once = dict(pipeline_mode=pl.Buffered(1))
    return pl.pallas_call(
        functools.partial(_nsa_kernel, n_sel=n_sel),
        grid=(bsz, g, t // TQ),
        in_specs=[pl.BlockSpec((1, r, LANE, TQ), lambda b, gg, i: (b, gg, 0, i)),
                  pl.BlockSpec((1, 1, CMP_PAD, LANE), lambda b, gg, i: (b, gg, 0, 0)),
                  pl.BlockSpec((1, 1, LANE, CMP_PAD), lambda b, gg, i: (b, gg, 0, 0)),
                  pl.BlockSpec((1, t, LANE), lambda b, gg, i: (b, 0, gg)),
                  pl.BlockSpec((1, 1, nkt, NSA_VT_ROWS, TK), lambda b, gg, i: (b, gg, 0, 0, 0)),
                  pl.BlockSpec((1, t, LANE), lambda b, gg, i: (b, 0, gg)),
                  pl.BlockSpec((1, 1, nkt, NSA_VT_ROWS, TK), lambda b, gg, i: (b, gg, 0, 0, 0)),
                  pl.BlockSpec((1, 1, GATE_ROWS, TQ), lambda b, gg, i: (b, gg, 0, i)),
                  pl.BlockSpec((1, GATE_ROWS, TQ), lambda b, gg, i: (gg, 0, 0)),
                  pl.BlockSpec((1, N_BIAS_TILES, r, KS, LANE), lambda b, gg, i: (gg, 0, 0, 0, 0), **once),
                  pl.BlockSpec((1, r, 2 * CMP_PAD, TQ), lambda b, gg, i: (gg, 0, 0, 0), **once),
                  pl.BlockSpec((LANE, CMP_PAD), lambda b, gg, i: (0, 0))],
        out_specs=pl.BlockSpec((1, TQ, r * HEAD_DIM), lambda b, gg, i: (b, i, gg)),
        out_shape=jax.ShapeDtypeStruct((bsz, t, g * r * HEAD_DIM), BF16),
        scratch_shapes=[pltpu.VMEM((CMP_PAD, TQ), F32), pltpu.VMEM((LANE, TQ), F32),
                        pltpu.VMEM((TK, TQ), F32), pltpu.VMEM((r, TK, TQ), F32),
                        pltpu.VMEM((r, 1, TQ), F32), pltpu.VMEM((r, 1, TQ), F32), pltpu.VMEM((r, 1, TQ), F32),
                        pltpu.VMEM((r, NSA_VT_ROWS, TQ), F32), pltpu.VMEM((r, HEAD_DIM, TQ), F32),
                        pltpu.VMEM((r, HEAD_DIM, TQ), F32)],
        compiler_params=pltpu.CompilerParams(
            dimension_semantics=("arbitrary", "arbitrary", "arbitrary"),
            vmem_limit_bytes=VMEM_LIMIT),
        name="nsa_attention",
    )(qt, kvc, vct, kvs, vst, kvw, vwt, glt, gbt, tab, tabc, ovl)


def _dsa_proj_kernel(x_ref, g_ref, w_ref, qn_ref, kvn_ref, kin_ref, wqu_ref, wuk_ref, wqi_ref,
                     qat_ref, qit_ref, wit_ref, ckv_ref, ckvt_ref, kidx_ref, qmem_ref):
    tm = x_ref.shape[1]
    y = _rms(x_ref[0], g_ref[...]).astype(BF16)
    c_q = _rms(_dot(y, w_ref[:, 0:Q_LORA]), qn_ref[...]).astype(BF16)
    c_kv = _rms(_dot(y, w_ref[:, Q_LORA:Q_LORA + KV_LORA]), kvn_ref[...])
    ckv_ref[0] = c_kv.astype(BF16)
    c_kv_t = c_kv.T.astype(BF16)
    for jt in range(tm // TK):
        ckvt_ref[0, jt] = c_kv_t[:, jt * TK:(jt + 1) * TK]
    off = Q_LORA + KV_LORA
    k_idx = _rms(_dot(y, w_ref[:, off:off + LANE]), kin_ref[...], n=IDX_DIM)
    kidx_ref[0] = k_idx.astype(BF16)
    off += LANE
    w_idx = _dot(y, w_ref[:, off:off + LANE]) * (IDX_HEADS ** -0.5 * IDX_DIM ** -0.5)
    wit_ref[0] = w_idx.T[:IDX_HEADS, :]
    off += LANE
    qmem_ref[0] = _spread_heads(_dot(y, w_ref[:, off:off + N_MEM_HEADS * HEAD_DIM]) * HEAD_DIM ** -0.5
                                ).astype(BF16)
    value_row = lax.broadcasted_iota(I32, (LANE, tm), 0) >= IDX_DIM
    for pair in range(IDX_HEADS // 2):
        acc_t = _dot(c_q, wqi_ref[:, pair * LANE:(pair + 1) * LANE]).T
        qit_ref[0, 2 * pair] = jnp.where(value_row, 0.0, acc_t).astype(BF16)
        qit_ref[0, 2 * pair + 1] = jnp.where(value_row, 0.0, pltpu.roll(acc_t, IDX_DIM, axis=0)).astype(BF16)
    for pair in range(N_MIX_HEADS // 2):
        q_pair = (_dot(c_q, wqu_ref[:, pair * LANE:(pair + 1) * LANE]) * HEAD_DIM ** -0.5).astype(BF16)
        for h in (2 * pair, 2 * pair + 1):
            qat_ref[0, h] = (_dot(q_pair, wuk_ref[h]) * LOG2E).T.astype(BF16)


def _dsa_proj(h, gain, w, qn, kvn, kin, wqu, wuk, wqi, tm):
    bsz, t, d = h.shape
    nkt = t // TK
    full = lambda a: pl.BlockSpec(a.shape, lambda b, i: (0,) * a.ndim)
    row = lambda n: pl.BlockSpec((1, tm, n), lambda b, i: (b, i, 0))
    return pl.pallas_call(
        _dsa_proj_kernel,
        grid=(bsz, t // tm),
        in_specs=[pl.BlockSpec((1, tm, d), lambda b, i: (b, i, 0)), full(gain), full(w), full(qn),
                  full(kvn), full(kin), full(wqu), full(wuk), full(wqi)],
        out_specs=[pl.BlockSpec((1, N_MIX_HEADS, LANE, tm), lambda b, i: (b, 0, 0, i)),
                   pl.BlockSpec((1, IDX_HEADS, LANE, tm), lambda b, i: (b, 0, 0, i)),
                   pl.BlockSpec((1, IDX_HEADS, tm), lambda b, i: (b, 0, i)),
                   row(LANE),
                   pl.BlockSpec((1, tm // TK, DSA_VT_ROWS, TK), lambda b, i: (b, i, 0, 0)),
                   row(LANE), row(N_MEM_HEADS * LANE)],
        out_shape=[jax.ShapeDtypeStruct((bsz, N_MIX_HEADS, LANE, t), BF16),
                   jax.ShapeDtypeStruct((bsz, IDX_HEADS, LANE, t), BF16),
                   jax.ShapeDtypeStruct((bsz, IDX_HEADS, t), F32),
                   jax.ShapeDtypeStruct((bsz, t, LANE), BF16),
                   jax.ShapeDtypeStruct((bsz, nkt, DSA_VT_ROWS, TK), BF16),
                   jax.ShapeDtypeStruct((bsz, t, LANE), BF16),
                   jax.ShapeDtypeStruct((bsz, t, N_MEM_HEADS * LANE), BF16)],
        compiler_params=pltpu.CompilerParams(dimension_semantics=("arbitrary", "arbitrary"),
                                             vmem_limit_bytes=VMEM_LIMIT),
        name="dsa_proj",
    )(h, gain, w, qn, kvn, kin, wqu, wuk, wqi)


def _dsa_kernel(qi_ref, wi_ref, qa_ref, kidx_ref, ckv_ref, ckvt_ref, tab_ref, o_ref,
                sc_ref, sc_hi_ref, negm_ref, s_ref, tmax_ref, m_ref, l_ref, acc_ref, *, topk):
    i = pl.program_id(1)
    qs = i * TQ
    n_tiles = (qs + TQ - 1) // TK + 1
    sub = TK // KS
    n_steps = (qs + TQ - 1) // SCORE_TK + 1
    per_step = SCORE_TK // KS
    t_row = qs + lax.broadcasted_iota(I32, (1, TQ), 1)
    krow = lax.broadcasted_iota(I32, (KS, TQ), 0)
    k_f = float(topk)

    grouped = lambda x: x.reshape(KS // SUBLANE, SUBLANE, TQ)

    def score_body(ks, carry):
        for a in range(per_step):
            k0 = pl.multiple_of(ks * SCORE_TK + a * KS, KS)
            kk = kidx_ref[0, pl.ds(k0, KS), :]
            sc = jnp.maximum(_dot(kk, qi_ref[0, 0]), 0.0) * wi_ref[0, 0:1, :]
            for h in range(1, IDX_HEADS):
                sc = sc + jnp.maximum(_dot(kk, qi_ref[0, h]), 0.0) * wi_ref[0, h:h + 1, :]
            sc = jnp.where((k0 + krow) <= t_row, sc, NEG)
            sc_ref[per_step * ks + a] = sc
            sc_hi_ref[per_step * ks + a] = sc.astype(BF16)
        return carry

    lax.fori_loop(0, n_steps, score_body, 0)

    def count(pred):
        def body(ks, acc):
            for a in range(per_step):
                hit = pred(sc_ref[per_step * ks + a], ks * SCORE_TK + a * KS).astype(F32)
                acc = acc + jnp.sum(grouped(hit), axis=0)
            return acc
        acc = lax.fori_loop(0, n_steps, body, jnp.zeros((SUBLANE, TQ), F32))
        return jnp.sum(acc, axis=0, keepdims=True)

    def count_rounded(cand_f):
        cand = jnp.broadcast_to(cand_f, (BF16_ROWS, TQ)).astype(BF16)
        one, zero = jnp.ones((), BF16), jnp.zeros((), BF16)

        def body(ks, acc):
            for a in range(per_step):
                tile = sc_hi_ref[per_step * ks + a].reshape(KS // BF16_ROWS, BF16_ROWS, TQ)
                hit = jnp.where(tile >= cand[None], one, zero)
                part = functools.reduce(jnp.add, [hit[r] for r in range(KS // BF16_ROWS)])
                acc = acc + part.astype(F32)
            return acc
        acc = lax.fori_loop(0, n_steps, body, jnp.zeros((BF16_ROWS, TQ), F32))
        return jnp.sum(acc, axis=0, keepdims=True)

    short = t_row < topk
    half_step = 1 << 15

    def bf16_key_to_float(v16):
        raw16 = jnp.where(v16 >= 0, v16, v16 ^ jnp.int32(0x7FFF))
        return pltpu.bitcast(jnp.left_shift(raw16, 16), F32)

    def high_body(it, v16):
        cand = v16 + jnp.left_shift(jnp.int32(1), 15 - it)
        return jnp.where(count_rounded(bf16_key_to_float(cand)) >= k_f, cand, v16)

    v16 = lax.fori_loop(0, 16, high_body, jnp.full((1, TQ), -(1 << 15), I32))
    key_g = jnp.where(v16 >= 0, jnp.left_shift(v16, 16), jnp.left_shift(v16, 16) | jnp.int32(0xFFFF))

    def low_pass(bit, v, cnt_v):
        cand = v + jnp.left_shift(jnp.int32(1), bit)
        cand_f = _key_to_float(cand)
        cnt = count(lambda sc, k0: sc >= cand_f)
        take = cnt >= k_f
        return jnp.where(take, cand, v), jnp.where(take, cnt, cnt_v)

    def open_count(cnt_v):
        return jnp.sum(((cnt_v != k_f) & ~short).astype(I32))

    passes_per_check = 2
    v, cnt_v = low_pass(16, key_g - half_step, jnp.full((1, TQ), 1e9, F32))

    def v_cond(c):
        return (c[0] >= 0) & (c[3] > 0)

    def v_body(c):
        bit, v, cnt_v, _ = c
        for step in range(passes_per_check):
            v, cnt_v = low_pass(bit - step, v, cnt_v)
        return bit - passes_per_check, v, cnt_v, open_count(cnt_v)

    _, v, cnt_v, open_cols = lax.while_loop(v_cond, v_body, (jnp.int32(15), v, cnt_v, open_count(cnt_v)))
    thr = _key_to_float(v)

    def tie_cut(_):
        need = k_f - count(lambda sc, k0: sc > thr)

        def c_body(it, c):
            cand = c + jnp.left_shift(jnp.int32(1), 13 - it)
            cnt = count(lambda sc, k0: (sc == thr) & ((k0 + krow) < cand))
            return jnp.where(cnt < need, cand, c)

        return lax.fori_loop(0, 14, c_body, jnp.zeros((1, TQ), I32))

    cut = lax.cond(open_cols > 0, tie_cut, lambda _: jnp.full((1, TQ), 2 ** 30, I32), 0)

    def att_mask(kt):
        for a in range(sub):
            sc = sc_ref[sub * kt + a]
            kpos = kt * TK + a * KS + krow
            chosen = short | (sc > thr) | ((sc == thr) & (kpos <= cut))
            negm_ref[a * KS:(a + 1) * KS, :] = jnp.where(chosen & (kpos <= t_row), 0.0, -jnp.inf)

    _attend_tiles(N_MIX_HEADS, n_tiles, lambda h: qa_ref[0, h],
                  lambda kt: ckv_ref[0, pl.ds(pl.multiple_of(kt * TK, TK), TK), :],
                  lambda kt: ckvt_ref[0, kt],
                  lambda kt: (lambda h: _bias_block(lambda m: tab_ref[m, h],
                                                    (TQ // KS) * i - (TK // KS) * kt)),
                  att_mask, negm_ref, s_ref, tmax_ref, m_ref, l_ref, acc_ref)
    for h in range(N_MIX_HEADS):
        out = acc_ref[h] * (1.0 / l_ref[h])
        o_ref[0, :, h * LANE:(h + 1) * LANE] = out.T.astype(o_ref.dtype)


def _dsa_attention(qit, wit, qat, kidx, ckv, ckvt, tab, topk):
    bsz, _, _, t = qat.shape
    nkt = t // TK
    return pl.pallas_call(
        functools.partial(_dsa_kernel, topk=topk),
        grid=(bsz, t // TQ),
        in_specs=[pl.BlockSpec((1, IDX_HEADS, LANE, TQ), lambda b, i: (b, 0, 0, i)),
                  pl.BlockSpec((1, IDX_HEADS, TQ), lambda b, i: (b, 0, i)),
                  pl.BlockSpec((1, N_MIX_HEADS, LANE, TQ), lambda b, i: (b, 0, 0, i)),
                  pl.BlockSpec((1, t, LANE), lambda b, i: (b, 0, 0), pipeline_mode=pl.Buffered(1)),
                  pl.BlockSpec((1, t, LANE), lambda b, i: (b, 0, 0), pipeline_mode=pl.Buffered(1)),
                  pl.BlockSpec((1, nkt, DSA_VT_ROWS, TK), lambda b, i: (b, 0, 0, 0),
                               pipeline_mode=pl.Buffered(1)),
                  pl.BlockSpec(tab.shape, lambda b, i: (0, 0, 0, 0), pipeline_mode=pl.Buffered(1))],
        out_specs=pl.BlockSpec((1, TQ, N_MIX_HEADS * LANE), lambda b, i: (b, i, 0)),
        out_shape=jax.ShapeDtypeStruct((bsz, t, N_MIX_HEADS * LANE), BF16),
        scratch_shapes=[pltpu.VMEM((t // KS, KS, TQ), F32), pltpu.VMEM((t // KS, KS, TQ), BF16),
                        pltpu.VMEM((TK, TQ), F32),
                        pltpu.VMEM((N_MIX_HEADS, TK, TQ), F32), pltpu.VMEM((N_MIX_HEADS, 1, TQ), F32),
                        pltpu.VMEM((N_MIX_HEADS, 1, TQ), F32), pltpu.VMEM((N_MIX_HEADS, 1, TQ), F32),
                        pltpu.VMEM((N_MIX_HEADS, DSA_VT_ROWS, TQ), F32)],
        compiler_params=pltpu.CompilerParams(dimension_semantics=("arbitrary", "arbitrary"),
                                             vmem_limit_bytes=VMEM_LIMIT),
        name="dsa_attention",
    )(qit, wit, qat, kidx, ckv, ckvt, tab)


def _tail_kernel(h_ref, mix_ref, qmem_ref, kvm_ref, *rest, has_uv, final_norm):
    if has_uv:
        wuv_ref, *rest = rest
    wmix_ref, wmem_ref, g_ref, wg_ref, wu_ref, wd_ref, gf_ref, o_ref = rest
    mix = mix_ref[0]
    if has_uv:
        mix = _dot(mix, wuv_ref[...]).astype(BF16)
    upd = _dot(mix, wmix_ref[...])
    qm = qmem_ref[0]
    for hm in range(N_MEM_HEADS):
        sl = slice(hm * LANE, (hm + 1) * LANE)
        kv = kvm_ref[0, :, sl]
        s = _dot_nt(qm[:, sl], kv)
        e = jnp.exp(s - jnp.max(s, axis=-1, keepdims=True))
        p = e / jnp.sum(e, axis=-1, keepdims=True)
        o_h = _dot(p.astype(BF16), kv).astype(BF16)
        upd = upd + _dot(o_h, wmem_ref[sl, :])
    h = h_ref[0] + upd
    hn = _rms(h, g_ref[...]).astype(BF16)
    act = (jax.nn.silu(_dot(hn, wg_ref[...])) * _dot(hn, wu_ref[...])).astype(BF16)
    out = h + _dot(act, wd_ref[...])
    if final_norm:
        out = _rms(out, gf_ref[...])
    o_ref[0] = out


def _layer_tail(h, mix, qmem, kvm, w_uv, w_mix, w_mem, gain, wg, wu, wd, gain_final, final_norm, tm):
    bsz, t, d = h.shape
    has_uv = w_uv is not None
    const = lambda a: pl.BlockSpec(a.shape, lambda b, i: (0,) * a.ndim, pipeline_mode=pl.Buffered(1))
    weights = ([w_uv] if has_uv else []) + [w_mix, w_mem, gain, wg, wu, wd, gain_final]
    return pl.pallas_call(
        functools.partial(_tail_kernel, has_uv=has_uv, final_norm=final_norm),
        grid=(bsz, t // tm),
        in_specs=[pl.BlockSpec((1, tm, d), lambda b, i: (b, i, 0)),
                  pl.BlockSpec((1, tm, mix.shape[2]), lambda b, i: (b, i, 0)),
                  pl.BlockSpec((1, tm, qmem.shape[2]), lambda b, i: (b, i, 0)),
                  pl.BlockSpec((1,) + kvm.shape[1:], lambda b, i: (b, 0, 0))]
                 + [const(w) for w in weights],
        out_specs=pl.BlockSpec((1, tm, d), lambda b, i: (b, i, 0)),
        out_shape=jax.ShapeDtypeStruct((bsz, t, d), F32),
        compiler_params=pltpu.CompilerParams(dimension_semantics=("arbitrary", "arbitrary"),
                                             vmem_limit_bytes=VMEM_LIMIT),
        name="layer_tail",
    )(h, mix, qmem, kvm, *weights)


def _pad_cols(w, n):
    return jnp.pad(w, ((0, 0), (0, n - w.shape[1])))


def _value_rows(w_rows, n_heads):
    d_out = w_rows.shape[1]
    w = w_rows.reshape(n_heads, HEAD_DIM, d_out)
    return jnp.pad(w, ((0, 0), (LANE - HEAD_DIM, 0), (0, 0))).reshape(n_heads * LANE, d_out)


def _interleave_kv(k, v, n_heads):
    d_in = k.shape[0]
    kv = jnp.concatenate([k.reshape(d_in, n_heads, HEAD_DIM), v.reshape(d_in, n_heads, HEAD_DIM)], axis=2)
    return kv.reshape(d_in, n_heads * LANE)


def _bucket_bias(rel_bias, bucket_np):
    onehot = jax.nn.one_hot(jnp.asarray(bucket_np.reshape(-1)), N_REL_BUCKETS, dtype=F32)
    out = jnp.dot(onehot, rel_bias, precision=lax.Precision.HIGHEST) * LOG2E
    return out.reshape(bucket_np.shape + (rel_bias.shape[1],))


def _bias_tiles(rel_bias):
    m = np.arange(N_BIAS_TILES)[:, None, None]
    dist = LANE * m + np.arange(LANE)[None, None, :] - np.arange(KS)[None, :, None]
    return jnp.transpose(_bucket_bias(rel_bias, _rel_bucket_np(dist)), (0, 3, 1, 2))


def _bias_cmp_table(rel_bias):
    rel = np.arange(2 * CMP_PAD) - CMP_PAD
    dist = np.arange(TQ)[None, :] - CMP_STRIDE * rel[:, None] - (CMP_LEN - 1)
    return jnp.transpose(_bucket_bias(rel_bias, _rel_bucket_np(dist)), (2, 0, 1))


def _overlap_matrix(t):
    n_cmp = (t - CMP_LEN) // CMP_STRIDE + 1
    n_sel = t // SEL_BLOCK
    cs = np.arange(CMP_PAD) * CMP_STRIDE
    ss = np.arange(LANE) * SEL_BLOCK
    ov = (cs[None, :] <= ss[:, None] + SEL_BLOCK - 1) & (cs[None, :] + CMP_LEN - 1 >= ss[:, None])
    ov &= (np.arange(CMP_PAD) < n_cmp)[None, :] & (np.arange(LANE) < n_sel)[:, None]
    return jnp.asarray(ov, BF16)


def kernel(x, mem, rel_bias, norm_mix, norm_ffn, norm_mem, w_mem_kv, w_out, ffn_gate, ffn_up, ffn_down,
           nsa_w_in, nsa_gate_b, nsa_cmp_pos_k, nsa_cmp_pos_v,
           nsa_cmp_k_w1, nsa_cmp_k_b1, nsa_cmp_k_w2, nsa_cmp_k_b2,
           nsa_cmp_v_w1, nsa_cmp_v_b1, nsa_cmp_v_w2, nsa_cmp_v_b2,
           dsa_w_in, dsa_q_norm, dsa_kv_norm, dsa_w_q_up, dsa_w_uk, dsa_w_uv, dsa_w_q_idx, dsa_kidx_norm,
           norm_final):
    bsz, t, d = x.shape
    m_len = mem.shape[1]
    depth = norm_mix.shape[0]
    g, r, hh = NSA_KV_HEADS, NSA_GROUP, N_MIX_HEADS
    d_mix = hh * HEAD_DIM
    kvw_ = g * HEAD_DIM
    assert t % 1024 == 0 and t // SEL_BLOCK <= LANE and t // CMP_STRIDE <= CMP_PAD
    tm = 512
    h = x.astype(F32)

    tab = _bias_tiles(rel_bias)
    tab_nsa = jnp.transpose(tab.reshape(N_BIAS_TILES, g, r, KS, LANE), (1, 0, 2, 3, 4))
    tabc = _bias_cmp_table(rel_bias).reshape(g, r, 2 * CMP_PAD, TQ)
    ovl = _overlap_matrix(t)

    for layer in range(depth):
        j = layer // 2
        wkv = w_mem_kv[layer]
        wkv = _interleave_kv(wkv[:, :N_MEM_HEADS * HEAD_DIM], wkv[:, N_MEM_HEADS * HEAD_DIM:], N_MEM_HEADS)
        kvm = _rms_proj(mem.reshape(bsz * m_len, d), norm_mem[layer], wkv.astype(BF16), BF16, m_len)
        kvm = kvm.reshape(bsz, m_len, N_MEM_HEADS * LANE)
        w_o = w_out[layer]
        w_mem_o = _value_rows(w_o[d_mix:], N_MEM_HEADS).astype(BF16)
        w_mix = w_o[:d_mix].astype(BF16)

        if layer % 2 == 0:
            w = nsa_w_in[j]
            c = np.cumsum([0, d_mix, kvw_, kvw_, kvw_, kvw_, kvw_, kvw_, hh * N_BRANCH,
                           N_MEM_HEADS * HEAD_DIM])
            wq, wkc, wvc, wks, wvs, wkw, wvw, wgl, wqm = [w[:, c[k]:c[k + 1]] for k in range(9)]
            n_g = r * N_BRANCH
            wgl = jnp.concatenate([_pad_cols(wgl[:, gg * n_g:(gg + 1) * n_g], LANE) for gg in range(g)], 1)
            w_all = jnp.concatenate([wq, wkc, wvc, _interleave_kv(wks, wvs, g),
                                     _interleave_kv(wkw, wvw, g), wgl, wqm], axis=1)
            qt, cmp_raw, kvs, vst, kvw, vwt, glt, qmem = _nsa_proj(h, norm_mix[layer], w_all.astype(BF16), tm)
            nc = t // CMP_STRIDE
            x2 = cmp_raw.reshape(bsz, 2 * g, nc, CMP_STRIDE * HEAD_DIM)
            pos = jnp.stack([nsa_cmp_pos_k[j], nsa_cmp_pos_v[j]]).reshape(2, 2, CMP_STRIDE * HEAD_DIM)
            w1 = jnp.stack([nsa_cmp_k_w1[j], nsa_cmp_v_w1[j]]).astype(BF16)
            b1 = jnp.stack([nsa_cmp_k_b1[j], nsa_cmp_v_b1[j]])[:, None, :]
            w2 = jnp.stack([jnp.pad(nsa_cmp_k_w2[j], ((0, 0), (0, HEAD_DIM))),
                            jnp.pad(nsa_cmp_v_w2[j], ((0, 0), (HEAD_DIM, 0)))]).astype(BF16)
            b2 = jnp.stack([jnp.pad(nsa_cmp_k_b2[j], (0, HEAD_DIM)),
                            jnp.pad(nsa_cmp_v_b2[j], (HEAD_DIM, 0))])[:, None, :]
            kvc = _compress(x2, pos, w1, b1, w2, b2)
            kvc = jnp.pad(kvc, ((0, 0), (0, 0), (0, CMP_PAD - nc), (0, 0)))
            vct = jnp.transpose(kvc, (0, 1, 3, 2))
            gb = jnp.pad(nsa_gate_b[j].reshape(g, n_g), ((0, 0), (0, GATE_ROWS - n_g)))
            gbt = jnp.broadcast_to(gb[:, :, None], (g, GATE_ROWS, TQ))
            mix = _nsa_attention(qt, kvc, vct, kvs, vst, kvw, vwt, glt, gbt, tab_nsa, tabc, ovl,
                                 t // SEL_BLOCK)
            w_uv = None
        else:
            w = dsa_w_in[j]
            c = np.cumsum([0, Q_LORA, KV_LORA, IDX_DIM, IDX_HEADS, N_MEM_HEADS * HEAD_DIM])
            wcq, wckv, wki, wwi, wqm = [w[:, c[k]:c[k + 1]] for k in range(5)]
            w_all = jnp.concatenate([wcq, wckv, _pad_cols(wki, LANE), _pad_cols(wwi, LANE),
                                     wqm], axis=1).astype(BF16)
            wuk = jnp.transpose(dsa_w_uk[j], (1, 2, 0))
            wuk = jnp.stack([jnp.pad(wuk[hd], ((HEAD_DIM * (hd % 2), HEAD_DIM * (1 - hd % 2)), (0, 0)))
                             for hd in range(hh)]).astype(BF16)
            qat, qit, wit, ckv, ckvt, kidx, qmem = _dsa_proj(
                h, norm_mix[layer][None], w_all, dsa_q_norm[j][None], dsa_kv_norm[j][None],
                _pad_cols(dsa_kidx_norm[j][None], LANE), dsa_w_q_up[j].astype(BF16), wuk,
                dsa_w_q_idx[j].astype(BF16), tm)
            mix = _dsa_attention(qit, wit, qat, kidx, ckv, ckvt, tab, min(DSA_TOPK, t // 4))
            wv = jnp.transpose(dsa_w_uv[j], (1, 0, 2))
            w_uv = (jnp.eye(hh, dtype=F32)[:, None, :, None] * wv[:, :, None, :]
                    ).reshape(hh * KV_LORA, d_mix).astype(BF16)

        h = _layer_tail(h, mix, qmem, kvm, w_uv, w_mix, w_mem_o, norm_ffn[layer][None],
                        ffn_gate[layer].astype(BF16), ffn_up[layer].astype(BF16),
                        ffn_down[layer].astype(BF16), norm_final[None], layer == depth - 1, tm)
    return h.astype(x.dtype)
```

```python
import functools
import math

import numpy as np
import jax
import jax.numpy as jnp
from jax import lax
from jax.experimental import pallas as pl
from jax.experimental.pallas import tpu as pltpu

F32 = jnp.float32
BF16 = jnp.bfloat16
I32 = jnp.int32

NEG = -1e30
EPS = 1e-6
LOG2E = math.log2(math.e)
LANE = 128
SUBLANE = 8
BF16_ROWS = 16
HEAD_DIM = 64
N_MIX_HEADS = 12
N_MEM_HEADS = 4
N_REL_BUCKETS = 32
REL_MAX_EXACT = 16
REL_MAX_DIST = 2048
NSA_KV_HEADS = 2
NSA_GROUP = N_MIX_HEADS // NSA_KV_HEADS
N_BRANCH = 3
GATE_ROWS = 24
CMP_LEN = 32
CMP_STRIDE = 16
SEL_BLOCK = 64
N_SEL = 16
WINDOW = 512
FORCE_BONUS = 1e4
Q_LORA = 256
KV_LORA = 128
IDX_HEADS = 8
IDX_DIM = 64
DSA_TOPK = 256
TQ = 256
TK = 256
SCORE_TK = 512
KS = LANE
N_BIAS_TILES = REL_MAX_DIST // KS + 2
CMP_PAD = 512
NSA_VT_ROWS = HEAD_DIM
DSA_VT_ROWS = KV_LORA
VMEM_LIMIT = 56 * 1024 * 1024
INT_MIN = -2 ** 31
F32_MIN = float(np.finfo(np.float32).min)


def _dot(a, b):
    return jnp.dot(a, b, preferred_element_type=F32)


def _dot_nt(a, b):
    return lax.dot_general(a, b, (((1,), (1,)), ((), ())), preferred_element_type=F32)


def _rms(x, gain, n=None):
    n = x.shape[-1] if n is None else n
    ms = jnp.sum(x * x, axis=-1, keepdims=True) * (1.0 / n)
    return x * lax.rsqrt(ms + EPS) * gain


def _rel_bucket_np(dist):
    n = np.maximum(dist, 0)
    nf = np.maximum(n, REL_MAX_EXACT).astype(np.float32)
    large = REL_MAX_EXACT + (np.log(nf / np.float32(REL_MAX_EXACT))
                             / np.float32(math.log(REL_MAX_DIST / REL_MAX_EXACT))
                             * np.float32(N_REL_BUCKETS - REL_MAX_EXACT)).astype(np.int32)
    large = np.minimum(large, N_REL_BUCKETS - 1)
    return np.where(n < REL_MAX_EXACT, n, large).astype(np.int32)


def _spread_heads(x):
    rows, n = x.shape
    low = lax.broadcasted_iota(I32, (rows, LANE), 1) < HEAD_DIM
    out = []
    for j in range(n // LANE):
        pair = x[:, j * LANE:(j + 1) * LANE]
        out += [jnp.where(low, pair, 0.0), jnp.where(low, pltpu.roll(pair, HEAD_DIM, axis=1), 0.0)]
    return jnp.concatenate(out, axis=1)


def _colsum(x):
    rows, n = x.shape
    return jnp.sum(jnp.sum(x.reshape(rows // SUBLANE, SUBLANE, n), axis=0), axis=0, keepdims=True)


def _key_to_float(v):
    bits = jnp.where(v >= 0, v, v ^ jnp.int32(0x7FFFFFFF))
    return pltpu.bitcast(bits, F32)


def _topk_cols(score, k, row_idx):
    n = score.shape[1]

    def vbody(it, v):
        cand = v + jnp.left_shift(jnp.int32(1), 31 - it)
        cnt = _colsum((score >= _key_to_float(cand)).astype(F32))
        return jnp.where(cnt >= k, cand, v)

    v = lax.fori_loop(0, 32, vbody, jnp.full((1, n), INT_MIN, I32))
    thr = _key_to_float(v)
    gt = score > thr
    eq = score == thr
    need = k - _colsum(gt.astype(F32))

    def cbody(it, c):
        cand = c + jnp.left_shift(jnp.int32(1), 6 - it)
        cnt = _colsum((eq & (row_idx < cand)).astype(F32))
        return jnp.where(cnt < need, cand, c)

    c = lax.fori_loop(0, 7, cbody, jnp.zeros((1, n), I32))
    return gt | (eq & (row_idx <= c))


def _bias_block(tile_of, m0):
    idx = lambda m: jnp.clip(m, 0, N_BIAS_TILES - 1)
    return [jnp.concatenate([tile_of(idx(m0 + b - a)) for b in range(TQ // LANE)], axis=1)
            for a in range(TK // KS)]


def _attend_tiles(n_heads, n_tiles, q_of, kv_of, vt_of, bias_of, fill_mask, negm_ref, s_ref, tmax_ref,
                  m_ref, l_ref, acc_ref):
    parts = [slice(a * KS, (a + 1) * KS) for a in range(TK // KS)]
    m_ref[...] = jnp.full(m_ref.shape, F32_MIN, F32)
    l_ref[...] = jnp.zeros(l_ref.shape, F32)
    acc_ref[...] = jnp.zeros(acc_ref.shape, F32)

    def logits_of(j):
        fill_mask(j)
        kv = kv_of(j)
        bias_h = bias_of(j)

        def run(h):
            bias = bias_h(h)
            tmax = None
            for a, sl in enumerate(parts):
                s_a = _dot(kv[sl], q_of(h)) + bias[a] + negm_ref[sl, :]
                s_ref[h, sl, :] = s_a
                mx = jnp.max(s_a, axis=0, keepdims=True)
                tmax = mx if tmax is None else jnp.maximum(tmax, mx)
            tmax_ref[h] = tmax
        return run

    def accumulate(j):
        vt = vt_of(j)

        def run(h):
            m_old = m_ref[h]
            m_new = jnp.maximum(m_old, tmax_ref[h])
            alpha = jnp.exp2(m_old - m_new)
            m_ref[h] = m_new
            p = [jnp.exp2(s_ref[h, sl, :] - m_new) for sl in parts]
            l_ref[h] = alpha * l_ref[h] + _colsum(sum(p[1:], p[0]))
            acc = alpha * acc_ref[h]
            for sl, p_a in zip(parts, p):
                acc = acc + _dot(vt[:, sl], p_a.astype(BF16))
            acc_ref[h] = acc
        return run

    first = logits_of(0)
    for h in range(n_heads):
        first(h)

    def body(j, carry):
        consume = accumulate(j)
        produce = logits_of(jnp.minimum(j + 1, n_tiles - 1))
        for h in range(n_heads):
            consume(h)
            produce(h)
        return carry

    lax.fori_loop(0, n_tiles, body, 0)


def _rms_proj_kernel(x_ref, g_ref, w_ref, o_ref):
    y = _rms(x_ref[...], g_ref[...]).astype(BF16)
    o_ref[...] = _dot(y, w_ref[...]).astype(o_ref.dtype)


def _rms_proj(x2d, gain, w_bf16, out_dtype, tm):
    m, d = x2d.shape
    n = w_bf16.shape[1]
    return pl.pallas_call(
        _rms_proj_kernel,
        grid=(m // tm,),
        in_specs=[pl.BlockSpec((tm, d), lambda i: (i, 0)),
                  pl.BlockSpec((1, d), lambda i: (0, 0)),
                  pl.BlockSpec((d, n), lambda i: (0, 0))],
        out_specs=pl.BlockSpec((tm, n), lambda i: (i, 0)),
        out_shape=jax.ShapeDtypeStruct((m, n), out_dtype),
        compiler_params=pltpu.CompilerParams(dimension_semantics=("arbitrary",),
                                             vmem_limit_bytes=VMEM_LIMIT),
        name="rms_proj",
    )(x2d, gain.reshape(1, d), w_bf16)


def _nsa_proj_kernel(x_ref, g_ref, w_ref, qt_ref, cmp_ref, kvs_ref, vst_ref, kvw_ref, vwt_ref,
                     glt_ref, qmem_ref):
    tm = x_ref.shape[1]
    y = _rms(x_ref[0], g_ref[...]).astype(BF16)
    value_row = lax.broadcasted_iota(I32, (LANE, tm), 0) >= HEAD_DIM
    off = 0
    for pair in range(N_MIX_HEADS // 2):
        acc_t = (_dot(y, w_ref[:, off:off + LANE]) * (HEAD_DIM ** -0.5 * LOG2E)).T
        qt_ref[0, 2 * pair] = jnp.where(value_row, 0.0, acc_t).astype(BF16)
        qt_ref[0, 2 * pair + 1] = jnp.where(value_row, 0.0, pltpu.roll(acc_t, HEAD_DIM, axis=0)).astype(BF16)
        off += LANE
    raw = _dot(y, w_ref[:, off:off + 2 * LANE])
    for piece in range(2 * NSA_KV_HEADS):
        cmp_ref[0, piece] = raw[:, piece * HEAD_DIM:(piece + 1) * HEAD_DIM]
    off += 2 * LANE
    for kv_ref, vt_ref in ((kvs_ref, vst_ref), (kvw_ref, vwt_ref)):
        for g in range(NSA_KV_HEADS):
            acc = _dot(y, w_ref[:, off:off + LANE])
            off += LANE
            kv_ref[0, :, g * LANE:(g + 1) * LANE] = acc.astype(BF16)
            v_t = acc.T[HEAD_DIM:, :].astype(BF16)
            for jt in range(tm // TK):
                vt_ref[0, g, jt] = v_t[:, jt * TK:(jt + 1) * TK]
    for g in range(NSA_KV_HEADS):
        acc = _dot(y, w_ref[:, off:off + LANE])
        off += LANE
        glt_ref[0, g] = acc.T[:GATE_ROWS, :]
    qmem_ref[0] = _spread_heads(_dot(y, w_ref[:, off:off + N_MEM_HEADS * HEAD_DIM]) * HEAD_DIM ** -0.5
                                ).astype(BF16)


def _nsa_proj(h, gain, w, tm):
    bsz, t, d = h.shape
    g = NSA_KV_HEADS
    nkt = t // TK
    row = lambda n: pl.BlockSpec((1, tm, n), lambda b, i: (b, i, 0))
    vt_spec = pl.BlockSpec((1, g, tm // TK, NSA_VT_ROWS, TK), lambda b, i: (b, 0, i, 0, 0))
    return pl.pallas_call(
        _nsa_proj_kernel,
        grid=(bsz, t // tm),
        in_specs=[pl.BlockSpec((1, tm, d), lambda b, i: (b, i, 0)),
                  pl.BlockSpec((1, d), lambda b, i: (0, 0)),
                  pl.BlockSpec(w.shape, lambda b, i: (0, 0))],
        out_specs=[pl.BlockSpec((1, N_MIX_HEADS, LANE, tm), lambda b, i: (b, 0, 0, i)),
                   pl.BlockSpec((1, 2 * g, tm, HEAD_DIM), lambda b, i: (b, 0, i, 0)),
                   row(g * LANE), vt_spec, row(g * LANE), vt_spec,
                   pl.BlockSpec((1, g, GATE_ROWS, tm), lambda b, i: (b, 0, 0, i)),
                   row(N_MEM_HEADS * LANE)],
        out_shape=[jax.ShapeDtypeStruct((bsz, N_MIX_HEADS, LANE, t), BF16),
                   jax.ShapeDtypeStruct((bsz, 2 * g, t, HEAD_DIM), F32),
                   jax.ShapeDtypeStruct((bsz, t, g * LANE), BF16),
                   jax.ShapeDtypeStruct((bsz, g, nkt, NSA_VT_ROWS, TK), BF16),
                   jax.ShapeDtypeStruct((bsz, t, g * LANE), BF16),
                   jax.ShapeDtypeStruct((bsz, g, nkt, NSA_VT_ROWS, TK), BF16),
                   jax.ShapeDtypeStruct((bsz, g, GATE_ROWS, t), F32),
                   jax.ShapeDtypeStruct((bsz, t, N_MEM_HEADS * LANE), BF16)],
        compiler_params=pltpu.CompilerParams(dimension_semantics=("arbitrary", "arbitrary"),
                                             vmem_limit_bytes=VMEM_LIMIT),
        name="nsa_proj",
    )(h, gain.reshape(1, d), w)


def _compress_kernel(xk_ref, xv_ref, pos_ref, w1_ref, b1_ref, w2_ref, b2_ref, o_ref):
    nc = xk_ref.shape[2]
    half = xk_ref.shape[3]
    out = None
    for j, x_ref in enumerate((xk_ref, xv_ref)):
        x = x_ref[0, 0]
        top = _dot((x + pos_ref[j, 0:1, :]).astype(BF16), w1_ref[j, :half, :])
        bot = _dot((x + pos_ref[j, 1:2, :]).astype(BF16), w1_ref[j, half:, :])
        pre = top + pltpu.roll(bot, nc - 1, axis=0) + b1_ref[j]
        hid = jax.nn.gelu(pre)
        res = _dot(hid.astype(BF16), w2_ref[j]) + b2_ref[j]
        out = res if out is None else out + res
    o_ref[0, 0] = out.astype(o_ref.dtype)


def _compress(x2, pos, w1, b1, w2, b2):
    bsz, _, nc, width = x2.shape
    g = NSA_KV_HEADS
    return pl.pallas_call(
        _compress_kernel,
        grid=(bsz, g),
        in_specs=[pl.BlockSpec((1, 1, nc, width), lambda b, gg: (b, gg, 0, 0)),
                  pl.BlockSpec((1, 1, nc, width), lambda b, gg: (b, gg + NSA_KV_HEADS, 0, 0)),
                  pl.BlockSpec(pos.shape, lambda b, gg: (0, 0, 0)),
                  pl.BlockSpec(w1.shape, lambda b, gg: (0, 0, 0)),
                  pl.BlockSpec(b1.shape, lambda b, gg: (0, 0, 0)),
                  pl.BlockSpec(w2.shape, lambda b, gg: (0, 0, 0)),
                  pl.BlockSpec(b2.shape, lambda b, gg: (0, 0, 0))],
        out_specs=pl.BlockSpec((1, 1, nc, LANE), lambda b, gg: (b, gg, 0, 0)),
        out_shape=jax.ShapeDtypeStruct((bsz, g, nc, LANE), BF16),
        compiler_params=pltpu.CompilerParams(dimension_semantics=("arbitrary", "arbitrary"),
                                             vmem_limit_bytes=VMEM_LIMIT),
        name="nsa_compress",
    )(x2, x2, pos, w1, b1, w2, b2)


def _nsa_kernel(q_ref, kvc_ref, vct_ref, kvs_ref, vst_ref, kvw_ref, vwt_ref, glt_ref, gbt_ref,
                tab_ref, tabc_ref, ovl_ref, o_ref,
                psum_ref, negsel_ref, negm_ref, s_ref, tmax_ref, m_ref, l_ref, acc_ref, ocmp_ref, oslc_ref,
                *, n_sel):
    r_heads = NSA_GROUP
    i = pl.program_id(2)
    qs = i * TQ
    t_row = qs + lax.broadcasted_iota(I32, (1, TQ), 1)
    krow = lax.broadcasted_iota(I32, (TK, TQ), 0)
    q_of = lambda h: q_ref[0, h]

    def normalised(h):
        return acc_ref[h] * (1.0 / l_ref[h])

    kvc = kvc_ref[0, 0]
    vct = vct_ref[0, 0]
    n_row = lax.broadcasted_iota(I32, (CMP_PAD, TQ), 0)
    negc = jnp.where((CMP_STRIDE * n_row + (CMP_LEN - 1)) <= t_row, 0.0, -jnp.inf)
    j0 = pl.multiple_of(CMP_PAD - (TQ // CMP_STRIDE) * i, SUBLANE)
    psum_ref[...] = jnp.zeros(psum_ref.shape, F32)

    cparts = [slice(a * KS, (a + 1) * KS) for a in range(CMP_PAD // KS)]
    for h in range(r_heads):
        s = [_dot(kvc[sl], q_of(h)) + tabc_ref[0, h, pl.ds(j0 + sl.start, KS), :] + negc[sl] for sl in cparts]
        m = functools.reduce(jnp.maximum, [jnp.max(s_a, axis=0, keepdims=True) for s_a in s])
        m = jnp.maximum(m, F32_MIN)
        e = [jnp.exp2(s_a - m) for s_a in s]
        den = _colsum(functools.reduce(jnp.add, e))
        inv = 1.0 / jnp.where(den > 0.0, den, 1.0)
        acc = functools.reduce(jnp.add, [_dot(vct[:, sl], e_a.astype(BF16)) for sl, e_a in zip(cparts, e)])
        ocmp_ref[h] = acc[LANE - HEAD_DIM:] * inv
        for sl, e_a in zip(cparts, e):
            psum_ref[sl, :] += e_a * inv

    psum = psum_ref[...]
    ovl = ovl_ref[...]
    hi = psum.astype(BF16)
    rem1 = psum - hi.astype(F32)
    mid = rem1.astype(BF16)
    lo = (rem1 - mid.astype(F32)).astype(BF16)
    p_slc = _dot(ovl, hi) + _dot(ovl, mid) + _dot(ovl, lo)
    blk = lax.broadcasted_iota(I32, (LANE, TQ), 0)
    cur = jnp.right_shift(t_row, 6)
    forced = (blk == 0) | (blk == cur) | (blk == cur - 1)
    admissible = (blk * SEL_BLOCK) <= t_row
    score = jnp.where(admissible, p_slc + jnp.where(forced, FORCE_BONUS, 0.0), NEG)
    score = jnp.where(blk < n_sel, score, -jnp.inf)
    sel = _topk_cols(score, min(N_SEL, n_sel), blk) & (score > 0.5 * NEG)
    negsel_ref[...] = jnp.where(sel, 0.0, -jnp.inf)

    def bias_of(kt):
        return lambda h: _bias_block(lambda m: tab_ref[0, m, h], (TQ // KS) * i - (TK // KS) * kt)

    def key_rows(ref, kt):
        return ref[0, pl.ds(pl.multiple_of(kt * TK, TK), TK), :]

    last_tile = (qs + TQ - 1) // TK

    def sel_mask(kt):
        rows = [jnp.broadcast_to(negsel_ref[pl.ds((TK // SEL_BLOCK) * kt + a, 1), :], (SEL_BLOCK, TQ))
                for a in range(TK // SEL_BLOCK)]
        negm_ref[...] = jnp.where((kt * TK + krow) <= t_row, jnp.concatenate(rows, axis=0), -jnp.inf)

    _attend_tiles(r_heads, last_tile + 1, q_of, lambda kt: key_rows(kvs_ref, kt), lambda kt: vst_ref[0, 0, kt],
                  bias_of, sel_mask, negm_ref, s_ref, tmax_ref, m_ref, l_ref, acc_ref)
    for h in range(r_heads):
        oslc_ref[h] = normalised(h)

    n_win = max((TQ * e + TQ - 1) // TK - (TQ * e - WINDOW + 1) // TK + 1 for e in range(max(TK // TQ, 1)))

    def win_mask(j):
        kt = last_tile - j
        dist = t_row - (jnp.maximum(kt, 0) * TK + krow)
        ok = (dist >= 0) & (dist < jnp.where(kt >= 0, WINDOW, 0))
        negm_ref[...] = jnp.where(ok, 0.0, -jnp.inf)

    _attend_tiles(r_heads, n_win, q_of, lambda j: key_rows(kvw_ref, jnp.maximum(last_tile - j, 0)),
                  lambda j: vwt_ref[0, 0, jnp.maximum(last_tile - j, 0)], lambda j: bias_of(last_tile - j),
                  win_mask, negm_ref, s_ref, tmax_ref, m_ref, l_ref, acc_ref)

    gates = jax.nn.sigmoid(glt_ref[0, 0] + gbt_ref[0])
    for pair in range(r_heads // 2):
        outs = []
        for r in (2 * pair, 2 * pair + 1):
            c = N_BRANCH * r
            outs.append(gates[c:c + 1] * ocmp_ref[r] + gates[c + 1:c + 2] * oslc_ref[r]
                        + gates[c + 2:c + 3] * normalised(r))
        o_ref[0, :, pair * LANE:(pair + 1) * LANE] = jnp.concatenate(outs, axis=0).T.astype(o_ref.dtype)


def _nsa_attention(qt, kvc, vct, kvs, vst, kvw, vwt, glt, gbt, tab, tabc, ovl, n_sel):
    bsz, _, _, t = qt.shape
    g, r = NSA_KV_HEADS, NSA_GROUP
    nkt = t // TK
    once = dict(pipeline_mode=pl.Buffered(1))
    return pl.pallas_call(
        functools.partial(_nsa_kernel, n_sel=n_sel),
        grid=(bsz, g, t // TQ),
        in_specs=[pl.BlockSpec((1, r, LANE, TQ), lambda b, gg, i: (b, gg, 0, i)),
                  pl.BlockSpec((1, 1, CMP_PAD, LANE), lambda b, gg, i: (b, gg, 0, 0)),
                  pl.BlockSpec((1, 1, LANE, CMP_PAD), lambda b, gg, i: (b, gg, 0, 0)),
                  pl.BlockSpec((1, t, LANE), lambda b, gg, i: (b, 0, gg)),
                  pl.BlockSpec((1, 1, nkt, NSA_VT_ROWS, TK), lambda b, gg, i: (b, gg, 0, 0, 0)),
                  pl.BlockSpec((1, t, LANE), lambda b, gg, i: (b, 0, gg)),
                  pl.BlockSpec((1, 1, nkt, NSA_VT_ROWS, TK), lambda b, gg, i: (b, gg, 0, 0, 0)),
                  pl.BlockSpec((1, 1, GATE_ROWS, TQ), lambda b, gg, i: (b, gg, 0, i)),
                  pl.BlockSpec((1, GATE_ROWS, TQ), lambda b, gg, i: (gg, 0, 0)),
                  pl.BlockSpec((1, N_BIAS_TILES, r, KS, LANE), lambda b, gg, i: (gg, 0, 0, 0, 0), **once),
                  pl.BlockSpec((1, r, 2 * CMP_PAD, TQ), lambda b, gg, i: (gg, 0, 0, 0), **once),
                  pl.BlockSpec((LANE, CMP_PAD), lambda b, gg, i: (0, 0))],
        out_specs=pl.BlockSpec((1, TQ, r * HEAD_DIM), lambda b, gg, i: (b, i, gg)),
        out_shape=jax.ShapeDtypeStruct((bsz, t, g * r * HEAD_DIM), BF16),
        scratch_shapes=[pltpu.VMEM((CMP_PAD, TQ), F32), pltpu.VMEM((LANE, TQ), F32),
                        pltpu.VMEM((TK, TQ), F32), pltpu.VMEM((r, TK, TQ), F32),
                        pltpu.VMEM((r, 1, TQ), F32), pltpu.VMEM((r, 1, TQ), F32), pltpu.VMEM((r, 1, TQ), F32),
                        pltpu.VMEM((r, NSA_VT_ROWS, TQ), F32), pltpu.VMEM((r, HEAD_DIM, TQ), F32),
                        pltpu.VMEM((r, HEAD_DIM, TQ), F32)],
        compiler_params=pltpu.CompilerParams(
            dimension_semantics=("arbitrary", "arbitrary", "arbitrary"),
            vmem_limit_bytes=VMEM_LIMIT),
        name="nsa_attention",
    )(qt, kvc, vct, kvs, vst, kvw, vwt, glt, gbt, tab, tabc, ovl)


def _dsa_proj_kernel(x_ref, g_ref, w_ref, qn_ref, kvn_ref, kin_ref, wqu_ref, wuk_ref, wqi_ref,
                     qat_ref, qit_ref, wit_ref, ckv_ref, ckvt_ref, kidx_ref, qmem_ref):
    tm = x_ref.shape[1]
    y = _rms(x_ref[0], g_ref[...]).astype(BF16)
    c_q = _rms(_dot(y, w_ref[:, 0:Q_LORA]), qn_ref[...]).astype(BF16)
    c_kv = _rms(_dot(y, w_ref[:, Q_LORA:Q_LORA + KV_LORA]), kvn_ref[...])
    ckv_ref[0] = c_kv.astype(BF16)
    c_kv_t = c_kv.T.astype(BF16)
    for jt in range(tm // TK):
        ckvt_ref[0, jt] = c_kv_t[:, jt * TK:(jt + 1) * TK]
    off = Q_LORA + KV_LORA
    k_idx = _rms(_dot(y, w_ref[:, off:off + LANE]), kin_ref[...], n=IDX_DIM)
    kidx_ref[0] = k_idx.astype(BF16)
    off += LANE
    w_idx = _dot(y, w_ref[:, off:off + LANE]) * (IDX_HEADS ** -0.5 * IDX_DIM ** -0.5)
    wit_ref[0] = w_idx.T[:IDX_HEADS, :]
    off += LANE
    qmem_ref[0] = _spread_heads(_dot(y, w_ref[:, off:off + N_MEM_HEADS * HEAD_DIM]) * HEAD_DIM ** -0.5
                                ).astype(BF16)
    value_row = lax.broadcasted_iota(I32, (LANE, tm), 0) >= IDX_DIM
    for pair in range(IDX_HEADS // 2):
        acc_t = _dot(c_q, wqi_ref[:, pair * LANE:(pair + 1) * LANE]).T
        qit_ref[0, 2 * pair] = jnp.where(value_row, 0.0, acc_t).astype(BF16)
        qit_ref[0, 2 * pair + 1] = jnp.where(value_row, 0.0, pltpu.roll(acc_t, IDX_DIM, axis=0)).astype(BF16)
    for pair in range(N_MIX_HEADS // 2):
        q_pair = (_dot(c_q, wqu_ref[:, pair * LANE:(pair + 1) * LANE]) * HEAD_DIM ** -0.5).astype(BF16)
        for h in (2 * pair, 2 * pair + 1):
            qat_ref[0, h] = (_dot(q_pair, wuk_ref[h]) * LOG2E).T.astype(BF16)


def _dsa_proj(h, gain, w, qn, kvn, kin, wqu, wuk, wqi, tm):
    bsz, t, d = h.shape
    nkt = t // TK
    full = lambda a: pl.BlockSpec(a.shape, lambda b, i: (0,) * a.ndim)
    row = lambda n: pl.BlockSpec((1, tm, n), lambda b, i: (b, i, 0))
    return pl.pallas_call(
        _dsa_proj_kernel,
        grid=(bsz, t // tm),
        in_specs=[pl.BlockSpec((1, tm, d), lambda b, i: (b, i, 0)), full(gain), full(w), full(qn),
                  full(kvn), full(kin), full(wqu), full(wuk), full(wqi)],
        out_specs=[pl.BlockSpec((1, N_MIX_HEADS, LANE, tm), lambda b, i: (b, 0, 0, i)),
                   pl.BlockSpec((1, IDX_HEADS, LANE, tm), lambda b, i: (b, 0, 0, i)),
                   pl.BlockSpec((1, IDX_HEADS, tm), lambda b, i: (b, 0, i)),
                   row(LANE),
                   pl.BlockSpec((1, tm // TK, DSA_VT_ROWS, TK), lambda b, i: (b, i, 0, 0)),
                   row(LANE), row(N_MEM_HEADS * LANE)],
        out_shape=[jax.ShapeDtypeStruct((bsz, N_MIX_HEADS, LANE, t), BF16),
                   jax.ShapeDtypeStruct((bsz, IDX_HEADS, LANE, t), BF16),
                   jax.ShapeDtypeStruct((bsz, IDX_HEADS, t), F32),
                   jax.ShapeDtypeStruct((bsz, t, LANE), BF16),
                   jax.ShapeDtypeStruct((bsz, nkt, DSA_VT_ROWS, TK), BF16),
                   jax.ShapeDtypeStruct((bsz, t, LANE), BF16),
                   jax.ShapeDtypeStruct((bsz, t, N_MEM_HEADS * LANE), BF16)],
        compiler_params=pltpu.CompilerParams(dimension_semantics=("arbitrary", "arbitrary"),
                                             vmem_limit_bytes=VMEM_LIMIT),
        name="dsa_proj",
    )(h, gain, w, qn, kvn, kin, wqu, wuk, wqi)


def _dsa_kernel(qi_ref, wi_ref, qa_ref, kidx_ref, ckv_ref, ckvt_ref, tab_ref, o_ref,
                sc_ref, sc_hi_ref, negm_ref, s_ref, tmax_ref, m_ref, l_ref, acc_ref, *, topk):
    i = pl.program_id(1)
    qs = i * TQ
    n_tiles = (qs + TQ - 1) // TK + 1
    sub = TK // KS
    n_steps = (qs + TQ - 1) // SCORE_TK + 1
    per_step = SCORE_TK // KS
    t_row = qs + lax.broadcasted_iota(I32, (1, TQ), 1)
    krow = lax.broadcasted_iota(I32, (KS, TQ), 0)
    k_f = float(topk)

    grouped = lambda x: x.reshape(KS // SUBLANE, SUBLANE, TQ)

    def score_body(ks, carry):
        for a in range(per_step):
            k0 = pl.multiple_of(ks * SCORE_TK + a * KS, KS)
            kk = kidx_ref[0, pl.ds(k0, KS), :]
            sc = jnp.maximum(_dot(kk, qi_ref[0, 0]), 0.0) * wi_ref[0, 0:1, :]
            for h in range(1, IDX_HEADS):
                sc = sc + jnp.maximum(_dot(kk, qi_ref[0, h]), 0.0) * wi_ref[0, h:h + 1, :]
            sc = jnp.where((k0 + krow) <= t_row, sc, NEG)
            sc_ref[per_step * ks + a] = sc
            sc_hi_ref[per_step * ks + a] = sc.astype(BF16)
        return carry

    lax.fori_loop(0, n_steps, score_body, 0)

    def count(pred):
        def body(ks, acc):
            for a in range(per_step):
                hit = pred(sc_ref[per_step * ks + a], ks * SCORE_TK + a * KS).astype(F32)
                acc = acc + jnp.sum(grouped(hit), axis=0)
            return acc
        acc = lax.fori_loop(0, n_steps, body, jnp.zeros((SUBLANE, TQ), F32))
        return jnp.sum(acc, axis=0, keepdims=True)

    def count_rounded(cand_f):
        cand = jnp.broadcast_to(cand_f, (BF16_ROWS, TQ)).astype(BF16)
        one, zero = jnp.ones((), BF16), jnp.zeros((), BF16)

        def body(ks, acc):
            for a in range(per_step):
                tile = sc_hi_ref[per_step * ks + a].reshape(KS // BF16_ROWS, BF16_ROWS, TQ)
                hit = jnp.where(tile >= cand[None], one, zero)
                part = functools.reduce(jnp.add, [hit[r] for r in range(KS // BF16_ROWS)])
                acc = acc + part.astype(F32)
            return acc
        acc = lax.fori_loop(0, n_steps, body, jnp.zeros((BF16_ROWS, TQ), F32))
        return jnp.sum(acc, axis=0, keepdims=True)

    short = t_row < topk
    half_step = 1 << 15

    def bf16_key_to_float(v16):
        raw16 = jnp.where(v16 >= 0, v16, v16 ^ jnp.int32(0x7FFF))
        return pltpu.bitcast(jnp.left_shift(raw16, 16), F32)

    def high_body(it, v16):
        cand = v16 + jnp.left_shift(jnp.int32(1), 15 - it)
        return jnp.where(count_rounded(bf16_key_to_float(cand)) >= k_f, cand, v16)

    v16 = lax.fori_loop(0, 16, high_body, jnp.full((1, TQ), -(1 << 15), I32))
    key_g = jnp.where(v16 >= 0, jnp.left_shift(v16, 16), jnp.left_shift(v16, 16) | jnp.int32(0xFFFF))

    def low_pass(bit, v, cnt_v):
        cand = v + jnp.left_shift(jnp.int32(1), bit)
        cand_f = _key_to_float(cand)
        cnt = count(lambda sc, k0: sc >= cand_f)
        take = cnt >= k_f
        return jnp.where(take, cand, v), jnp.where(take, cnt, cnt_v)

    def open_count(cnt_v):
        return jnp.sum(((cnt_v != k_f) & ~short).astype(I32))

    passes_per_check = 1
    v, cnt_v = low_pass(16, key_g - half_step, jnp.full((1, TQ), 1e9, F32))

    def v_cond(c):
        return (c[0] >= 0) & (c[3] > 0)

    def v_body(c):
        bit, v, cnt_v, _ = c
        for step in range(passes_per_check):
            v, cnt_v = low_pass(bit - step, v, cnt_v)
        return bit - passes_per_check, v, cnt_v, open_count(cnt_v)

    _, v, cnt_v, open_cols = lax.while_loop(v_cond, v_body, (jnp.int32(15), v, cnt_v, open_count(cnt_v)))
    thr = _key_to_float(v)

    def tie_cut(_):
        need = k_f - count(lambda sc, k0: sc > thr)

        def c_body(it, c):
            cand = c + jnp.left_shift(jnp.int32(1), 13 - it)
            cnt = count(lambda sc, k0: (sc == thr) & ((k0 + krow) < cand))
            return jnp.where(cnt < need, cand, c)

        return lax.fori_loop(0, 14, c_body, jnp.zeros((1, TQ), I32))

    cut = lax.cond(open_cols > 0, tie_cut, lambda _: jnp.full((1, TQ), 2 ** 30, I32), 0)

    def att_mask(kt):
        for a in range(sub):
            sc = sc_ref[sub * kt + a]
            kpos = kt * TK + a * KS + krow
            chosen = short | (sc > thr) | ((sc == thr) & (kpos <= cut))
            negm_ref[a * KS:(a + 1) * KS, :] = jnp.where(chosen & (kpos <= t_row), 0.0, -jnp.inf)

    _attend_tiles(N_MIX_HEADS, n_tiles, lambda h: qa_ref[0, h],
                  lambda kt: ckv_ref[0, pl.ds(pl.multiple_of(kt * TK, TK), TK), :],
                  lambda kt: ckvt_ref[0, kt],
                  lambda kt: (lambda h: _bias_block(lambda m: tab_ref[m, h],
                                                    (TQ // KS) * i - (TK // KS) * kt)),
                  att_mask, negm_ref, s_ref, tmax_ref, m_ref, l_ref, acc_ref)
    for h in range(N_MIX_HEADS):
        out = acc_ref[h] * (1.0 / l_ref[h])
        o_ref[0, :, h * LANE:(h + 1) * LANE] = out.T.astype(o_ref.dtype)


def _dsa_attention(qit, wit, qat, kidx, ckv, ckvt, tab, topk):
    bsz, _, _, t = qat.shape
    nkt = t // TK
    return pl.pallas_call(
        functools.partial(_dsa_kernel, topk=topk),
        grid=(bsz, t // TQ),
        in_specs=[pl.BlockSpec((1, IDX_HEADS, LANE, TQ), lambda b, i: (b, 0, 0, i)),
                  pl.BlockSpec((1, IDX_HEADS, TQ), lambda b, i: (b, 0, i)),
                  pl.BlockSpec((1, N_MIX_HEADS, LANE, TQ), lambda b, i: (b, 0, 0, i)),
                  pl.BlockSpec((1, t, LANE), lambda b, i: (b, 0, 0), pipeline_mode=pl.Buffered(1)),
                  pl.BlockSpec((1, t, LANE), lambda b, i: (b, 0, 0), pipeline_mode=pl.Buffered(1)),
                  pl.BlockSpec((1, nkt, DSA_VT_ROWS, TK), lambda b, i: (b, 0, 0, 0),
                               pipeline_mode=pl.Buffered(1)),
                  pl.BlockSpec(tab.shape, lambda b, i: (0, 0, 0, 0), pipeline_mode=pl.Buffered(1))],
        out_specs=pl.BlockSpec((1, TQ, N_MIX_HEADS * LANE), lambda b, i: (b, i, 0)),
        out_shape=jax.ShapeDtypeStruct((bsz, t, N_MIX_HEADS * LANE), BF16),
        scratch_shapes=[pltpu.VMEM((t // KS, KS, TQ), F32), pltpu.VMEM((t // KS, KS, TQ), BF16),
                        pltpu.VMEM((TK, TQ), F32),
                        pltpu.VMEM((N_MIX_HEADS, TK, TQ), F32), pltpu.VMEM((N_MIX_HEADS, 1, TQ), F32),
                        pltpu.VMEM((N_MIX_HEADS, 1, TQ), F32), pltpu.VMEM((N_MIX_HEADS, 1, TQ), F32),
                        pltpu.VMEM((N_MIX_HEADS, DSA_VT_ROWS, TQ), F32)],
        compiler_params=pltpu.CompilerParams(dimension_semantics=("arbitrary", "arbitrary"),
                                             vmem_limit_bytes=VMEM_LIMIT),
        name="dsa_attention",
    )(qit, wit, qat, kidx, ckv, ckvt, tab)


def _tail_kernel(h_ref, mix_ref, qmem_ref, kvm_ref, *rest, has_uv, final_norm):
    if has_uv:
        wuv_ref, *rest = rest
    wmix_ref, wmem_ref, g_ref, wg_ref, wu_ref, wd_ref, gf_ref, o_ref = rest
    mix = mix_ref[0]
    if has_uv:
        mix = _dot(mix, wuv_ref[...]).astype(BF16)
    upd = _dot(mix, wmix_ref[...])
    qm = qmem_ref[0]
    for hm in range(N_MEM_HEADS):
        sl = slice(hm * LANE, (hm + 1) * LANE)
        kv = kvm_ref[0, :, sl]
        s = _dot_nt(qm[:, sl], kv)
        e = jnp.exp(s - jnp.max(s, axis=-1, keepdims=True))
        p = e / jnp.sum(e, axis=-1, keepdims=True)
        o_h = _dot(p.astype(BF16), kv).astype(BF16)
        upd = upd + _dot(o_h, wmem_ref[sl, :])
    h = h_ref[0] + upd
    hn = _rms(h, g_ref[...]).astype(BF16)
    act = (jax.nn.silu(_dot(hn, wg_ref[...])) * _dot(hn, wu_ref[...])).astype(BF16)
    out = h + _dot(act, wd_ref[...])
    if final_norm:
        out = _rms(out, gf_ref[...])
    o_ref[0] = out


def _layer_tail(h, mix, qmem, kvm, w_uv, w_mix, w_mem, gain, wg, wu, wd, gain_final, final_norm, tm):
    bsz, t, d = h.shape
    has_uv = w_uv is not None
    const = lambda a: pl.BlockSpec(a.shape, lambda b, i: (0,) * a.ndim, pipeline_mode=pl.Buffered(1))
    weights = ([w_uv] if has_uv else []) + [w_mix, w_mem, gain, wg, wu, wd, gain_final]
    return pl.pallas_call(
        functools.partial(_tail_kernel, has_uv=has_uv, final_norm=final_norm),
        grid=(bsz, t // tm),
        in_specs=[pl.BlockSpec((1, tm, d), lambda b, i: (b, i, 0)),
                  pl.BlockSpec((1, tm, mix.shape[2]), lambda b, i: (b, i, 0)),
                  pl.BlockSpec((1, tm, qmem.shape[2]), lambda b, i: (b, i, 0)),
                  pl.BlockSpec((1,) + kvm.shape[1:], lambda b, i: (b, 0, 0))]
                 + [const(w) for w in weights],
        out_specs=pl.BlockSpec((1, tm, d), lambda b, i: (b, i, 0)),
        out_shape=jax.ShapeDtypeStruct((bsz, t, d), F32),
        compiler_params=pltpu.CompilerParams(dimension_semantics=("arbitrary", "arbitrary"),
                                             vmem_limit_bytes=VMEM_LIMIT),
        name="layer_tail",
    )(h, mix, qmem, kvm, *weights)


def _pad_cols(w, n):
    return jnp.pad(w, ((0, 0), (0, n - w.shape[1])))


def _value_rows(w_rows, n_heads):
    d_out = w_rows.shape[1]
    w = w_rows.reshape(n_heads, HEAD_DIM, d_out)
    return jnp.pad(w, ((0, 0), (LANE - HEAD_DIM, 0), (0, 0))).reshape(n_heads * LANE, d_out)


def _interleave_kv(k, v, n_heads):
    d_in = k.shape[0]
    kv = jnp.concatenate([k.reshape(d_in, n_heads, HEAD_DIM), v.reshape(d_in, n_heads, HEAD_DIM)], axis=2)
    return kv.reshape(d_in, n_heads * LANE)


def _bucket_bias(rel_bias, bucket_np):
    onehot = jax.nn.one_hot(jnp.asarray(bucket_np.reshape(-1)), N_REL_BUCKETS, dtype=F32)
    out = jnp.dot(onehot, rel_bias, precision=lax.Precision.HIGHEST) * LOG2E
    return out.reshape(bucket_np.shape + (rel_bias.shape[1],))


def _bias_tiles(rel_bias):
    m = np.arange(N_BIAS_TILES)[:, None, None]
    dist = LANE * m + np.arange(LANE)[None, None, :] - np.arange(KS)[None, :, None]
    return jnp.transpose(_bucket_bias(rel_bias, _rel_bucket_np(dist)), (0, 3, 1, 2))


def _bias_cmp_table(rel_bias):
    rel = np.arange(2 * CMP_PAD) - CMP_PAD
    dist = np.arange(TQ)[None, :] - CMP_STRIDE * rel[:, None] - (CMP_LEN - 1)
    return jnp.transpose(_bucket_bias(rel_bias, _rel_bucket_np(dist)), (2, 0, 1))


def _overlap_matrix(t):
    n_cmp = (t - CMP_LEN) // CMP_STRIDE + 1
    n_sel = t // SEL_BLOCK
    cs = np.arange(CMP_PAD) * CMP_STRIDE
    ss = np.arange(LANE) * SEL_BLOCK
    ov = (cs[None, :] <= ss[:, None] + SEL_BLOCK - 1) & (cs[None, :] + CMP_LEN - 1 >= ss[:, None])
    ov &= (np.arange(CMP_PAD) < n_cmp)[None, :] & (np.arange(LANE) < n_sel)[:, None]
    return jnp.asarray(ov, BF16)


def kernel(x, mem, rel_bias, norm_mix, norm_ffn, norm_mem, w_mem_kv, w_out, ffn_gate, ffn_up, ffn_down,
           nsa_w_in, nsa_gate_b, nsa_cmp_pos_k, nsa_cmp_pos_v,
           nsa_cmp_k_w1, nsa_cmp_k_b1, nsa_cmp_k_w2, nsa_cmp_k_b2,
           nsa_cmp_v_w1, nsa_cmp_v_b1, nsa_cmp_v_w2, nsa_cmp_v_b2,
           dsa_w_in, dsa_q_norm, dsa_kv_norm, dsa_w_q_up, dsa_w_uk, dsa_w_uv, dsa_w_q_idx, dsa_kidx_norm,
           norm_final):
    bsz, t, d = x.shape
    m_len = mem.shape[1]
    depth = norm_mix.shape[0]
    g, r, hh = NSA_KV_HEADS, NSA_GROUP, N_MIX_HEADS
    d_mix = hh * HEAD_DIM
    kvw_ = g * HEAD_DIM
    assert t % 1024 == 0 and t // SEL_BLOCK <= LANE and t // CMP_STRIDE <= CMP_PAD
    tm = 512
    h = x.astype(F32)

    tab = _bias_tiles(rel_bias)
    tab_nsa = jnp.transpose(tab.reshape(N_BIAS_TILES, g, r, KS, LANE), (1, 0, 2, 3, 4))
    tabc = _bias_cmp_table(rel_bias).reshape(g, r, 2 * CMP_PAD, TQ)
    ovl = _overlap_matrix(t)

    for layer in range(depth):
        j = layer // 2
        wkv = w_mem_kv[layer]
        wkv = _interleave_kv(wkv[:, :N_MEM_HEADS * HEAD_DIM], wkv[:, N_MEM_HEADS * HEAD_DIM:], N_MEM_HEADS)
        kvm = _rms_proj(mem.reshape(bsz * m_len, d), norm_mem[layer], wkv.astype(BF16), BF16, m_len)
        kvm = kvm.reshape(bsz, m_len, N_MEM_HEADS * LANE)
        w_o = w_out[layer]
        w_mem_o = _value_rows(w_o[d_mix:], N_MEM_HEADS).astype(BF16)
        w_mix = w_o[:d_mix].astype(BF16)

        if layer % 2 == 0:
            w = nsa_w_in[j]
            c = np.cumsum([0, d_mix, kvw_, kvw_, kvw_, kvw_, kvw_, kvw_, hh * N_BRANCH,
                           N_MEM_HEADS * HEAD_DIM])
            wq, wkc, wvc, wks, wvs, wkw, wvw, wgl, wqm = [w[:, c[k]:c[k + 1]] for k in range(9)]
            n_g = r * N_BRANCH
            wgl = jnp.concatenate([_pad_cols(wgl[:, gg * n_g:(gg + 1) * n_g], LANE) for gg in range(g)], 1)
            w_all = jnp.concatenate([wq, wkc, wvc, _interleave_kv(wks, wvs, g),
                                     _interleave_kv(wkw, wvw, g), wgl, wqm], axis=1)
            qt, cmp_raw, kvs, vst, kvw, vwt, glt, qmem = _nsa_proj(h, norm_mix[layer], w_all.astype(BF16), tm)
            nc = t // CMP_STRIDE
            x2 = cmp_raw.reshape(bsz, 2 * g, nc, CMP_STRIDE * HEAD_DIM)
            pos = jnp.stack([nsa_cmp_pos_k[j], nsa_cmp_pos_v[j]]).reshape(2, 2, CMP_STRIDE * HEAD_DIM)
            w1 = jnp.stack([nsa_cmp_k_w1[j], nsa_cmp_v_w1[j]]).astype(BF16)
            b1 = jnp.stack([nsa_cmp_k_b1[j], nsa_cmp_v_b1[j]])[:, None, :]
            w2 = jnp.stack([jnp.pad(nsa_cmp_k_w2[j], ((0, 0), (0, HEAD_DIM))),
                            jnp.pad(nsa_cmp_v_w2[j], ((0, 0), (HEAD_DIM, 0)))]).astype(BF16)
            b2 = jnp.stack([jnp.pad(nsa_cmp_k_b2[j], (0, HEAD_DIM)),
                            jnp.pad(nsa_cmp_v_b2[j], (HEAD_DIM, 0))])[:, None, :]
            kvc = _compress(x2, pos, w1, b1, w2, b2)
            kvc = jnp.pad(kvc, ((0, 0), (0, 0), (0, CMP_PAD - nc), (0, 0)))
            vct = jnp.transpose(kvc, (0, 1, 3, 2))
            gb = jnp.pad(nsa_gate_b[j].reshape(g, n_g), ((0, 0), (0, GATE_ROWS - n_g)))
            gbt = jnp.broadcast_to(gb[:, :, None], (g, GATE_ROWS, TQ))
            mix = _nsa_attention(qt, kvc, vct, kvs, vst, kvw, vwt, glt, gbt, tab_nsa, tabc, ovl,
                                 t // SEL_BLOCK)
            w_uv = None
        else:
            w = dsa_w_in[j]
            c = np.cumsum([0, Q_LORA, KV_LORA, IDX_DIM, IDX_HEADS, N_MEM_HEADS * HEAD_DIM])
            wcq, wckv, wki, wwi, wqm = [w[:, c[k]:c[k + 1]] for k in range(5)]
            w_all = jnp.concatenate([wcq, wckv, _pad_cols(wki, LANE), _pad_cols(wwi, LANE),
                                     wqm], axis=1).astype(BF16)
            wuk = jnp.transpose(dsa_w_uk[j], (1, 2, 0))
            wuk = jnp.stack([jnp.pad(wuk[hd], ((HEAD_DIM * (hd % 2), HEAD_DIM * (1 - hd % 2)), (0, 0)))
                             for hd in range(hh)]).astype(BF16)
            qat, qit, wit, ckv, ckvt, kidx, qmem = _dsa_proj(
                h, norm_mix[layer][None], w_all, dsa_q_norm[j][None], dsa_kv_norm[j][None],
                _pad_cols(dsa_kidx_norm[j][None], LANE), dsa_w_q_up[j].astype(BF16), wuk,
                dsa_w_q_idx[j].astype(BF16), tm)
            mix = _dsa_attention(qit, wit, qat, kidx, ckv, ckvt, tab, min(DSA_TOPK, t // 4))
            wv = jnp.transpose(dsa_w_uv[j], (1, 0, 2))
            w_uv = (jnp.eye(hh, dtype=F32)[:, None, :, None] * wv[:, :, None, :]
                    ).reshape(hh * KV_LORA, d_mix).astype(BF16)

        h = _layer_tail(h, mix, qmem, kvm, w_uv, w_mix, w_mem_o, norm_ffn[layer][None],
                        ffn_gate[layer].astype(BF16), ffn_up[layer].astype(BF16),
                        ffn_down[layer].astype(BF16), norm_final[None], layer == depth - 1, tm)
    return h.astype(x.dtype)
```
